```python
import math
import jax
import jax.numpy as jnp
from jax import lax
import numpy as np

D_MODEL = 1024
BATCH = 32
SEQ = 2048
DEPTH = 4

CTX_LEN = 256
GRID_W = 64
EPS = 1e-6
ROPE_BASE = 10000.0
Q_BLOCK = 128
F32 = jnp.float32
F_MIN = 1e-30
LB_MAX = 1.0 - 1e-6

A_HEADS = 4
A_DK = 128
A_DV = 128
A_KEY = A_HEADS * A_DK
A_VAL = A_HEADS * A_DV
HGRN_CHUNK = 32

B_HEADS = 4
B_HD = 64
B_QK = B_HEADS * 2 * B_HD
B_VAL = B_HEADS * 2 * B_HD

C_HEADS = 8
C_NOPE = 64
C_ROPE = 32
C_V = 64
C_Q_LORA = 384
C_KV_LORA = 256
C_VAL = C_HEADS * C_V

N_BRANCH = 3
BRANCH_W = 512

N_EXPERTS = 16
EXPERT_FF = 2048
CAPACITY_FACTOR = 2

IN_SPLITS = (A_KEY, A_VAL, A_VAL, A_KEY, A_KEY,
             B_QK, B_QK, B_VAL,
             C_Q_LORA, C_KV_LORA, C_ROPE,
             N_BRANCH * D_MODEL)
N_IN = sum(IN_SPLITS)

kernel_name = 'hybrid_dit_hgrn2_diffattn_mla_ecmoe'


def rmsnorm(x, g):
    xf = x.astype(F32)
    y = xf * lax.rsqrt(jnp.mean(xf * xf, axis=-1, keepdims=True) + EPS)
    return y.astype(x.dtype) * g


def _modulation(vec, w, b):
    m = jax.nn.silu(vec) @ w + b
    return jnp.split(m[..., None, :], 6, axis=-1)


def _modulate(h, shift, scale):
    return h * (1.0 + scale) + shift


def _split_cols(a):
    offs, acc = [], 0
    for w in IN_SPLITS[:-1]:
        acc += w
        offs.append(acc)
    return jnp.split(a, offs, axis=-1)


def rope_2d(rows, rot_dim):
    row = jnp.repeat(jnp.arange(rows, dtype=F32), GRID_W)
    col = jnp.tile(jnp.arange(GRID_W, dtype=F32), rows)
    n_freq = rot_dim // 4
    inv_freq = ROPE_BASE ** (-jnp.arange(n_freq, dtype=F32) / n_freq)
    ang = jnp.concatenate([row[:, None] * inv_freq, col[:, None] * inv_freq], axis=-1)
    return jnp.cos(ang), jnp.sin(ang)


def apply_rope(x, cos, sin):
    shape = (cos.shape[0],) + (1,) * (x.ndim - 3) + (cos.shape[1],)
    c = cos.reshape(shape).astype(x.dtype)
    s = sin.reshape(shape).astype(x.dtype)
    x1, x2 = jnp.split(x, 2, axis=-1)
    return jnp.concatenate([x1 * c - x2 * s, x1 * s + x2 * c], axis=-1)


def _query_blocks(fn, *qs):
    B, T = qs[0].shape[:2]
    nb = T // Q_BLOCK
    blocks = tuple(jnp.moveaxis(q.reshape(B, nb, Q_BLOCK, *q.shape[2:]), 1, 0) for q in qs)
    out = lax.map(lambda bl: fn(*bl), blocks)
    return jnp.moveaxis(out, 0, 1).reshape(B, T, *out.shape[3:])


def softmax_attention(q, k, v, scale):
    def block(qb):
        s = jnp.einsum('bqhd,bkhd->bhqk', qb, k).astype(F32) * scale
        p = jax.nn.softmax(s, axis=-1).astype(v.dtype)
        return jnp.einsum('bhqk,bkhd->bqhd', p, v)
    return _query_blocks(block, q)


def diff_attention(q1, q2, k1, k2, v, lam, scale):
    def block(q1b, q2b):
        p1 = jax.nn.softmax(jnp.einsum('bqhd,bkhd->bhqk', q1b, k1).astype(F32) * scale, axis=-1)
        p2 = jax.nn.softmax(jnp.einsum('bqhd,bkhd->bhqk', q2b, k2).astype(F32) * scale, axis=-1)
        return jnp.einsum('bhqk,bkhd->bqhd', (p1 - lam * p2).astype(v.dtype), v)
    return _query_blocks(block, q1, q2)


def hgrn2_chunk_scan(q, k, v, log_f, s0):
    B, T, H, DK = q.shape
    DV = v.shape[-1]
    n = T // HGRN_CHUNK

    def chunks(a):
        return jnp.moveaxis(a.reshape(B, n, HGRN_CHUNK, H, a.shape[-1]), 1, 0)

    causal = jnp.tril(jnp.ones((HGRN_CHUNK, HGRN_CHUNK), dtype=bool))[None, :, :, None, None]

    def step(state, inp):
        qc, kc, vc, lf = inp
        b = jnp.cumsum(lf, axis=1)
        o_inter = jnp.einsum('blhk,bhkv->blhv', qc * jnp.exp(b), state)
        diff = jnp.where(causal, b[:, :, None] - b[:, None, :], 0.0)
        decay = jnp.where(causal, jnp.exp(diff), 0.0)
        scores = jnp.einsum('btshk,bshk->bhts', qc[:, :, None] * decay, kc)
        o_intra = jnp.einsum('bhts,bshv->bthv', scores, vc)
        b_last = b[:, -1]
        state = (jnp.exp(b_last)[..., None] * state
                 + jnp.einsum('bshk,bshv->bhkv', kc * jnp.exp(b_last[:, None] - b), vc))
        return state, o_inter + o_intra

    s_final, o = lax.scan(step, s0, (chunks(q), chunks(k), chunks(v), chunks(log_f)))
    return jnp.moveaxis(o, 0, 1).reshape(B, T, H, DV), s_final


def _flip(a, rev):
    return a[:, ::-1] if rev else a


def hgrn2_branch(cols_l, cols_c, lower_bound, norm_g, with_ctx):
    dtype = cols_l[0].dtype
    lb = lower_bound.reshape(2, A_HEADS, A_DK)

    def prep(q, i, f, lbd):
        B, T = q.shape[:2]
        z = f.reshape(B, T, A_HEADS, A_DK).astype(F32)
        k = (1.0 - lbd) * jax.nn.sigmoid(-z)
        fg = lbd + (1.0 - lbd) * jax.nn.sigmoid(z)
        log_f = jnp.log(jnp.maximum(fg, F_MIN))
        return (q.reshape(B, T, A_HEADS, A_DK).astype(F32), k,
                i.reshape(B, T, A_HEADS, A_DV).astype(F32), log_f)

    B = cols_c[0].shape[0]
    o_l, o_c = 0.0, 0.0
    for d in range(2):
        rev = d == 1
        ctx_in = tuple(_flip(a, rev) for a in prep(cols_c[0], cols_c[1], cols_c[3 + d], lb[d]))
        lat_in = tuple(_flip(a, rev) for a in prep(cols_l[0], cols_l[1], cols_l[3 + d], lb[d]))
        s0 = jnp.zeros((B, A_HEADS, A_DK, A_DV), F32)
        oc, s_ctx = hgrn2_chunk_scan(*ctx_in, s0)
        ol, _ = hgrn2_chunk_scan(*lat_in, s_ctx)
        o_l = o_l + _flip(ol, rev)
        o_c = o_c + _flip(oc, rev)

    def readout(o, g):
        B_, T = o.shape[:2]
        o = rmsnorm(o, norm_g.reshape(A_HEADS, A_DV).astype(F32))
        return (o.reshape(B_, T, A_VAL) * jax.nn.silu(g.astype(F32))).astype(dtype)

    return readout(o_l, cols_l[2]), (readout(o_c, cols_c[2]) if with_ctx else None)


def diff_branch(cols_l, cols_c, rope, lam_params, norm_g, layer, with_ctx):
    lam_init = 0.8 - 0.6 * math.exp(-0.3 * layer)
    lp = lam_params.astype(F32)
    lam = jnp.exp(jnp.sum(lp[0] * lp[1])) - jnp.exp(jnp.sum(lp[2] * lp[3])) + lam_init
    scale = B_HD ** -0.5

    def heads(q, k, v):
        B, T = q.shape[:2]
        return (q.reshape(B, T, B_HEADS, 2, B_HD), k.reshape(B, T, B_HEADS, 2, B_HD),
                v.reshape(B, T, B_HEADS, 2 * B_HD))

    ql, kl, vl = heads(*cols_l)
    qc, kc, vc = heads(*cols_c)
    cos, sin = rope
    ql = apply_rope(ql, cos, sin)
    kl = apply_rope(kl, cos, sin)
    k_all = jnp.concatenate([kc, kl], axis=1)
    v_all = jnp.concatenate([vc, vl], axis=1)

    def post(o):
        B, T = o.shape[:2]
        o = rmsnorm(o, norm_g.reshape(B_HEADS, 2 * B_HD)) * (1.0 - lam_init)
        return o.reshape(B, T, B_VAL)

    o_l = post(diff_attention(ql[..., 0, :], ql[..., 1, :], k_all[..., 0, :], k_all[..., 1, :],
                              v_all, lam, scale))
    o_c = None
    if with_ctx:
        o_c = post(diff_attention(qc[..., 0, :], qc[..., 1, :], kc[..., 0, :], kc[..., 1, :],
                                  vc, lam, scale))
    return o_l, o_c


def mla_branch(cols_l, cols_c, rope, q_norm_g, w_uq, kv_norm_g, w_ukv, with_ctx):
    cos, sin = rope

    def project(cq, ckv, kr, rotate):
        B, T = cq.shape[:2]
        q = (rmsnorm(cq, q_norm_g) @ w_uq).reshape(B, T, C_HEADS, C_NOPE + C_ROPE)
        kv = (rmsnorm(ckv, kv_norm_g) @ w_ukv).reshape(B, T, C_HEADS, C_NOPE + C_V)
        q_nope, q_rope = q[..., :C_NOPE], q[..., C_NOPE:]
        k_nope, v = kv[..., :C_NOPE], kv[..., C_NOPE:]
        k_rope = kr[:, :, None, :]
        if rotate:
            q_rope = apply_rope(q_rope, cos, sin)
            k_rope = apply_rope(k_rope, cos, sin)
        q = jnp.concatenate([q_nope, q_rope], axis=-1)
        k = jnp.concatenate([k_nope, jnp.broadcast_to(k_rope, (B, T, C_HEADS, C_ROPE))], axis=-1)
        return q, k, v

    ql, kl, vl = project(*cols_l, True)
    qc, kc, vc = project(*cols_c, False)
    scale = (C_NOPE + C_ROPE) ** -0.5
    o_l = softmax_attention(ql, jnp.concatenate([kc, kl], axis=1), jnp.concatenate([vc, vl], axis=1), scale)
    o_l = o_l.reshape(*o_l.shape[:2], C_VAL)
    o_c = None
    if with_ctx:
        o_c = softmax_attention(qc, kc, vc, scale)
        o_c = o_c.reshape(*o_c.shape[:2], C_VAL)
    return o_l, o_c


def merge_branches(branches, gate_cols, w_branch, w_out):
    B, T = gate_cols.shape[:2]
    o = jnp.stack(branches, axis=2)
    y = jnp.einsum('btnc,ncd->btnd', o, w_branch)
    g = jax.nn.sigmoid(gate_cols.astype(F32)).astype(y.dtype).reshape(B, T, N_BRANCH, D_MODEL)
    return jnp.sum(g * y, axis=2) @ w_out


def token_mixer(h_l, h_c, layer, lower_bound, rope_b, rope_c, w_in, hgrn_norm_g, diff_lambda,
                diff_norm_g, mla_q_norm_g, mla_w_uq, mla_kv_norm_g, mla_w_ukv, w_branch, w_out, with_ctx):
    cl = _split_cols(h_l @ w_in)
    cc = _split_cols(h_c @ w_in)
    a_l, a_c = hgrn2_branch(cl[0:5], cc[0:5], lower_bound, hgrn_norm_g, with_ctx)
    b_l, b_c = diff_branch(cl[5:8], cc[5:8], rope_b, diff_lambda, diff_norm_g, layer, with_ctx)
    m_l, m_c = mla_branch(cl[8:11], cc[8:11], rope_c, mla_q_norm_g, mla_w_uq, mla_kv_norm_g, mla_w_ukv, with_ctx)
    out_l = merge_branches((a_l, b_l, m_l), cl[11], w_branch, w_out)
    out_c = merge_branches((a_c, b_c, m_c), cc[11], w_branch, w_out) if with_ctx else None
    return out_l, out_c


def expert_choice_ffn(h, router_w, w_gate, w_up, w_down):
    B, T, D = h.shape
    cap = CAPACITY_FACTOR * T // N_EXPERTS
    affinity = jax.nn.softmax((h @ router_w).astype(F32), axis=-1)
    g, idx = lax.top_k(jnp.swapaxes(affinity, 1, 2), cap)
    xs = jax.vmap(lambda hb, ib: hb[ib])(h, idx)
    a = jnp.einsum('becd,edf->becf', xs, w_gate)
    u = jnp.einsum('becd,edf->becf', xs, w_up)
    y = jnp.einsum('becf,efd->becd', jax.nn.silu(a) * u, w_down) * g[..., None].astype(h.dtype)
    return jax.vmap(lambda ib, yb: jnp.zeros((T, D), yb.dtype).at[ib].add(yb))(idx, y)


def setup_inputs(seed: int = 0) -> dict:
    key = jax.random.key(seed)
    ks = jax.random.split(key, 24)
    L, D = DEPTH, D_MODEL

    def nrm(k, shape, scale):
        return jax.random.normal(k, shape, F32) * scale

    return {
        'x': nrm(ks[0], (BATCH, SEQ, D), 1.0),
        'c': nrm(ks[1], (BATCH, D), 1.0),
        'ctx': nrm(ks[2], (BATCH, CTX_LEN, D), 1.0),
        'c_ctx': nrm(ks[3], (D,), 1.0),
        'ada_w': nrm(ks[4], (L, D, 6 * D), 0.5 * D ** -0.5),
        'ada_b': nrm(ks[5], (L, 6 * D), 0.02),
        'norm_mix_g': 1.0 + nrm(ks[6], (L, D), 0.02),
        'norm_ffn_g': 1.0 + nrm(ks[7], (L, D), 0.02),
        'w_in': nrm(ks[8], (L, D, N_IN), D ** -0.5),
        'hgrn_lb': nrm(ks[9], (L, 2, A_KEY), 0.5),
        'hgrn_norm_g': 1.0 + nrm(ks[10], (L, A_VAL), 0.02),
        'diff_lambda': nrm(ks[11], (L, 4, B_HD), 0.1),
        'diff_norm_g': 1.0 + nrm(ks[12], (L, B_VAL), 0.02),
        'mla_q_norm_g': 1.0 + nrm(ks[13], (L, C_Q_LORA), 0.02),
        'mla_w_uq': nrm(ks[14], (L, C_Q_LORA, C_HEADS * (C_NOPE + C_ROPE)), C_Q_LORA ** -0.5),
        'mla_kv_norm_g': 1.0 + nrm(ks[15], (L, C_KV_LORA), 0.02),
        'mla_w_ukv': nrm(ks[16], (L, C_KV_LORA, C_HEADS * (C_NOPE + C_V)), C_KV_LORA ** -0.5),
        'w_branch': nrm(ks[17], (L, N_BRANCH, BRANCH_W, D), BRANCH_W ** -0.5),
        'w_out': nrm(ks[18], (L, D, D), D ** -0.5),
        'router_w': nrm(ks[19], (L, D, N_EXPERTS), D ** -0.5),
        'exp_w_gate': nrm(ks[20], (L, N_EXPERTS, D, EXPERT_FF), D ** -0.5),
        'exp_w_up': nrm(ks[21], (L, N_EXPERTS, D, EXPERT_FF), D ** -0.5),
        'exp_w_down': nrm(ks[22], (L, N_EXPERTS, EXPERT_FF, D), EXPERT_FF ** -0.5),
        'final_norm_g': 1.0 + nrm(ks[23], (D,), 0.02),
    }


def reference(x, c, ctx, c_ctx, ada_w, ada_b, norm_mix_g, norm_ffn_g, w_in, hgrn_lb, hgrn_norm_g,
              diff_lambda, diff_norm_g, mla_q_norm_g, mla_w_uq, mla_kv_norm_g, mla_w_ukv, w_branch,
              w_out, router_w, exp_w_gate, exp_w_up, exp_w_down, final_norm_g):
    rows = x.shape[1] // GRID_W
    rope_b = rope_2d(rows, B_HD)
    rope_c = rope_2d(rows, C_ROPE)
    p = jax.nn.softmax(hgrn_lb.astype(F32), axis=0)
    lower_bounds = jnp.clip(jnp.cumsum(p, axis=0) - p[0], 0.0, LB_MAX)
    xc = ctx
    for layer in range(DEPTH):
        with_ctx = layer < DEPTH - 1
        sh_m, sc_m, g_m, sh_f, sc_f, g_f = _modulation(c, ada_w[layer], ada_b[layer])
        csh_m, csc_m, cg_m, csh_f, csc_f, cg_f = _modulation(c_ctx, ada_w[layer], ada_b[layer])
        h_l = _modulate(rmsnorm(x, norm_mix_g[layer]), sh_m, sc_m)
        h_c = _modulate(rmsnorm(xc, norm_mix_g[layer]), csh_m, csc_m)
        mix_l, mix_c = token_mixer(h_l, h_c, layer, lower_bounds[layer], rope_b, rope_c, w_in[layer],
                                   hgrn_norm_g[layer], diff_lambda[layer], diff_norm_g[layer],
                                   mla_q_norm_g[layer], mla_w_uq[layer], mla_kv_norm_g[layer],
                                   mla_w_ukv[layer], w_branch[layer], w_out[layer], with_ctx)
        x = x + g_m * mix_l
        h = _modulate(rmsnorm(x, norm_ffn_g[layer]), sh_f, sc_f)
        x = x + g_f * expert_choice_ffn(h, router_w[layer], exp_w_gate[layer], exp_w_up[layer], exp_w_down[layer])
        if with_ctx:
            xc = xc + cg_m * mix_c
            hc = _modulate(rmsnorm(xc, norm_ffn_g[layer]), csh_f, csc_f)
            xc = xc + cg_f * expert_choice_ffn(hc, router_w[layer], exp_w_gate[layer], exp_w_up[layer], exp_w_down[layer])
    return rmsnorm(x, final_norm_g)
```

```python
import functools
import math

import jax
import jax.numpy as jnp
from jax import lax
from jax.experimental import pallas as pl
from jax.experimental.pallas import tpu as pltpu

F32 = jnp.float32
BF16 = jnp.bfloat16

EPS = 1e-6
ROPE_BASE = 10000.0
GRID_W = 64
F_MIN = 1e-30
LB_MAX = 1.0 - 1e-6

A_HEADS, A_DK = 4, 128
B_HEADS, B_HD = 4, 64
C_HEADS, C_NOPE, C_ROPE, C_V = 8, 64, 32, 64
C_Q_LORA, C_KV_LORA = 384, 256
N_BRANCH, BRANCH_W = 3, 512
N_EXPERTS = 16
CAPACITY_FACTOR = 2

LANE = 128
SUBLANE = 8
VMEM_LIMIT = 56 * 1024 * 1024

HGRN_W = 5 * 512
MLA_W = 768
MLA_KV_IN = 384
GATE_W = N_BRANCH * 1024
HGRN_CHUNK = 128


def _params(*sem):
    return pltpu.CompilerParams(dimension_semantics=sem, vmem_limit_bytes=VMEM_LIMIT)


def _resident(shape, index_map):
    return pl.BlockSpec(shape, index_map, pipeline_mode=pl.Buffered(1))


def _rms(x, eps=EPS):
    return x * lax.rsqrt(jnp.mean(x * x, axis=-1, keepdims=True) + eps)


def _dot(a, b):
    return jnp.dot(a, b, preferred_element_type=F32)


def _dot_nt(a, b):
    return lax.dot_general(a, b, (((1,), (1,)), ((), ())), preferred_element_type=F32)


def _dot_tn(a, b):
    return lax.dot_general(a, b, (((0,), (0,)), ((), ())), preferred_element_type=F32)


def _mods_kernel(v_ref, w_ref, b_ref, o_ref):
    v = v_ref[...]
    s = (v * jax.nn.sigmoid(v)).astype(BF16)
    o_ref[0] = _dot(s, w_ref[0].astype(BF16)) + b_ref[0]


def _modulation(vecs, ada_w, ada_b):
    L, D, N = ada_w.shape
    R = vecs.shape[0]
    tn = 1536
    return pl.pallas_call(
        _mods_kernel,
        grid=(L, N // tn),
        in_specs=[pl.BlockSpec((R, D), lambda l, j: (0, 0)),
                  pl.BlockSpec((1, D, tn), lambda l, j: (l, 0, j)),
                  pl.BlockSpec((1, 1, tn), lambda l, j: (l, 0, j))],
        out_specs=pl.BlockSpec((1, R, tn), lambda l, j: (l, 0, j)),
        out_shape=jax.ShapeDtypeStruct((L, R, N), F32),
        compiler_params=_params("parallel", "parallel"),
        name="adaln_mods",
    )(vecs, ada_w, ada_b.reshape(L, 1, N))


def _rope_apply(x, cos, sin_a, sin_b, half):
    return (x * cos + pltpu.roll(x, LANE - half, axis=1) * sin_a + pltpu.roll(x, half, axis=1) * sin_b)


def _layer_in_kernel(x_ref, g_ref, mod_ref, w_ref, cq_ref, saq_ref, sbq_ref, ck_ref, sak_ref, sbk_ref,
                     hg_ref, dqk_ref, dv_ref, mla_ref, gate_ref):
    x = x_ref[0]
    mod = mod_ref[0]
    h = (_rms(x) * g_ref[...]) * (1.0 + mod[1:2]) + mod[0:1]
    h = h.astype(BF16)
    c0 = 0
    for j in range(HGRN_W // 512):
        hg_ref[0, :, j * 512:(j + 1) * 512] = _dot(h, w_ref[:, c0:c0 + 512])
        c0 += 512
    for j in range(2):
        acc = _dot(h, w_ref[:, c0:c0 + 512])
        cos, sa, sb = (cq_ref, saq_ref, sbq_ref) if j == 0 else (ck_ref, sak_ref, sbk_ref)
        for g in range(4):
            xg = acc[:, g * LANE:(g + 1) * LANE]
            dqk_ref[0, :, j * 512 + g * LANE:j * 512 + (g + 1) * LANE] = _rope_apply(
                xg, cos[...], sa[...], sb[...], B_HD // 2).astype(BF16)
        c0 += 512
    dv_ref[0] = _dot(h, w_ref[:, c0:c0 + 512]).astype(BF16)
    c0 += 512
    mla_ref[0] = _dot(h, w_ref[:, c0:c0 + MLA_W])
    c0 += MLA_W
    for j in range(GATE_W // 512):
        gate_ref[0, :, j * 512:(j + 1) * 512] = jax.nn.sigmoid(_dot(h, w_ref[:, c0:c0 + 512])).astype(BF16)
        c0 += 512


def _layer_in(x, norm_g, mods, mod_row, w_perm, tabs_q, tabs_k, tm):
    Bx, Tx, D = x.shape
    NW = w_perm.shape[1]
    row = lambda b, i: (b, i, 0)
    tab = pl.BlockSpec((tm, LANE), lambda b, i: (i, 0))
    outs = [(HGRN_W, F32), (1024, BF16), (512, BF16), (MLA_W, F32), (GATE_W, BF16)]
    return pl.pallas_call(
        _layer_in_kernel,
        grid=(Bx, Tx // tm),
        in_specs=[pl.BlockSpec((1, tm, D), row),
                  pl.BlockSpec((1, D), lambda b, i: (0, 0)),
                  pl.BlockSpec((1, 6, D), lambda b, i: (mod_row(b), 0, 0)),
                  _resident((D, NW), lambda b, i: (0, 0)),
                  tab, tab, tab, tab, tab, tab],
        out_specs=[pl.BlockSpec((1, tm, w), row) for w, _ in outs],
        out_shape=[jax.ShapeDtypeStruct((Bx, Tx, w), dt) for w, dt in outs],
        compiler_params=_params("parallel", "parallel"),
        name="layer_in",
    )(x, norm_g.reshape(1, D), mods, w_perm, *tabs_q, *tabs_k)


def _cumsum_rows(x, reverse):
    n = x.shape[0]
    s = 1
    while s < n:
        if s < SUBLANE:
            row = lax.broadcasted_iota(jnp.int32, x.shape, 0)
            if reverse:
                sh = jnp.where(row < n - s, pltpu.roll(x, n - s, axis=0), 0.0)
            else:
                sh = jnp.where(row >= s, pltpu.roll(x, s, axis=0), 0.0)
        else:
            z = jnp.zeros((s, x.shape[1]), x.dtype)
            sh = jnp.concatenate([x[s:], z], axis=0) if reverse else jnp.concatenate([z, x[:n - s]], axis=0)
        x = x + sh
        s *= 2
    return x


def _hgrn_chunk(q, k_in, v, lf, st, reverse):
    C = q.shape[0]
    b = _cumsum_rows(lf, reverse)
    b_last = b[0:1] if reverse else b[C - 1:C]
    qd = (q * jnp.exp(b)).astype(BF16)
    o = _dot_nt(qd, st.astype(BF16))
    kd = (k_in * jnp.exp(b_last - b)).astype(BF16)
    vb = v.astype(BF16)
    st_new = st * jnp.exp(b_last) + _dot_tn(vb, kd)

    row = lax.broadcasted_iota(jnp.int32, (C, C), 0)
    col = lax.broadcasted_iota(jnp.int32, (C, C), 1)
    scores = jnp.zeros((C, C), F32)
    m = C // 2
    while m >= SUBLANE:
        nb = C // (2 * m)
        b3 = b.reshape(nb, 2 * m, LANE)
        anchor = b3[:, m:m + 1, :] if reverse else b3[:, m - 1:m, :]
        e = jnp.exp(-jnp.abs(b3 - anchor)).reshape(C, LANE)
        r1 = lax.broadcasted_iota(jnp.int32, (C, 1), 0)
        second = (r1 % (2 * m)) >= m
        late = jnp.logical_not(second) if reverse else second
        qh = jnp.where(late, q * e, 0.0).astype(BF16)
        kh = jnp.where(late, 0.0, k_in * e).astype(BF16)
        same = (row // (2 * m)) == (col // (2 * m))
        scores = scores + jnp.where(same, _dot_nt(qh, kh), 0.0)
        m //= 2
    o = o + _dot(scores.astype(BF16), vb)

    nb = C // SUBLANE
    b3 = b.reshape(nb, SUBLANE, LANE)
    q3 = q.reshape(nb, SUBLANE, LANE)
    k3 = k_in.reshape(nb, SUBLANE, LANE)
    v3 = v.reshape(nb, SUBLANE, LANE)
    t3 = lax.broadcasted_iota(jnp.int32, (nb, SUBLANE, 1), 1)
    od = jnp.zeros((nb, SUBLANE, LANE), F32)
    for j in range(SUBLANE):
        e = jnp.exp(jnp.minimum(b3 - b3[:, j:j + 1, :], 0.0))
        w = q3 * e * k3[:, j:j + 1, :]
        valid = (t3 <= j) if reverse else (t3 >= j)
        sc = jnp.sum(jnp.where(valid, w, 0.0), axis=-1, keepdims=True)
        od = od + sc * v3[:, j:j + 1, :]
    return o + od.reshape(C, LANE), st_new


def _hgrn_kernel(*refs, with_ctx_out):
    (ql, il, gl, ffl, fbl, qc, ic, gc, ffc, fbc, lb_ref, ng_ref) = refs[:12]
    if with_ctx_out:
        ol_ref, oc_ref, accl, accc = refs[12:]
    else:
        ol_ref, accl, accc = refs[12:]
        oc_ref = None
    C = HGRN_CHUNK
    nl = ql.shape[1] // C
    nc = qc.shape[1] // C

    for d in range(2):
        reverse = d == 1
        lbd = lb_ref[d:d + 1, :]
        one_m = 1.0 - lbd

        def make_step(qr, ir, fr, acc, n, first):
            def step(s, st):
                c = (n - 1 - s) if reverse else s
                r0 = pl.multiple_of(c * C, C)
                z = fr[0, pl.ds(r0, C), :]
                sg = jax.nn.sigmoid(z)
                kk = one_m * (1.0 - sg)
                lf = jnp.log(jnp.maximum(lbd + one_m * sg, F_MIN))
                o, st = _hgrn_chunk(qr[0, pl.ds(r0, C), :], kk, ir[0, pl.ds(r0, C), :], lf, st, reverse)
                if first:
                    acc[pl.ds(r0, C), :] = o
                else:
                    acc[pl.ds(r0, C), :] += o
                return st
            return step

        st = jnp.zeros((LANE, LANE), F32)
        st = lax.fori_loop(0, nc, make_step(qc, ic, fbc if reverse else ffc, accc, nc, d == 0), st)
        lax.fori_loop(0, nl, make_step(ql, il, fbl if reverse else ffl, accl, nl, d == 0), st)

    def readout(acc, g_ref, o_ref):
        o = _rms(acc[...]) * ng_ref[...]
        g = g_ref[0]
        o_ref[0] = (o * (g * jax.nn.sigmoid(g))).astype(BF16)

    readout(accl, gl, ol_ref)
    if with_ctx_out:
        readout(accc, gc, oc_ref)


def _hgrn(hg_l, hg_c, lb, norm_g, with_ctx_out):
    B, T, _ = hg_l.shape
    Tc = hg_c.shape[1]
    nh = A_HEADS

    def col(n, t):
        return [pl.BlockSpec((1, t, LANE), (lambda b, h, j=j: (b, 0, j * nh + h))) for j in range(n)]

    in_specs = col(5, T) + col(5, Tc) + [pl.BlockSpec((2, LANE), lambda b, h: (0, h)),
                                         pl.BlockSpec((1, LANE), lambda b, h: (0, h))]
    out_specs = [pl.BlockSpec((1, T, LANE), lambda b, h: (b, 0, h))]
    out_shape = [jax.ShapeDtypeStruct((B, T, nh * LANE), BF16)]
    if with_ctx_out:
        out_specs.append(pl.BlockSpec((1, Tc, LANE), lambda b, h: (b, 0, h)))
        out_shape.append(jax.ShapeDtypeStruct((B, Tc, nh * LANE), BF16))
    res = pl.pallas_call(
        functools.partial(_hgrn_kernel, with_ctx_out=with_ctx_out),
        grid=(B, nh),
        in_specs=in_specs,
        out_specs=out_specs,
        out_shape=out_shape,
        scratch_shapes=[pltpu.VMEM((T, LANE), F32), pltpu.VMEM((Tc, LANE), F32)],
        compiler_params=_params("parallel", "parallel"),
        name="hgrn2",
    )(*([hg_l] * 5), *([hg_c] * 5), lb, norm_g.reshape(1, -1))
    return (res[0], res[1]) if with_ctx_out else (res[0], None)


def _softmax_parts(q, ks):
    ss = [_dot_nt(q, k) for k in ks]
    m = ss[0].max(axis=-1, keepdims=True)
    for s in ss[1:]:
        m = jnp.maximum(m, s.max(axis=-1, keepdims=True))
    ps = [jnp.exp(s - m) for s in ss]
    l = ps[0].sum(axis=-1, keepdims=True)
    for p in ps[1:]:
        l = l + p.sum(axis=-1, keepdims=True)
    return ps, 1.0 / l


def _diff_attn_kernel(*refs, n_kv, lam_init):
    q_ref, lam_ref, ng_ref = refs[0], refs[1], refs[2]
    kv = refs[3:3 + 2 * n_kv]
    o_ref = refs[3 + 2 * n_kv]
    lp = lam_ref[...]
    lam = (jnp.exp(jnp.sum(lp[0:1] * lp[1:2], axis=-1, keepdims=True))
           - jnp.exp(jnp.sum(lp[2:3] * lp[3:4], axis=-1, keepdims=True)) + lam_init)
    q = q_ref[0]
    lane = lax.broadcasted_iota(jnp.int32, q.shape, 1)
    zero = jnp.zeros_like(q)
    q1 = jnp.where(lane < B_HD, q, zero)
    q2 = jnp.where(lane < B_HD, zero, q)
    ks = [kv[2 * i][0] for i in range(n_kv)]
    p1, r1 = _softmax_parts(q1, ks)
    p2, r2 = _softmax_parts(q2, ks)
    r2 = r2 * lam
    o = None
    for i in range(n_kv):
        w = (p1[i] * r1 - p2[i] * r2).astype(BF16)
        c = _dot(w, kv[2 * i + 1][0])
        o = c if o is None else o + c
    o_ref[0] = (_rms(o) * ng_ref[...] * (1.0 - lam_init)).astype(BF16)


def _diff_attn(q_src, kvs, lam_params, norm_g, lam_init, tq):
    B, Tq, _ = q_src.shape
    nh = B_HEADS
    in_specs = [pl.BlockSpec((1, tq, LANE), lambda b, h, i: (b, i, h)),
                pl.BlockSpec((4, B_HD), lambda b, h, i: (0, 0)),
                pl.BlockSpec((1, LANE), lambda b, h, i: (0, h))]
    args = [q_src, lam_params, norm_g.reshape(1, -1)]
    for k_arr, v_arr in kvs:
        Tk = k_arr.shape[1]
        in_specs.append(pl.BlockSpec((1, Tk, LANE), lambda b, h, i: (b, 0, nh + h)))
        in_specs.append(pl.BlockSpec((1, Tk, LANE), lambda b, h, i: (b, 0, h)))
        args += [k_arr, v_arr]
    return pl.pallas_call(
        functools.partial(_diff_attn_kernel, n_kv=len(kvs), lam_init=lam_init),
        grid=(B, nh, Tq // tq),
        in_specs=in_specs,
        out_specs=pl.BlockSpec((1, tq, LANE), lambda b, h, i: (b, i, h)),
        out_shape=jax.ShapeDtypeStruct((B, Tq, nh * LANE), BF16),
        compiler_params=_params("parallel", "parallel", "parallel"),
        name="diff_attn",
    )(*args)


def _mla_attn_kernel(*refs, n_kv):
    q_ref = refs[0]
    kv = refs[1:1 + 2 * n_kv]
    o_ref = refs[1 + 2 * n_kv]
    outs = []
    for hh in range(2):
        q = q_ref[0, :, hh * LANE:(hh + 1) * LANE]
        ks = [kv[2 * i][0, :, hh * LANE:(hh + 1) * LANE] for i in range(n_kv)]
        ps, r = _softmax_parts(q, ks)
        o = None
        for i in range(n_kv):
            c = _dot(ps[i].astype(BF16), kv[2 * i + 1][0])
            o = c if o is None else o + c
        outs.append(o * r)
    lane = lax.broadcasted_iota(jnp.int32, outs[0].shape, 1)
    o_ref[0] = jnp.where(lane < C_V, outs[0], outs[1]).astype(BF16)


def _mla_attn(q, kvs, tq):
    B, Tq, _ = q.shape
    npair = C_HEADS // 2
    in_specs = [pl.BlockSpec((1, tq, 2 * LANE), lambda b, h, i: (b, i, h))]
    args = [q]
    for k_arr, v_arr in kvs:
        Tk = k_arr.shape[1]
        in_specs.append(pl.BlockSpec((1, Tk, 2 * LANE), lambda b, h, i: (b, 0, h)))
        in_specs.append(pl.BlockSpec((1, Tk, LANE), lambda b, h, i: (b, 0, h)))
        args += [k_arr, v_arr]
    return pl.pallas_call(
        functools.partial(_mla_attn_kernel, n_kv=len(kvs)),
        grid=(B, npair, Tq // tq),
        in_specs=in_specs,
        out_specs=pl.BlockSpec((1, tq, LANE), lambda b, h, i: (b, i, h)),
        out_shape=jax.ShapeDtypeStruct((B, Tq, C_HEADS * C_V), BF16),
        compiler_params=_params("parallel", "parallel", "parallel"),
        name="mla_attn",
    )(*args)


def _mla_proj_kernel(x_ref, gq_ref, gkv_ref, wq_ref, wkv_ref, cq_ref, saq_ref, sbq_ref, ck_ref, sak_ref,
                     sbk_ref, q_ref, k_ref, v_ref):
    x = x_ref[0]
    xkv = x[:, :MLA_KV_IN]
    lane = lax.broadcasted_iota(jnp.int32, xkv.shape, 1)
    lat = jnp.where(lane < C_KV_LORA, xkv, 0.0)
    ms = jnp.sum(lat * lat, axis=-1, keepdims=True) * (1.0 / C_KV_LORA)
    hk = jnp.where(lane < C_KV_LORA, lat * lax.rsqrt(ms + EPS) * gkv_ref[...], xkv).astype(BF16)
    hq = (_rms(x[:, MLA_KV_IN:]) * gq_ref[...]).astype(BF16)
    nk = C_HEADS * LANE
    aq = _dot(hq, wq_ref[...])
    akv = _dot(hk, wkv_ref[...])
    for g in range(C_HEADS):
        sl = slice(g * LANE, (g + 1) * LANE)
        q_ref[0, :, sl] = _rope_apply(aq[:, sl], cq_ref[...], saq_ref[...], sbq_ref[...], C_ROPE // 2).astype(BF16)
        k_ref[0, :, sl] = _rope_apply(akv[:, sl], ck_ref[...], sak_ref[...], sbk_ref[...], C_ROPE // 2).astype(BF16)
    v_ref[0] = akv[:, nk:].astype(BF16)


def _mla_proj(mla, gq, gkv, wq, wkv, tabs_q, tabs_k, tm):
    Bx, Tx, _ = mla.shape
    row = lambda b, i: (b, i, 0)
    const = lambda b, i: (0, 0)
    tab = pl.BlockSpec((tm, LANE), lambda b, i: (i, 0))
    nk = C_HEADS * LANE
    outs = [(nk, BF16), (nk, BF16), (C_HEADS * C_V, BF16)]
    return pl.pallas_call(
        _mla_proj_kernel,
        grid=(Bx, Tx // tm),
        in_specs=[pl.BlockSpec((1, tm, MLA_W), row),
                  pl.BlockSpec((1, C_Q_LORA), const),
                  pl.BlockSpec((1, MLA_KV_IN), const),
                  _resident(wq.shape, const),
                  _resident(wkv.shape, const),
                  tab, tab, tab, tab, tab, tab],
        out_specs=[pl.BlockSpec((1, tm, w), row) for w, _ in outs],
        out_shape=[jax.ShapeDtypeStruct((Bx, Tx, w), dt) for w, dt in outs],
        compiler_params=_params("parallel", "parallel"),
        name="mla_proj",
    )(mla, gq, gkv, wq, wkv, *tabs_q, *tabs_k)


def _merge_kernel(a_ref, b_ref, c_ref, gate_ref, wb_ref, wo_ref, x_ref, mod_ref, o_ref):
    D = x_ref.shape[2]
    z = None
    for n, br in enumerate((a_ref, b_ref, c_ref)):
        y = _dot(br[0], wb_ref[n])
        t = gate_ref[0, :, n * D:(n + 1) * D].astype(F32) * y
        z = t if z is None else z + t
    out = _dot(z.astype(BF16), wo_ref[...])
    o_ref[0] = x_ref[0] + mod_ref[0, 2:3] * out


def _merge(a, b, c, gates, wb, wo, x, mods, mod_row, tm):
    Bx, Tx, D = x.shape
    row = lambda bb, i: (bb, i, 0)
    br = pl.BlockSpec((1, tm, BRANCH_W), row)
    return pl.pallas_call(
        _merge_kernel,
        grid=(Bx, Tx // tm),
        in_specs=[br, br, br,
                  pl.BlockSpec((1, tm, GATE_W), row),
                  _resident(wb.shape, lambda bb, i: (0, 0, 0)),
                  _resident(wo.shape, lambda bb, i: (0, 0)),
                  pl.BlockSpec((1, tm, D), row),
                  pl.BlockSpec((1, 6, D), lambda bb, i: (mod_row(bb), 0, 0))],
        out_specs=pl.BlockSpec((1, tm, D), row),
        out_shape=jax.ShapeDtypeStruct((Bx, Tx, D), F32),
        compiler_params=_params("parallel", "parallel"),
        name="merge_out",
    )(a, b, c, gates, wb, wo, x, mods)


def _prefix_count(mask, blk):
    T = mask.shape[0]
    r = lax.broadcasted_iota(jnp.int32, (blk, blk), 0)
    c = lax.broadcasted_iota(jnp.int32, (blk, blk), 1)
    tri = jnp.where(c < r, 1.0, 0.0).astype(BF16)
    parts = []
    carry = jnp.zeros((1, mask.shape[1]), F32)
    for i in range(T // blk):
        mb = mask[i * blk:(i + 1) * blk]
        parts.append(_dot(tri, mb.astype(BF16)) + carry)
        carry = carry + jnp.sum(mb, axis=0, keepdims=True)
    return jnp.concatenate(parts, axis=0) if len(parts) > 1 else parts[0]


def _router_kernel(x_ref, g_ref, mod_ref, rw_ref, h_ref, aff_ref):
    mod = mod_ref[0]
    h = ((_rms(x_ref[0]) * g_ref[...]) * (1.0 + mod[4:5]) + mod[3:4]).astype(BF16)
    h_ref[0] = h
    logits = _dot(h, rw_ref[...])
    lane = lax.broadcasted_iota(jnp.int32, logits.shape, 1)
    lg = jnp.where(lane < N_EXPERTS, logits, -jnp.inf)
    e = jnp.exp(lg - lg.max(axis=-1, keepdims=True))
    aff_ref[0] = e / e.sum(axis=-1, keepdims=True)


def _router(x, norm_g, mods, mod_row, rw, tm):
    Bx, Tx, D = x.shape
    row = lambda b, i: (b, i, 0)
    outs = [(D, BF16), (LANE, F32)]
    return pl.pallas_call(
        _router_kernel,
        grid=(Bx, Tx // tm),
        in_specs=[pl.BlockSpec((1, tm, D), row),
                  pl.BlockSpec((1, D), lambda b, i: (0, 0)),
                  pl.BlockSpec((1, 6, D), lambda b, i: (mod_row(b), 0, 0)),
                  pl.BlockSpec((D, LANE), lambda b, i: (0, 0))],
        out_specs=[pl.BlockSpec((1, tm, w), row) for w, _ in outs],
        out_shape=[jax.ShapeDtypeStruct((Bx, Tx, w), dt) for w, dt in outs],
        compiler_params=_params("parallel", "parallel"),
        name="router",
    )(x, norm_g.reshape(1, D), mods, rw)


def _select_kernel(aff_ref, pos_ref, *, cap):
    aff = aff_ref[0]
    T = aff.shape[0]
    real = lax.broadcasted_iota(jnp.int32, aff.shape, 1) < N_EXPERTS
    bits = pltpu.bitcast(aff, jnp.int32)

    def search(i, thr):
        cand = thr | (jnp.int32(1) << (30 - i))
        cnt = jnp.sum(jnp.where(bits >= cand, 1.0, 0.0), axis=0, keepdims=True)
        return jnp.where(cnt >= cap, cand, thr)

    thr = lax.fori_loop(0, 31, search, jnp.zeros((1, LANE), jnp.int32))
    gt = jnp.where(bits > thr, 1.0, 0.0)
    eq = jnp.where(bits == thr, 1.0, 0.0)
    need = cap - jnp.sum(gt, axis=0, keepdims=True)
    blk = min(T, 256)
    sel = gt + eq * jnp.where(_prefix_count(eq, blk) < need, 1.0, 0.0)
    pos = _prefix_count(sel, blk)
    pos_ref[0] = jnp.where(jnp.logical_and(sel > 0.0, real), pos, -1.0)


def _select(aff, cap):
    Bx, Tx, _ = aff.shape
    blk = pl.BlockSpec((1, Tx, LANE), lambda b: (b, 0, 0))
    return pl.pallas_call(
        functools.partial(_select_kernel, cap=cap),
        grid=(Bx,),
        in_specs=[blk],
        out_specs=blk,
        out_shape=jax.ShapeDtypeStruct((Bx, Tx, LANE), F32),
        compiler_params=_params("parallel"),
        name="moe_select",
    )(aff)


def _onehot_t(pos_col, cap, val=None):
    slot = lax.broadcasted_iota(jnp.int32, (pos_col.shape[0], cap), 1).astype(F32)
    hit = pos_col == slot
    if val is None:
        return jnp.where(hit, 1.0, 0.0).astype(BF16)
    return jnp.where(hit, val, 0.0).astype(BF16)


def _dispatch_kernel(h_ref, pos_ref, xs_ref, *, cap):
    e = pl.program_id(1)
    lane = lax.broadcasted_iota(jnp.int32, pos_ref.shape[1:], 1)
    pos_col = jnp.sum(jnp.where(lane == e, pos_ref[0], 0.0), axis=-1, keepdims=True)
    xs_ref[0] = _dot_tn(_onehot_t(pos_col, cap), h_ref[0]).astype(BF16)


def _dispatch(h, pos, cap):
    Bx, Tx, D = h.shape
    return pl.pallas_call(
        functools.partial(_dispatch_kernel, cap=cap),
        grid=(Bx, N_EXPERTS),
        in_specs=[pl.BlockSpec((1, Tx, D), lambda b, e: (b, 0, 0)),
                  pl.BlockSpec((1, Tx, LANE), lambda b, e: (b, 0, 0))],
        out_specs=pl.BlockSpec((1, cap, D), lambda b, e: (e, b, 0)),
        out_shape=jax.ShapeDtypeStruct((N_EXPERTS, Bx * cap, D), BF16),
        compiler_params=_params("parallel", "parallel"),
        name="moe_dispatch",
    )(h, pos)


def _expert_kernel(x_ref, wg_ref, wu_ref, wd_ref, y_ref):
    x = x_ref[0]
    ff = wg_ref.shape[2]
    fc = 512
    acc = None
    for f in range(ff // fc):
        a = _dot(x, wg_ref[0, :, f * fc:(f + 1) * fc])
        u = _dot(x, wu_ref[0, :, f * fc:(f + 1) * fc])
        hm = (a * jax.nn.sigmoid(a) * u).astype(BF16)
        c = _dot(hm, wd_ref[0, f * fc:(f + 1) * fc, :])
        acc = c if acc is None else acc + c
    y_ref[0] = acc.astype(BF16)


def _experts(xs, wg, wu, wd, tm):
    E, M, D = xs.shape
    FF = wg.shape[2]
    return pl.pallas_call(
        _expert_kernel,
        grid=(E, M // tm),
        in_specs=[pl.BlockSpec((1, tm, D), lambda e, i: (e, i, 0)),
                  pl.BlockSpec((1, D, FF), lambda e, i: (e, 0, 0)),
                  pl.BlockSpec((1, D, FF), lambda e, i: (e, 0, 0)),
                  pl.BlockSpec((1, FF, D), lambda e, i: (e, 0, 0))],
        out_specs=pl.BlockSpec((1, tm, D), lambda e, i: (e, i, 0)),
        out_shape=jax.ShapeDtypeStruct((E, M, D), BF16),
        compiler_params=_params("parallel", "parallel"),
        name="moe_experts",
    )(xs, wg, wu, wd)


def _combine_kernel(y_ref, pos_ref, aff_ref, x_ref, mod_ref, o_ref, acc_ref, *, cap):
    e = pl.program_id(2)
    lane = lax.broadcasted_iota(jnp.int32, pos_ref.shape[1:], 1)
    pick = lane == e
    pos_col = jnp.sum(jnp.where(pick, pos_ref[0], 0.0), axis=-1, keepdims=True)
    gate_col = jnp.sum(jnp.where(pick, aff_ref[0], 0.0), axis=-1, keepdims=True)
    contrib = _dot(_onehot_t(pos_col, cap, gate_col), y_ref[0])

    @pl.when(e == 0)
    def _():
        acc_ref[...] = contrib

    @pl.when(e > 0)
    def _():
        acc_ref[...] += contrib

    @pl.when(e == N_EXPERTS - 1)
    def _():
        o_ref[0] = x_ref[0] + mod_ref[0, 5:6] * acc_ref[...]


def _combine(y, pos, aff, x, mods, mod_row, cap, tm):
    Bx, Tx, D = x.shape
    row = lambda b, i, e: (b, i, 0)
    return pl.pallas_call(
        functools.partial(_combine_kernel, cap=cap),
        grid=(Bx, Tx // tm, N_EXPERTS),
        in_specs=[pl.BlockSpec((1, cap, D), lambda b, i, e: (e, b, 0)),
                  pl.BlockSpec((1, tm, LANE), row),
                  pl.BlockSpec((1, tm, LANE), row),
                  pl.BlockSpec((1, tm, D), row),
                  pl.BlockSpec((1, 6, D), lambda b, i, e: (mod_row(b), 0, 0))],
        out_specs=pl.BlockSpec((1, tm, D), row),
        out_shape=jax.ShapeDtypeStruct((Bx, Tx, D), F32),
        scratch_shapes=[pltpu.VMEM((tm, D), F32)],
        compiler_params=_params("parallel", "parallel", "arbitrary"),
        name="moe_combine",
    )(y, pos, aff, x, mods)


def _moe(x, norm_g, mods, mod_row, rw, wg, wu, wd, tm):
    Bx, Tx, D = x.shape
    cap = CAPACITY_FACTOR * Tx // N_EXPERTS
    h, aff = _router(x, norm_g, mods, mod_row, rw, tm)
    pos = _select(aff, cap)
    xs = _dispatch(h, pos, cap)
    y = _experts(xs, wg, wu, wd, min(Bx * cap, 512))
    return _combine(y, pos, aff, x, mods, mod_row, cap, tm)


def _final_norm_kernel(x_ref, g_ref, o_ref):
    o_ref[0] = _rms(x_ref[0]) * g_ref[...]


def _final_norm(x, g, tm):
    Bx, Tx, D = x.shape
    return pl.pallas_call(
        _final_norm_kernel,
        grid=(Bx, Tx // tm),
        in_specs=[pl.BlockSpec((1, tm, D), lambda b, i: (b, i, 0)),
                  pl.BlockSpec((1, D), lambda b, i: (0, 0))],
        out_specs=pl.BlockSpec((1, tm, D), lambda b, i: (b, i, 0)),
        out_shape=jax.ShapeDtypeStruct((Bx, Tx, D), F32),
        compiler_params=_params("parallel", "parallel"),
        name="final_norm",
    )(x, g.reshape(1, D))


def _rope_tables(T, rot_dim, lane_lo, period, scale):
    rows = T // GRID_W
    row = jnp.repeat(jnp.arange(rows, dtype=F32), GRID_W)
    colp = jnp.tile(jnp.arange(GRID_W, dtype=F32), rows)
    n_freq = rot_dim // 4
    inv_freq = ROPE_BASE ** (-jnp.arange(n_freq, dtype=F32) / n_freq)
    ang = jnp.concatenate([row[:, None] * inv_freq, colp[:, None] * inv_freq], axis=-1)
    cos_h, sin_h = jnp.cos(ang), jnp.sin(ang)
    half = rot_dim // 2
    cos_g = jnp.ones((T, period), F32)
    sa_g = jnp.zeros((T, period), F32)
    sb_g = jnp.zeros((T, period), F32)
    cos_g = cos_g.at[:, lane_lo:lane_lo + rot_dim].set(jnp.concatenate([cos_h, cos_h], axis=-1))
    sa_g = sa_g.at[:, lane_lo:lane_lo + half].set(-sin_h)
    sb_g = sb_g.at[:, lane_lo + half:lane_lo + rot_dim].set(sin_h)
    rep = LANE // period
    return tuple(jnp.tile(t, (1, rep)) * scale for t in (cos_g, sa_g, sb_g))


def _identity_tables(T, scale):
    return (jnp.full((T, LANE), scale, F32), jnp.zeros((T, LANE), F32), jnp.zeros((T, LANE), F32))


def _prep_w_in(w):
    D = w.shape[0]
    o_diff = HGRN_W
    o_cq = o_diff + 1536
    o_ckv = o_cq + C_Q_LORA
    o_kr = o_ckv + C_KV_LORA
    o_gate = o_kr + C_ROPE
    pad = jnp.zeros((D, MLA_KV_IN - C_KV_LORA - C_ROPE), w.dtype)
    return jnp.concatenate([w[:, :o_cq], w[:, o_ckv:o_kr], w[:, o_kr:o_gate], pad, w[:, o_cq:o_ckv],
                            w[:, o_gate:]], axis=1).astype(BF16)


def _prep_mla_w(w_uq, w_ukv):
    wq = w_uq.reshape(C_Q_LORA, C_HEADS, C_NOPE + C_ROPE)
    wq = jnp.pad(wq, ((0, 0), (0, 0), (0, LANE - C_NOPE - C_ROPE))).reshape(C_Q_LORA, C_HEADS * LANE)
    wkv = w_ukv.reshape(C_KV_LORA, C_HEADS, C_NOPE + C_V)
    wk = jnp.pad(wkv[:, :, :C_NOPE], ((0, 0), (0, 0), (0, LANE - C_NOPE)))
    place = jnp.zeros((C_ROPE, C_HEADS, LANE), F32).at[:, :, C_NOPE:C_NOPE + C_ROPE].set(
        jnp.broadcast_to(jnp.eye(C_ROPE, dtype=F32)[:, None, :], (C_ROPE, C_HEADS, C_ROPE)))
    wk = jnp.concatenate([wk, place, jnp.zeros((MLA_KV_IN - C_KV_LORA - C_ROPE, C_HEADS, LANE), F32)], axis=0)
    wv = jnp.pad(wkv[:, :, C_NOPE:].reshape(C_KV_LORA, C_HEADS * C_V), ((0, MLA_KV_IN - C_KV_LORA), (0, 0)))
    wkv_comb = jnp.concatenate([wk.reshape(MLA_KV_IN, C_HEADS * LANE), wv], axis=1)
    return wq.astype(BF16), wkv_comb.astype(BF16)


def kernel(x, c, ctx, c_ctx, ada_w, ada_b, norm_mix_g, norm_ffn_g, w_in, hgrn_lb, hgrn_norm_g, diff_lambda,
           diff_norm_g, mla_q_norm_g, mla_w_uq, mla_kv_norm_g, mla_w_ukv, w_branch, w_out, router_w,
           exp_w_gate, exp_w_up, exp_w_down, final_norm_g):
    B, T, D = x.shape
    Tc = ctx.shape[1]
    depth = ada_w.shape[0]
    tm_l = min(T, 512)
    tm_c = min(Tc, 256)
    tq = min(Tc, 256)

    rows = ((B + 1 + SUBLANE - 1) // SUBLANE) * SUBLANE
    vecs = jnp.concatenate([c, c_ctx[None], jnp.zeros((rows - B - 1, D), F32)], axis=0)
    mods_all = _modulation(vecs, ada_w, ada_b).reshape(depth, rows, 6, D)
    lat_row = lambda b: b
    ctx_row = lambda b: B

    p = jax.nn.softmax(hgrn_lb.astype(F32), axis=0)
    lower_bounds = jnp.clip(jnp.cumsum(p, axis=0) - p[0], 0.0, LB_MAX)

    d_scale = B_HD ** -0.5
    c_scale = (C_NOPE + C_ROPE) ** -0.5
    dq_l = _rope_tables(T, B_HD, 0, B_HD, d_scale)
    dk_l = _rope_tables(T, B_HD, 0, B_HD, 1.0)
    dq_c, dk_c = _identity_tables(Tc, d_scale), _identity_tables(Tc, 1.0)
    mq_l = _rope_tables(T, C_ROPE, C_NOPE, LANE, c_scale)
    mk_l = _rope_tables(T, C_ROPE, C_NOPE, LANE, 1.0)
    mq_c, mk_c = _identity_tables(Tc, c_scale), _identity_tables(Tc, 1.0)

    xl, xc = x, ctx
    for layer in range(depth):
        with_ctx = layer < depth - 1
        mods = mods_all[layer]
        w_perm = _prep_w_in(w_in[layer])
        wq, wkv = _prep_mla_w(mla_w_uq[layer], mla_w_ukv[layer])
        wb = w_branch[layer].astype(BF16)
        wo = w_out[layer].astype(BF16)
        rw = jnp.pad(router_w[layer], ((0, 0), (0, LANE - N_EXPERTS))).astype(BF16)
        wg = exp_w_gate[layer].astype(BF16)
        wu = exp_w_up[layer].astype(BF16)
        wd = exp_w_down[layer].astype(BF16)
        lam_init = 0.8 - 0.6 * math.exp(-0.3 * layer)
        gq = mla_q_norm_g[layer].reshape(1, -1)
        gkv = jnp.pad(mla_kv_norm_g[layer], (0, MLA_KV_IN - C_KV_LORA)).reshape(1, -1)

        hg_l, dqk_l, dv_l, ml_l, gt_l = _layer_in(xl, norm_mix_g[layer], mods, lat_row, w_perm, dq_l, dk_l, tm_l)
        hg_c, dqk_c, dv_c, ml_c, gt_c = _layer_in(xc, norm_mix_g[layer], mods, ctx_row, w_perm, dq_c, dk_c, tm_c)

        a_l, a_c = _hgrn(hg_l, hg_c, lower_bounds[layer], hgrn_norm_g[layer], with_ctx)

        b_l = _diff_attn(dqk_l, [(dqk_c, dv_c), (dqk_l, dv_l)], diff_lambda[layer], diff_norm_g[layer], lam_init, tq)
        mq_lat, mk_lat, mv_lat = _mla_proj(ml_l, gq, gkv, wq, wkv, mq_l, mk_l, tm_l)
        mq_ctx, mk_ctx, mv_ctx = _mla_proj(ml_c, gq, gkv, wq, wkv, mq_c, mk_c, tm_c)
        m_l = _mla_attn(mq_lat, [(mk_ctx, mv_ctx), (mk_lat, mv_lat)], tq)

        xl = _merge(a_l, b_l, m_l, gt_l, wb, wo, xl, mods, lat_row, tm_l)
        xl = _moe(xl, norm_ffn_g[layer], mods, lat_row, rw, wg, wu, wd, tm_l)
        if with_ctx:
            b_c = _diff_attn(dqk_c, [(dqk_c, dv_c)], diff_lambda[layer], diff_norm_g[layer], lam_init, tq)
            m_c = _mla_attn(mq_ctx, [(mk_ctx, mv_ctx)], tq)
            xc = _merge(a_c, b_c, m_c, gt_c, wb, wo, xc, mods, ctx_row, tm_c)
            xc = _moe(xc, norm_ffn_g[layer], mods, ctx_row, rw, wg, wu, wd, tm_c)
    return _final_norm(xl, final_norm_g, tm_l)
```

```python
import functools
import math

import jax
import jax.numpy as jnp
from jax import lax
from jax.experimental import pallas as pl
from jax.experimental.pallas import tpu as pltpu

F32 = jnp.float32
BF16 = jnp.bfloat16

EPS = 1e-6
ROPE_BASE = 10000.0
GRID_W = 64
F_MIN = 1e-30
LB_MAX = 1.0 - 1e-6

A_HEADS, A_DK = 4, 128
B_HEADS, B_HD = 4, 64
C_HEADS, C_NOPE, C_ROPE, C_V = 8, 64, 32, 64
C_Q_LORA, C_KV_LORA = 384, 256
N_BRANCH, BRANCH_W = 3, 512
N_EXPERTS = 16
CAPACITY_FACTOR = 2

LANE = 128
SUBLANE = 8
VMEM_LIMIT = 56 * 1024 * 1024

HGRN_W = 5 * 512
MLA_W = 768
MLA_KV_IN = 384
GATE_W = N_BRANCH * 1024
HGRN_CHUNK = 128


def _params(*sem):
    return pltpu.CompilerParams(dimension_semantics=sem, vmem_limit_bytes=VMEM_LIMIT)


def _resident(shape, index_map):
    return pl.BlockSpec(shape, index_map, pipeline_mode=pl.Buffered(1))


def _rms(x, eps=EPS):
    return x * lax.rsqrt(jnp.mean(x * x, axis=-1, keepdims=True) + eps)


def _dot(a, b):
    return jnp.dot(a, b, preferred_element_type=F32)


def _dot_nt(a, b):
    return lax.dot_general(a, b, (((1,), (1,)), ((), ())), preferred_element_type=F32)


def _dot_tn(a, b):
    return lax.dot_general(a, b, (((0,), (0,)), ((), ())), preferred_element_type=F32)


def _mods_kernel(v_ref, w_ref, b_ref, o_ref):
    v = v_ref[...]
    s = (v * jax.nn.sigmoid(v)).astype(BF16)
    o_ref[0] = _dot(s, w_ref[0].astype(BF16)) + b_ref[0]


def _modulation(vecs, ada_w, ada_b):
    L, D, N = ada_w.shape
    R = vecs.shape[0]
    tn = 1536
    return pl.pallas_call(
        _mods_kernel,
        grid=(L, N // tn),
        in_specs=[pl.BlockSpec((R, D), lambda l, j: (0, 0)),
                  pl.BlockSpec((1, D, tn), lambda l, j: (l, 0, j)),
                  pl.BlockSpec((1, 1, tn), lambda l, j: (l, 0, j))],
        out_specs=pl.BlockSpec((1, R, tn), lambda l, j: (l, 0, j)),
        out_shape=jax.ShapeDtypeStruct((L, R, N), F32),
        compiler_params=_params("parallel", "parallel"),
        name="adaln_mods",
    )(vecs, ada_w, ada_b.reshape(L, 1, N))


def _rope_apply(x, cos, sin_a, sin_b, half):
    return (x * cos + pltpu.roll(x, LANE - half, axis=1) * sin_a + pltpu.roll(x, half, axis=1) * sin_b)


def _layer_in_kernel(x_ref, g_ref, mod_ref, w_ref, cq_ref, saq_ref, sbq_ref, ck_ref, sak_ref, sbk_ref,
                     hg_ref, dqk_ref, dv_ref, mla_ref, gate_ref):
    x = x_ref[0]
    mod = mod_ref[0]
    h = (_rms(x) * g_ref[...]) * (1.0 + mod[1:2]) + mod[0:1]
    h = h.astype(BF16)
    c0 = 0
    for j in range(HGRN_W // 512):
        hg_ref[0, :, j * 512:(j + 1) * 512] = _dot(h, w_ref[:, c0:c0 + 512])
        c0 += 512
    for j in range(2):
        acc = _dot(h, w_ref[:, c0:c0 + 512])
        cos, sa, sb = (cq_ref, saq_ref, sbq_ref) if j == 0 else (ck_ref, sak_ref, sbk_ref)
        for g in range(4):
            xg = acc[:, g * LANE:(g + 1) * LANE]
            dqk_ref[0, :, j * 512 + g * LANE:j * 512 + (g + 1) * LANE] = _rope_apply(
                xg, cos[...], sa[...], sb[...], B_HD // 2).astype(BF16)
        c0 += 512
    dv_ref[0] = _dot(h, w_ref[:, c0:c0 + 512]).astype(BF16)
    c0 += 512
    mla_ref[0] = _dot(h, w_ref[:, c0:c0 + MLA_W])
    c0 += MLA_W
    for j in range(GATE_W // 512):
        gate_ref[0, :, j * 512:(j + 1) * 512] = jax.nn.sigmoid(_dot(h, w_ref[:, c0:c0 + 512])).astype(BF16)
        c0 += 512


def _layer_in(x, norm_g, mods, mod_row, w_perm, tabs_q, tabs_k, tm):
    Bx, Tx, D = x.shape
    NW = w_perm.shape[1]
    row = lambda b, i: (b, i, 0)
    tab = pl.BlockSpec((tm, LANE), lambda b, i: (i, 0))
    outs = [(HGRN_W, F32), (1024, BF16), (512, BF16), (MLA_W, F32), (GATE_W, BF16)]
    return pl.pallas_call(
        _layer_in_kernel,
        grid=(Bx, Tx // tm),
        in_specs=[pl.BlockSpec((1, tm, D), row),
                  pl.BlockSpec((1, D), lambda b, i: (0, 0)),
                  pl.BlockSpec((1, 6, D), lambda b, i: (mod_row(b), 0, 0)),
                  _resident((D, NW), lambda b, i: (0, 0)),
                  tab, tab, tab, tab, tab, tab],
        out_specs=[pl.BlockSpec((1, tm, w), row) for w, _ in outs],
        out_shape=[jax.ShapeDtypeStruct((Bx, Tx, w), dt) for w, dt in outs],
        compiler_params=_params("parallel", "parallel"),
        name="layer_in",
    )(x, norm_g.reshape(1, D), mods, w_perm, *tabs_q, *tabs_k)


def _cumsum_rows(x, reverse):
    n = x.shape[0]
    s = 1
    while s < n:
        if s < SUBLANE:
            row = lax.broadcasted_iota(jnp.int32, x.shape, 0)
            if reverse:
                sh = jnp.where(row < n - s, pltpu.roll(x, n - s, axis=0), 0.0)
            else:
                sh = jnp.where(row >= s, pltpu.roll(x, s, axis=0), 0.0)
        else:
            z = jnp.zeros((s, x.shape[1]), x.dtype)
            sh = jnp.concatenate([x[s:], z], axis=0) if reverse else jnp.concatenate([z, x[:n - s]], axis=0)
        x = x + sh
        s *= 2
    return x


def _hgrn_chunk(q, k_in, v, lf, st, reverse):
    C = q.shape[0]
    b = _cumsum_rows(lf, reverse)
    b_last = b[0:1] if reverse else b[C - 1:C]
    qd = (q * jnp.exp(b)).astype(BF16)
    o = _dot_nt(qd, st.astype(BF16))
    kd = (k_in * jnp.exp(b_last - b)).astype(BF16)
    vb = v.astype(BF16)
    st_new = st * jnp.exp(b_last) + _dot_tn(vb, kd)

    row = lax.broadcasted_iota(jnp.int32, (C, C), 0)
    col = lax.broadcasted_iota(jnp.int32, (C, C), 1)
    scores = jnp.zeros((C, C), F32)
    m = C // 2
    while m >= SUBLANE:
        nb = C // (2 * m)
        b3 = b.reshape(nb, 2 * m, LANE)
        anchor = b3[:, m:m + 1, :] if reverse else b3[:, m - 1:m, :]
        e = jnp.exp(-jnp.abs(b3 - anchor)).reshape(C, LANE)
        r1 = lax.broadcasted_iota(jnp.int32, (C, 1), 0)
        second = (r1 % (2 * m)) >= m
        late = jnp.logical_not(second) if reverse else second
        qh = jnp.where(late, q * e, 0.0).astype(BF16)
        kh = jnp.where(late, 0.0, k_in * e).astype(BF16)
        same = (row // (2 * m)) == (col // (2 * m))
        scores = scores + jnp.where(same, _dot_nt(qh, kh), 0.0)
        m //= 2
    o = o + _dot(scores.astype(BF16), vb)

    nb = C // SUBLANE
    b3 = b.reshape(nb, SUBLANE, LANE)
    q3 = q.reshape(nb, SUBLANE, LANE)
    k3 = k_in.reshape(nb, SUBLANE, LANE)
    v3 = v.reshape(nb, SUBLANE, LANE)
    t3 = lax.broadcasted_iota(jnp.int32, (nb, SUBLANE, 1), 1)
    od = jnp.zeros((nb, SUBLANE, LANE), F32)
    for j in range(SUBLANE):
        e = jnp.exp(jnp.minimum(b3 - b3[:, j:j + 1, :], 0.0))
        w = q3 * e * k3[:, j:j + 1, :]
        valid = (t3 <= j) if reverse else (t3 >= j)
        sc = jnp.sum(jnp.where(valid, w, 0.0), axis=-1, keepdims=True)
        od = od + sc * v3[:, j:j + 1, :]
    return o + od.reshape(C, LANE), st_new


def _hgrn_kernel(*refs, with_ctx_out):
    (ql, il, gl, ffl, fbl, qc, ic, gc, ffc, fbc, lb_ref, ng_ref) = refs[:12]
    if with_ctx_out:
        ol_ref, oc_ref, accl, accc = refs[12:]
    else:
        ol_ref, accl, accc = refs[12:]
        oc_ref = None
    C = HGRN_CHUNK
    nl = ql.shape[1] // C
    nc = qc.shape[1] // C

    for d in range(2):
        reverse = d == 1
        lbd = lb_ref[d:d + 1, :]
        one_m = 1.0 - lbd

        def make_step(qr, ir, fr, acc, n, first):
            def step(s, st):
                c = (n - 1 - s) if reverse else s
                r0 = pl.multiple_of(c * C, C)
                z = fr[0, pl.ds(r0, C), :]
                sg = jax.nn.sigmoid(z)
                kk = one_m * (1.0 - sg)
                lf = jnp.log(jnp.maximum(lbd + one_m * sg, F_MIN))
                o, st = _hgrn_chunk(qr[0, pl.ds(r0, C), :], kk, ir[0, pl.ds(r0, C), :], lf, st, reverse)
                if first:
                    acc[pl.ds(r0, C), :] = o
                else:
                    acc[pl.ds(r0, C), :] += o
                return st
            return step

        st = jnp.zeros((LANE, LANE), F32)
        st = lax.fori_loop(0, nc, make_step(qc, ic, fbc if reverse else ffc, accc, nc, d == 0), st)
        lax.fori_loop(0, nl, make_step(ql, il, fbl if reverse else ffl, accl, nl, d == 0), st)

    def readout(acc, g_ref, o_ref):
        o = _rms(acc[...]) * ng_ref[...]
        g = g_ref[0]
        o_ref[0] = (o * (g * jax.nn.sigmoid(g))).astype(BF16)

    readout(accl, gl, ol_ref)
    if with_ctx_out:
        readout(accc, gc, oc_ref)


def _hgrn(hg_l, hg_c, lb, norm_g, with_ctx_out):
    B, T, _ = hg_l.shape
    Tc = hg_c.shape[1]
    nh = A_HEADS

    def col(n, t):
        return [pl.BlockSpec((1, t, LANE), (lambda b, h, j=j: (b, 0, j * nh + h))) for j in range(n)]

    in_specs = col(5, T) + col(5, Tc) + [pl.BlockSpec((2, LANE), lambda b, h: (0, h)),
                                         pl.BlockSpec((1, LANE), lambda b, h: (0, h))]
    out_specs = [pl.BlockSpec((1, T, LANE), lambda b, h: (b, 0, h))]
    out_shape = [jax.ShapeDtypeStruct((B, T, nh * LANE), BF16)]
    if with_ctx_out:
        out_specs.append(pl.BlockSpec((1, Tc, LANE), lambda b, h: (b, 0, h)))
        out_shape.append(jax.ShapeDtypeStruct((B, Tc, nh * LANE), BF16))
    res = pl.pallas_call(
        functools.partial(_hgrn_kernel, with_ctx_out=with_ctx_out),
        grid=(B, nh),
        in_specs=in_specs,
        out_specs=out_specs,
        out_shape=out_shape,
        scratch_shapes=[pltpu.VMEM((T, LANE), F32), pltpu.VMEM((Tc, LANE), F32)],
        compiler_params=_params("parallel", "parallel"),
        name="hgrn2",
    )(*([hg_l] * 5), *([hg_c] * 5), lb, norm_g.reshape(1, -1))
    return (res[0], res[1]) if with_ctx_out else (res[0], None)


ATTN_KEY_CHUNK = 512


def _softmax_pv(q, kv, kcols, s_scr):
    mx = None
    spans = []
    off = 0
    for k_ref, v_ref in kv:
        tk = k_ref.shape[1]
        kc = min(tk, ATTN_KEY_CHUNK)
        for c in range(tk // kc):
            s = _dot_nt(q, k_ref[0, c * kc:(c + 1) * kc, kcols])
            s_scr[:, off:off + kc] = s
            for j in range(kc // LANE):
                blk = s[:, j * LANE:(j + 1) * LANE]
                mx = blk if mx is None else jnp.maximum(mx, blk)
            spans.append((off, kc, v_ref, c))
            off += kc
    m = mx.max(axis=-1, keepdims=True)
    lv = None
    acc = None
    for off, kc, v_ref, c in spans:
        p = jnp.exp2(s_scr[:, off:off + kc] - m)
        for j in range(kc // LANE):
            blk = p[:, j * LANE:(j + 1) * LANE]
            lv = blk if lv is None else lv + blk
        pv = _dot(p.astype(BF16), v_ref[0, c * kc:(c + 1) * kc, :])
        acc = pv if acc is None else acc + pv
    return acc, lv.sum(axis=-1, keepdims=True)


def _diff_attn_kernel(*refs, n_kv, lam_init):
    q_ref, lam_ref, ng_ref = refs[0], refs[1], refs[2]
    kv = [(refs[3 + 2 * i], refs[4 + 2 * i]) for i in range(n_kv)]
    o_ref, s1_scr, s2_scr = refs[3 + 2 * n_kv:]
    lp = lam_ref[...]
    lam = (jnp.exp(jnp.sum(lp[0:1] * lp[1:2], axis=-1, keepdims=True))
           - jnp.exp(jnp.sum(lp[2:3] * lp[3:4], axis=-1, keepdims=True)) + lam_init)
    q = q_ref[0]
    lane = lax.broadcasted_iota(jnp.int32, q.shape, 1)
    zero = jnp.zeros_like(q)
    q1 = jnp.where(lane < B_HD, q, zero)
    q2 = jnp.where(lane < B_HD, zero, q)
    a1, l1 = _softmax_pv(q1, kv, slice(None), s1_scr)
    a2, l2 = _softmax_pv(q2, kv, slice(None), s2_scr)
    o = a1 * (1.0 / l1) - a2 * (lam / l2)
    o_ref[0] = (_rms(o) * ng_ref[...] * (1.0 - lam_init)).astype(BF16)


def _diff_attn(q_src, kvs, lam_params, norm_g, lam_init, tq):
    B, Tq, _ = q_src.shape
    nh = B_HEADS
    in_specs = [pl.BlockSpec((1, tq, LANE), lambda b, h, i: (b, i, h)),
                pl.BlockSpec((4, B_HD), lambda b, h, i: (0, 0)),
                pl.BlockSpec((1, LANE), lambda b, h, i: (0, h))]
    args = [q_src, lam_params, norm_g.reshape(1, -1)]
    n_keys = 0
    for k_arr, v_arr in kvs:
        Tk = k_arr.shape[1]
        n_keys += Tk
        in_specs.append(pl.BlockSpec((1, Tk, LANE), lambda b, h, i: (b, 0, nh + h)))
        in_specs.append(pl.BlockSpec((1, Tk, LANE), lambda b, h, i: (b, 0, h)))
        args += [k_arr, v_arr]
    return pl.pallas_call(
        functools.partial(_diff_attn_kernel, n_kv=len(kvs), lam_init=lam_init),
        grid=(B, nh, Tq // tq),
        in_specs=in_specs,
        out_specs=pl.BlockSpec((1, tq, LANE), lambda b, h, i: (b, i, h)),
        out_shape=jax.ShapeDtypeStruct((B, Tq, nh * LANE), BF16),
        scratch_shapes=[pltpu.VMEM((tq, n_keys), F32), pltpu.VMEM((tq, n_keys), F32)],
        compiler_params=_params("parallel", "parallel", "parallel"),
        name="diff_attn",
    )(*args)


def _mla_attn_kernel(*refs, n_kv):
    q_ref = refs[0]
    kv = [(refs[1 + 2 * i], refs[2 + 2 * i]) for i in range(n_kv)]
    o_ref, s1_scr, s2_scr = refs[1 + 2 * n_kv:]
    outs = []
    for hh, s_scr in enumerate((s1_scr, s2_scr)):
        cols = slice(hh * LANE, (hh + 1) * LANE)
        acc, l = _softmax_pv(q_ref[0, :, cols], kv, cols, s_scr)
        outs.append(acc * (1.0 / l))
    lane = lax.broadcasted_iota(jnp.int32, outs[0].shape, 1)
    o_ref[0] = jnp.where(lane < C_V, outs[0], outs[1]).astype(BF16)


def _mla_attn(q, kvs, tq):
    B, Tq, _ = q.shape
    npair = C_HEADS // 2
    in_specs = [pl.BlockSpec((1, tq, 2 * LANE), lambda b, h, i: (b, i, h))]
    args = [q]
    n_keys = 0
    for k_arr, v_arr in kvs:
        Tk = k_arr.shape[1]
        n_keys += Tk
        in_specs.append(pl.BlockSpec((1, Tk, 2 * LANE), lambda b, h, i: (b, 0, h)))
        in_specs.append(pl.BlockSpec((1, Tk, LANE), lambda b, h, i: (b, 0, h)))
        args += [k_arr, v_arr]
    return pl.pallas_call(
        functools.partial(_mla_attn_kernel, n_kv=len(kvs)),
        grid=(B, npair, Tq // tq),
        in_specs=in_specs,
        out_specs=pl.BlockSpec((1, tq, LANE), lambda b, h, i: (b, i, h)),
        out_shape=jax.ShapeDtypeStruct((B, Tq, C_HEADS * C_V), BF16),
        scratch_shapes=[pltpu.VMEM((tq, n_keys), F32), pltpu.VMEM((tq, n_keys), F32)],
        compiler_params=_params("parallel", "parallel", "parallel"),
        name="mla_attn",
    )(*args)


def _mla_proj_kernel(x_ref, gq_ref, gkv_ref, wq_ref, wkv_ref, cq_ref, saq_ref, sbq_ref, ck_ref, sak_ref,
                     sbk_ref, q_ref, k_ref, v_ref):
    x = x_ref[0]
    xkv = x[:, :MLA_KV_IN]
    lane = lax.broadcasted_iota(jnp.int32, xkv.shape, 1)
    lat = jnp.where(lane < C_KV_LORA, xkv, 0.0)
    ms = jnp.sum(lat * lat, axis=-1, keepdims=True) * (1.0 / C_KV_LORA)
    hk = jnp.where(lane < C_KV_LORA, lat * lax.rsqrt(ms + EPS) * gkv_ref[...], xkv).astype(BF16)
    hq = (_rms(x[:, MLA_KV_IN:]) * gq_ref[...]).astype(BF16)
    nk = C_HEADS * LANE
    aq = _dot(hq, wq_ref[...])
    akv = _dot(hk, wkv_ref[...])
    for g in range(C_HEADS):
        sl = slice(g * LANE, (g + 1) * LANE)
        q_ref[0, :, sl] = _rope_apply(aq[:, sl], cq_ref[...], saq_ref[...], sbq_ref[...], C_ROPE // 2).astype(BF16)
        k_ref[0, :, sl] = _rope_apply(akv[:, sl], ck_ref[...], sak_ref[...], sbk_ref[...], C_ROPE // 2).astype(BF16)
    v_ref[0] = akv[:, nk:].astype(BF16)


def _mla_proj(mla, gq, gkv, wq, wkv, tabs_q, tabs_k, tm):
    Bx, Tx, _ = mla.shape
    row = lambda b, i: (b, i, 0)
    const = lambda b, i: (0, 0)
    tab = pl.BlockSpec((tm, LANE), lambda b, i: (i, 0))
    nk = C_HEADS * LANE
    outs = [(nk, BF16), (nk, BF16), (C_HEADS * C_V, BF16)]
    return pl.pallas_call(
        _mla_proj_kernel,
        grid=(Bx, Tx // tm),
        in_specs=[pl.BlockSpec((1, tm, MLA_W), row),
                  pl.BlockSpec((1, C_Q_LORA), const),
                  pl.BlockSpec((1, MLA_KV_IN), const),
                  _resident(wq.shape, const),
                  _resident(wkv.shape, const),
                  tab, tab, tab, tab, tab, tab],
        out_specs=[pl.BlockSpec((1, tm, w), row) for w, _ in outs],
        out_shape=[jax.ShapeDtypeStruct((Bx, Tx, w), dt) for w, dt in outs],
        compiler_params=_params("parallel", "parallel"),
        name="mla_proj",
    )(mla, gq, gkv, wq, wkv, *tabs_q, *tabs_k)


def _merge_kernel(a_ref, b_ref, c_ref, gate_ref, wb_ref, wo_ref, x_ref, mod_ref, o_ref):
    D = x_ref.shape[2]
    z = None
    for n, br in enumerate((a_ref, b_ref, c_ref)):
        y = _dot(br[0], wb_ref[n])
        t = gate_ref[0, :, n * D:(n + 1) * D].astype(F32) * y
        z = t if z is None else z + t
    out = _dot(z.astype(BF16), wo_ref[...])
    o_ref[0] = x_ref[0] + mod_ref[0, 2:3] * out


def _merge(a, b, c, gates, wb, wo, x, mods, mod_row, tm):
    Bx, Tx, D = x.shape
    row = lambda bb, i: (bb, i, 0)
    br = pl.BlockSpec((1, tm, BRANCH_W), row)
    return pl.pallas_call(
        _merge_kernel,
        grid=(Bx, Tx // tm),
        in_specs=[br, br, br,
                  pl.BlockSpec((1, tm, GATE_W), row),
                  _resident(wb.shape, lambda bb, i: (0, 0, 0)),
                  _resident(wo.shape, lambda bb, i: (0, 0)),
                  pl.BlockSpec((1, tm, D), row),
                  pl.BlockSpec((1, 6, D), lambda bb, i: (mod_row(bb), 0, 0))],
        out_specs=pl.BlockSpec((1, tm, D), row),
        out_shape=jax.ShapeDtypeStruct((Bx, Tx, D), F32),
        compiler_params=_params("parallel", "parallel"),
        name="merge_out",
    )(a, b, c, gates, wb, wo, x, mods)


def _prefix_count(mask, blk):
    T = mask.shape[0]
    r = lax.broadcasted_iota(jnp.int32, (blk, blk), 0)
    c = lax.broadcasted_iota(jnp.int32, (blk, blk), 1)
    tri = jnp.where(c < r, 1.0, 0.0).astype(BF16)
    parts = []
    carry = jnp.zeros((1, mask.shape[1]), F32)
    for i in range(T // blk):
        mb = mask[i * blk:(i + 1) * blk]
        parts.append(_dot(tri, mb.astype(BF16)) + carry)
        carry = carry + jnp.sum(mb, axis=0, keepdims=True)
    return jnp.concatenate(parts, axis=0) if len(parts) > 1 else parts[0]


def _router_kernel(x_ref, g_ref, mod_ref, rw_ref, h_ref, aff_ref):
    mod = mod_ref[0]
    h = ((_rms(x_ref[0]) * g_ref[...]) * (1.0 + mod[4:5]) + mod[3:4]).astype(BF16)
    h_ref[0] = h
    logits = _dot(h, rw_ref[...])
    lane = lax.broadcasted_iota(jnp.int32, logits.shape, 1)
    lg = jnp.where(lane < N_EXPERTS, logits, -jnp.inf)
    e = jnp.exp(lg - lg.max(axis=-1, keepdims=True))
    aff_ref[0] = e / e.sum(axis=-1, keepdims=True)


def _router(x, norm_g, mods, mod_row, rw, tm):
    Bx, Tx, D = x.shape
    row = lambda b, i: (b, i, 0)
    outs = [(D, BF16), (LANE, F32)]
    return pl.pallas_call(
        _router_kernel,
        grid=(Bx, Tx // tm),
        in_specs=[pl.BlockSpec((1, tm, D), row),
                  pl.BlockSpec((1, D), lambda b, i: (0, 0)),
                  pl.BlockSpec((1, 6, D), lambda b, i: (mod_row(b), 0, 0)),
                  pl.BlockSpec((D, LANE), lambda b, i: (0, 0))],
        out_specs=[pl.BlockSpec((1, tm, w), row) for w, _ in outs],
        out_shape=[jax.ShapeDtypeStruct((Bx, Tx, w), dt) for w, dt in outs],
        compiler_params=_params("parallel", "parallel"),
        name="router",
    )(x, norm_g.reshape(1, D), mods, rw)


def _select_kernel(aff_ref, pos_ref, post_ref, *, cap):
    aff = aff_ref[0]
    T = aff.shape[0]
    real = lax.broadcasted_iota(jnp.int32, aff.shape, 1) < N_EXPERTS
    bits = pltpu.bitcast(aff, jnp.int32)

    def search(i, thr):
        cand = thr | (jnp.int32(1) << (30 - i))
        cnt = jnp.sum(jnp.where(bits >= cand, 1.0, 0.0), axis=0, keepdims=True)
        return jnp.where(cnt >= cap, cand, thr)

    thr = lax.fori_loop(0, 31, search, jnp.zeros((1, LANE), jnp.int32))
    gt = jnp.where(bits > thr, 1.0, 0.0)
    eq = jnp.where(bits == thr, 1.0, 0.0)
    need = cap - jnp.sum(gt, axis=0, keepdims=True)
    blk = min(T, 256)
    sel = gt + eq * jnp.where(_prefix_count(eq, blk) < need, 1.0, 0.0)
    pos = _prefix_count(sel, blk)
    pos = jnp.where(jnp.logical_and(sel > 0.0, real), pos, -1.0)
    pos_ref[0] = pos
    post_ref[0] = jnp.transpose(pos)[:N_EXPERTS]


def _select(aff, cap):
    Bx, Tx, _ = aff.shape
    blk = pl.BlockSpec((1, Tx, LANE), lambda b: (b, 0, 0))
    return pl.pallas_call(
        functools.partial(_select_kernel, cap=cap),
        grid=(Bx,),
        in_specs=[blk],
        out_specs=[blk, pl.BlockSpec((1, N_EXPERTS, Tx), lambda b: (b, 0, 0))],
        out_shape=[jax.ShapeDtypeStruct((Bx, Tx, LANE), F32),
                   jax.ShapeDtypeStruct((Bx, N_EXPERTS, Tx), F32)],
        compiler_params=_params("parallel"),
        name="moe_select",
    )(aff)


DISPATCH_GROUP = 4


def _dispatch_kernel(h_ref, post_ref, xs_ref, *, cap):
    g = pl.program_id(1)
    T = h_ref.shape[1]
    slot = lax.broadcasted_iota(jnp.int32, (cap, T), 0).astype(F32)
    for j in range(DISPATCH_GROUP):
        row = post_ref[0, pl.ds(g * DISPATCH_GROUP + j, 1), :]
        onehot = jnp.where(row == slot, 1.0, 0.0).astype(BF16)
        xs_ref[j] = _dot(onehot, h_ref[0]).astype(BF16)


def _dispatch(h, post, cap):
    Bx, Tx, D = h.shape
    return pl.pallas_call(
        functools.partial(_dispatch_kernel, cap=cap),
        grid=(Bx, N_EXPERTS // DISPATCH_GROUP),
        in_specs=[pl.BlockSpec((1, Tx, D), lambda b, g: (b, 0, 0)),
                  pl.BlockSpec((1, N_EXPERTS, Tx), lambda b, g: (b, 0, 0))],
        out_specs=pl.BlockSpec((DISPATCH_GROUP, cap, D), lambda b, g: (g, b, 0)),
        out_shape=jax.ShapeDtypeStruct((N_EXPERTS, Bx * cap, D), BF16),
        compiler_params=_params("parallel", "parallel"),
        name="moe_dispatch",
    )(h, post)


def _expert_kernel(x_ref, wg_ref, wu_ref, wd_ref, y_ref):
    x = x_ref[0]
    ff = wg_ref.shape[2]
    fc = 512
    acc = None
    for f in range(ff // fc):
        a = _dot(x, wg_ref[0, :, f * fc:(f + 1) * fc])
        u = _dot(x, wu_ref[0, :, f * fc:(f + 1) * fc])
        hm = (a * jax.nn.sigmoid(a) * u).astype(BF16)
        c = _dot(hm, wd_ref[0, f * fc:(f + 1) * fc, :])
        acc = c if acc is None else acc + c
    y_ref[0] = acc.astype(BF16)


def _experts(xs, wg, wu, wd, tm):
    E, M, D = xs.shape
    FF = wg.shape[2]
    return pl.pallas_call(
        _expert_kernel,
        grid=(E, M // tm),
        in_specs=[pl.BlockSpec((1, tm, D), lambda e, i: (e, i, 0)),
                  pl.BlockSpec((1, D, FF), lambda e, i: (e, 0, 0)),
                  pl.BlockSpec((1, D, FF), lambda e, i: (e, 0, 0)),
                  pl.BlockSpec((1, FF, D), lambda e, i: (e, 0, 0))],
        out_specs=pl.BlockSpec((1, tm, D), lambda e, i: (e, i, 0)),
        out_shape=jax.ShapeDtypeStruct((E, M, D), BF16),
        compiler_params=_params("parallel", "parallel"),
        name="moe_experts",
    )(xs, wg, wu, wd)


def _combine_kernel(y_ref, pos_ref, aff_ref, x_ref, mod_ref, o_ref, *, cap, group):
    width = group * cap
    pos = pos_ref[0].astype(BF16)
    gate = aff_ref[0].astype(BF16)
    src = lax.broadcasted_iota(jnp.int32, (LANE, width), 0)
    dst = lax.broadcasted_iota(jnp.int32, (LANE, width), 1)
    slot = (lax.broadcasted_iota(jnp.int32, (1, width), 1) % cap).astype(F32)
    out = None
    for g in range(N_EXPERTS // group):
        spread = jnp.where(dst // cap + g * group == src, 1.0, 0.0).astype(BF16)
        hit = _dot(pos, spread) == slot
        onehot = jnp.where(hit, _dot(gate, spread), 0.0).astype(BF16)
        c = _dot(onehot, y_ref[g * group:(g + 1) * group].reshape(width, y_ref.shape[2]))
        out = c if out is None else out + c
    o_ref[0] = x_ref[0] + mod_ref[0, 5:6] * out


def _combine(y, pos, aff, x, mods, mod_row, cap, tm):
    Bx, Tx, D = x.shape
    row = lambda b, i: (b, i, 0)
    group = max(4, LANE // cap)
    return pl.pallas_call(
        functools.partial(_combine_kernel, cap=cap, group=group),
        grid=(Bx, Tx // tm),
        in_specs=[pl.BlockSpec((N_EXPERTS, cap, D), lambda b, i: (0, b, 0)),
                  pl.BlockSpec((1, tm, LANE), row),
                  pl.BlockSpec((1, tm, LANE), row),
                  pl.BlockSpec((1, tm, D), row),
                  pl.BlockSpec((1, 6, D), lambda b, i: (mod_row(b), 0, 0))],
        out_specs=pl.BlockSpec((1, tm, D), row),
        out_shape=jax.ShapeDtypeStruct((Bx, Tx, D), F32),
        compiler_params=_params("parallel", "parallel"),
        name="moe_combine",
    )(y, pos, aff, x, mods)


def _moe(x, norm_g, mods, mod_row, rw, wg, wu, wd, tm):
    Bx, Tx, D = x.shape
    cap = CAPACITY_FACTOR * Tx // N_EXPERTS
    h, aff = _router(x, norm_g, mods, mod_row, rw, tm)
    pos, post = _select(aff, cap)
    xs = _dispatch(h, post, cap)
    y = _experts(xs, wg, wu, wd, min(Bx * cap, 512))
    return _combine(y, pos, aff, x, mods, mod_row, cap, tm)


def _final_norm_kernel(x_ref, g_ref, o_ref):
    o_ref[0] = _rms(x_ref[0]) * g_ref[...]


def _final_norm(x, g, tm):
    Bx, Tx, D = x.shape
    return pl.pallas_call(
        _final_norm_kernel,
        grid=(Bx, Tx // tm),
        in_specs=[pl.BlockSpec((1, tm, D), lambda b, i: (b, i, 0)),
                  pl.BlockSpec((1, D), lambda b, i: (0, 0))],
        out_specs=pl.BlockSpec((1, tm, D), lambda b, i: (b, i, 0)),
        out_shape=jax.ShapeDtypeStruct((Bx, Tx, D), F32),
        compiler_params=_params("parallel", "parallel"),
        name="final_norm",
    )(x, g.reshape(1, D))


def _rope_tables(T, rot_dim, lane_lo, period, scale):
    rows = T // GRID_W
    row = jnp.repeat(jnp.arange(rows, dtype=F32), GRID_W)
    colp = jnp.tile(jnp.arange(GRID_W, dtype=F32), rows)
    n_freq = rot_dim // 4
    inv_freq = ROPE_BASE ** (-jnp.arange(n_freq, dtype=F32) / n_freq)
    ang = jnp.concatenate([row[:, None] * inv_freq, colp[:, None] * inv_freq], axis=-1)
    cos_h, sin_h = jnp.cos(ang), jnp.sin(ang)
    half = rot_dim // 2
    cos_g = jnp.ones((T, period), F32)
    sa_g = jnp.zeros((T, period), F32)
    sb_g = jnp.zeros((T, period), F32)
    cos_g = cos_g.at[:, lane_lo:lane_lo + rot_dim].set(jnp.concatenate([cos_h, cos_h], axis=-1))
    sa_g = sa_g.at[:, lane_lo:lane_lo + half].set(-sin_h)
    sb_g = sb_g.at[:, lane_lo + half:lane_lo + rot_dim].set(sin_h)
    rep = LANE // period
    return tuple(jnp.tile(t, (1, rep)) * scale for t in (cos_g, sa_g, sb_g))


def _identity_tables(T, scale):
    return (jnp.full((T, LANE), scale, F32), jnp.zeros((T, LANE), F32), jnp.zeros((T, LANE), F32))


def _prep_w_in(w):
    D = w.shape[0]
    o_diff = HGRN_W
    o_cq = o_diff + 1536
    o_ckv = o_cq + C_Q_LORA
    o_kr = o_ckv + C_KV_LORA
    o_gate = o_kr + C_ROPE
    pad = jnp.zeros((D, MLA_KV_IN - C_KV_LORA - C_ROPE), w.dtype)
    return jnp.concatenate([w[:, :o_cq], w[:, o_ckv:o_kr], w[:, o_kr:o_gate], pad, w[:, o_cq:o_ckv],
                            w[:, o_gate:]], axis=1).astype(BF16)


def _prep_mla_w(w_uq, w_ukv):
    wq = w_uq.reshape(C_Q_LORA, C_HEADS, C_NOPE + C_ROPE)
    wq = jnp.pad(wq, ((0, 0), (0, 0), (0, LANE - C_NOPE - C_ROPE))).reshape(C_Q_LORA, C_HEADS * LANE)
    wkv = w_ukv.reshape(C_KV_LORA, C_HEADS, C_NOPE + C_V)
    wk = jnp.pad(wkv[:, :, :C_NOPE], ((0, 0), (0, 0), (0, LANE - C_NOPE)))
    place = jnp.zeros((C_ROPE, C_HEADS, LANE), F32).at[:, :, C_NOPE:C_NOPE + C_ROPE].set(
        jnp.broadcast_to(jnp.eye(C_ROPE, dtype=F32)[:, None, :], (C_ROPE, C_HEADS, C_ROPE)))
    wk = jnp.concatenate([wk, place, jnp.zeros((MLA_KV_IN - C_KV_LORA - C_ROPE, C_HEADS, LANE), F32)], axis=0)
    wv = jnp.pad(wkv[:, :, C_NOPE:].reshape(C_KV_LORA, C_HEADS * C_V), ((0, MLA_KV_IN - C_KV_LORA), (0, 0)))
    wkv_comb = jnp.concatenate([wk.reshape(MLA_KV_IN, C_HEADS * LANE), wv], axis=1)
    return wq.astype(BF16), wkv_comb.astype(BF16)


def kernel(x, c, ctx, c_ctx, ada_w, ada_b, norm_mix_g, norm_ffn_g, w_in, hgrn_lb, hgrn_norm_g, diff_lambda,
           diff_norm_g, mla_q_norm_g, mla_w_uq, mla_kv_norm_g, mla_w_ukv, w_branch, w_out, router_w,
           exp_w_gate, exp_w_up, exp_w_down, final_norm_g):
    B, T, D = x.shape
    Tc = ctx.shape[1]
    depth = ada_w.shape[0]
    tm_l = min(T, 512)
    tm_c = min(Tc, 256)
    tq = min(Tc, 256)

    rows = ((B + 1 + SUBLANE - 1) // SUBLANE) * SUBLANE
    vecs = jnp.concatenate([c, c_ctx[None], jnp.zeros((rows - B - 1, D), F32)], axis=0)
    mods_all = _modulation(vecs, ada_w, ada_b).reshape(depth, rows, 6, D)
    lat_row = lambda b: b
    ctx_row = lambda b: B

    p = jax.nn.softmax(hgrn_lb.astype(F32), axis=0)
    lower_bounds = jnp.clip(jnp.cumsum(p, axis=0) - p[0], 0.0, LB_MAX)

    d_scale = B_HD ** -0.5 * math.log2(math.e)
    c_scale = (C_NOPE + C_ROPE) ** -0.5 * math.log2(math.e)
    dq_l = _rope_tables(T, B_HD, 0, B_HD, d_scale)
    dk_l = _rope_tables(T, B_HD, 0, B_HD, 1.0)
    dq_c, dk_c = _identity_tables(Tc, d_scale), _identity_tables(Tc, 1.0)
    mq_l = _rope_tables(T, C_ROPE, C_NOPE, LANE, c_scale)
    mk_l = _rope_tables(T, C_ROPE, C_NOPE, LANE, 1.0)
    mq_c, mk_c = _identity_tables(Tc, c_scale), _identity_tables(Tc, 1.0)

    xl, xc = x, ctx
    for layer in range(depth):
        with_ctx = layer < depth - 1
        mods = mods_all[layer]
        w_perm = _prep_w_in(w_in[layer])
        wq, wkv = _prep_mla_w(mla_w_uq[layer], mla_w_ukv[layer])
        wb = w_branch[layer].astype(BF16)
        wo = w_out[layer].astype(BF16)
        rw = jnp.pad(router_w[layer], ((0, 0), (0, LANE - N_EXPERTS))).astype(BF16)
        wg = exp_w_gate[layer].astype(BF16)
        wu = exp_w_up[layer].astype(BF16)
        wd = exp_w_down[layer].astype(BF16)
        lam_init = 0.8 - 0.6 * math.exp(-0.3 * layer)
        gq = mla_q_norm_g[layer].reshape(1, -1)
        gkv = jnp.pad(mla_kv_norm_g[layer], (0, MLA_KV_IN - C_KV_LORA)).reshape(1, -1)

        hg_l, dqk_l, dv_l, ml_l, gt_l = _layer_in(xl, norm_mix_g[layer], mods, lat_row, w_perm, dq_l, dk_l, tm_l)
        hg_c, dqk_c, dv_c, ml_c, gt_c = _layer_in(xc, norm_mix_g[layer], mods, ctx_row, w_perm, dq_c, dk_c, tm_c)

        a_l, a_c = _hgrn(hg_l, hg_c, lower_bounds[layer], hgrn_norm_g[layer], with_ctx)

        b_l = _diff_attn(dqk_l, [(dqk_c, dv_c), (dqk_l, dv_l)], diff_lambda[layer], diff_norm_g[layer], lam_init, tq)
        mq_lat, mk_lat, mv_lat = _mla_proj(ml_l, gq, gkv, wq, wkv, mq_l, mk_l, tm_l)
        mq_ctx, mk_ctx, mv_ctx = _mla_proj(ml_c, gq, gkv, wq, wkv, mq_c, mk_c, tm_c)
        m_l = _mla_attn(mq_lat, [(mk_ctx, mv_ctx), (mk_lat, mv_lat)], tq)

        xl = _merge(a_l, b_l, m_l, gt_l, wb, wo, xl, mods, lat_row, tm_l)
        xl = _moe(xl, norm_ffn_g[layer], mods, lat_row, rw, wg, wu, wd, tm_l)
        if with_ctx:
            b_c = _diff_attn(dqk_c, [(dqk_c, dv_c)], diff_lambda[layer], diff_norm_g[layer], lam_init, tq)
            m_c = _mla_attn(mq_ctx, [(mk_ctx, mv_ctx)], tq)
            xc = _merge(a_c, b_c, m_c, gt_c, wb, wo, xc, mods, ctx_row, tm_c)
            xc = _moe(xc, norm_ffn_g[layer], mods, ctx_row, rw, wg, wu, wd, tm_c)
    return _final_norm(xl, final_norm_g, tm_l)
```

```python
import functools
import math

import jax
import jax.numpy as jnp
from jax import lax
from jax.experimental import pallas as pl
from jax.experimental.pallas import tpu as pltpu

F32 = jnp.float32
BF16 = jnp.bfloat16

EPS = 1e-6
ROPE_BASE = 10000.0
GRID_W = 64
F_MIN = 1e-30
LB_MAX = 1.0 - 1e-6

A_HEADS, A_DK = 4, 128
B_HEADS, B_HD = 4, 64
C_HEADS, C_NOPE, C_ROPE, C_V = 8, 64, 32, 64
C_Q_LORA, C_KV_LORA = 384, 256
N_BRANCH, BRANCH_W = 3, 512
N_EXPERTS = 16
CAPACITY_FACTOR = 2

LANE = 128
SUBLANE = 8
VMEM_LIMIT = 56 * 1024 * 1024

HGRN_W = 5 * 512
MLA_W = 768
MLA_KV_IN = 384
GATE_W = N_BRANCH * 1024
HGRN_CHUNK = 128


def _params(*sem):
    return pltpu.CompilerParams(dimension_semantics=sem, vmem_limit_bytes=VMEM_LIMIT)


def _resident(shape, index_map):
    return pl.BlockSpec(shape, index_map, pipeline_mode=pl.Buffered(1))


def _rms(x, eps=EPS):
    return x * lax.rsqrt(jnp.mean(x * x, axis=-1, keepdims=True) + eps)


def _dot(a, b):
    return jnp.dot(a, b, preferred_element_type=F32)


def _dot_nt(a, b):
    return lax.dot_general(a, b, (((1,), (1,)), ((), ())), preferred_element_type=F32)


def _dot_tn(a, b):
    return lax.dot_general(a, b, (((0,), (0,)), ((), ())), preferred_element_type=F32)


def _mods_kernel(v_ref, w_ref, b_ref, o_ref):
    v = v_ref[...]
    s = (v * jax.nn.sigmoid(v)).astype(BF16)
    o_ref[0] = _dot(s, w_ref[0].astype(BF16)) + b_ref[0]


def _modulation(vecs, ada_w, ada_b):
    L, D, N = ada_w.shape
    R = vecs.shape[0]
    tn = 1536
    return pl.pallas_call(
        _mods_kernel,
        grid=(L, N // tn),
        in_specs=[pl.BlockSpec((R, D), lambda l, j: (0, 0)),
                  pl.BlockSpec((1, D, tn), lambda l, j: (l, 0, j)),
                  pl.BlockSpec((1, 1, tn), lambda l, j: (l, 0, j))],
        out_specs=pl.BlockSpec((1, R, tn), lambda l, j: (l, 0, j)),
        out_shape=jax.ShapeDtypeStruct((L, R, N), F32),
        compiler_params=_params("parallel", "parallel"),
        name="adaln_mods",
    )(vecs, ada_w, ada_b.reshape(L, 1, N))


def _rope_apply(x, cos, sin_a, sin_b, half):
    return (x * cos + pltpu.roll(x, LANE - half, axis=1) * sin_a + pltpu.roll(x, half, axis=1) * sin_b)


def _layer_in_kernel(x_ref, g_ref, mod_ref, w_ref, cq_ref, saq_ref, sbq_ref, ck_ref, sak_ref, sbk_ref,
                     hg_ref, dqk_ref, dv_ref, mla_ref, gate_ref):
    x = x_ref[0]
    mod = mod_ref[0]
    h = (_rms(x) * g_ref[...]) * (1.0 + mod[1:2]) + mod[0:1]
    h = h.astype(BF16)
    c0 = 0
    for j in range(HGRN_W // 512):
        hg_ref[0, :, j * 512:(j + 1) * 512] = _dot(h, w_ref[:, c0:c0 + 512])
        c0 += 512
    for j in range(2):
        acc = _dot(h, w_ref[:, c0:c0 + 512])
        cos, sa, sb = (cq_ref, saq_ref, sbq_ref) if j == 0 else (ck_ref, sak_ref, sbk_ref)
        for g in range(4):
            xg = acc[:, g * LANE:(g + 1) * LANE]
            dqk_ref[0, :, j * 512 + g * LANE:j * 512 + (g + 1) * LANE] = _rope_apply(
                xg, cos[...], sa[...], sb[...], B_HD // 2).astype(BF16)
        c0 += 512
    dv_ref[0] = _dot(h, w_ref[:, c0:c0 + 512]).astype(BF16)
    c0 += 512
    mla_ref[0] = _dot(h, w_ref[:, c0:c0 + MLA_W])
    c0 += MLA_W
    for j in range(GATE_W // 512):
        gate_ref[0, :, j * 512:(j + 1) * 512] = jax.nn.sigmoid(_dot(h, w_ref[:, c0:c0 + 512])).astype(BF16)
        c0 += 512


def _layer_in(x, norm_g, mods, mod_row, w_perm, tabs_q, tabs_k, tm):
    Bx, Tx, D = x.shape
    NW = w_perm.shape[1]
    row = lambda b, i: (b, i, 0)
    tab = pl.BlockSpec((tm, LANE), lambda b, i: (i, 0))
    outs = [(HGRN_W, F32), (1024, BF16), (512, BF16), (MLA_W, F32), (GATE_W, BF16)]
    return pl.pallas_call(
        _layer_in_kernel,
        grid=(Bx, Tx // tm),
        in_specs=[pl.BlockSpec((1, tm, D), row),
                  pl.BlockSpec((1, D), lambda b, i: (0, 0)),
                  pl.BlockSpec((1, 6, D), lambda b, i: (mod_row(b), 0, 0)),
                  _resident((D, NW), lambda b, i: (0, 0)),
                  tab, tab, tab, tab, tab, tab],
        out_specs=[pl.BlockSpec((1, tm, w), row) for w, _ in outs],
        out_shape=[jax.ShapeDtypeStruct((Bx, Tx, w), dt) for w, dt in outs],
        compiler_params=_params("parallel", "parallel"),
        name="layer_in",
    )(x, norm_g.reshape(1, D), mods, w_perm, *tabs_q, *tabs_k)


def _cumsum_rows(x, reverse):
    n = x.shape[0]
    s = 1
    while s < n:
        if s < SUBLANE:
            row = lax.broadcasted_iota(jnp.int32, x.shape, 0)
            if reverse:
                sh = jnp.where(row < n - s, pltpu.roll(x, n - s, axis=0), 0.0)
            else:
                sh = jnp.where(row >= s, pltpu.roll(x, s, axis=0), 0.0)
        else:
            z = jnp.zeros((s, x.shape[1]), x.dtype)
            sh = jnp.concatenate([x[s:], z], axis=0) if reverse else jnp.concatenate([z, x[:n - s]], axis=0)
        x = x + sh
        s *= 2
    return x


def _bcast_rows(load_row, rows, reps):
    parts = []
    for r in rows:
        parts += [load_row(r)] * reps
    return jnp.concatenate(parts, axis=0)


def _hgrn_chunk(q, k_in, v, v_row, lf, st, b_scr, k_scr, reverse):
    C = q.shape[0]
    b = _cumsum_rows(lf, reverse)
    b_scr[...] = b
    k_scr[...] = k_in
    b_row = lambda r: jnp.broadcast_to(b_scr[pl.ds(r, 1), :], (SUBLANE, LANE))
    k_row = lambda r: jnp.broadcast_to(k_scr[pl.ds(r, 1), :], (SUBLANE, LANE))
    b_last = b[0:1] if reverse else b[C - 1:C]
    qd = (q * jnp.exp2(b)).astype(BF16)
    o = _dot_nt(qd, st.astype(BF16))
    kd = (k_in * jnp.exp2(b_last - b)).astype(BF16)
    vb = v.astype(BF16)
    st_new = st * jnp.exp2(b_last) + _dot_tn(vb, kd)

    row = lax.broadcasted_iota(jnp.int32, (C, C), 0)
    col = lax.broadcasted_iota(jnp.int32, (C, C), 1)
    r1 = lax.broadcasted_iota(jnp.int32, (C, 1), 0)
    scores = jnp.zeros((C, C), F32)
    m = C // 2
    while m >= SUBLANE:
        mid = m if reverse else m - 1
        anchor = _bcast_rows(b_row, [i * 2 * m + mid for i in range(C // (2 * m))], 2 * m // SUBLANE)
        e = jnp.exp2(-jnp.abs(b - anchor))
        second = (r1 % (2 * m)) >= m
        late = jnp.logical_not(second) if reverse else second
        qh = jnp.where(late, q * e, 0.0).astype(BF16)
        kh = jnp.where(late, 0.0, k_in * e).astype(BF16)
        same = (row // (2 * m)) == (col // (2 * m))
        scores = scores + jnp.where(same, _dot_nt(qh, kh), 0.0)
        m //= 2
    o = o + _dot(scores.astype(BF16), vb)

    t8 = r1 % SUBLANE
    for j in range(SUBLANE):
        rows = [SUBLANE * i + j for i in range(C // SUBLANE)]
        w = q * jnp.exp2(b - _bcast_rows(b_row, rows, 1)) * _bcast_rows(k_row, rows, 1)
        valid = (t8 <= j) if reverse else (t8 >= j)
        sc = jnp.sum(jnp.where(valid, w, 0.0), axis=-1, keepdims=True)
        o = o + sc * _bcast_rows(v_row, rows, 1)
    return o, st_new


def _hgrn_kernel(*refs, with_ctx_out):
    (ql, il, gl, ffl, fbl, qc, ic, gc, ffc, fbc, lb_ref, ng_ref) = refs[:12]
    if with_ctx_out:
        ol_ref, oc_ref, accl, accc, b_scr, k_scr = refs[12:]
    else:
        ol_ref, accl, accc, b_scr, k_scr = refs[12:]
        oc_ref = None
    C = HGRN_CHUNK
    nl = ql.shape[1] // C
    nc = qc.shape[1] // C

    accl[...] = jnp.zeros_like(accl)
    accc[...] = jnp.zeros_like(accc)

    def make_step(qr, ir, frs, acc, n):
        def step(s, sts):
            new = []
            for d in range(2):
                reverse = d == 1
                lbd = lb_ref[d:d + 1, :]
                one_m = 1.0 - lbd
                c = (n - 1 - s) if reverse else s
                r0 = pl.multiple_of(c * C, C)
                sg = jax.nn.sigmoid(frs[d][0, pl.ds(r0, C), :])
                kk = one_m * (1.0 - sg)
                lf = jnp.log2(jnp.maximum(lbd + one_m * sg, F_MIN))
                v_row = lambda r, r0=r0: jnp.broadcast_to(ir[0, pl.ds(r0 + r, 1), :], (SUBLANE, LANE))
                o, st = _hgrn_chunk(qr[0, pl.ds(r0, C), :], kk, ir[0, pl.ds(r0, C), :], v_row, lf, sts[d],
                                    b_scr.at[d], k_scr.at[d], reverse)
                acc[pl.ds(r0, C), :] += o
                new.append(st)
            return tuple(new)
        return step

    zero = jnp.zeros((LANE, LANE), F32)
    sts = lax.fori_loop(0, nc, make_step(qc, ic, (ffc, fbc), accc, nc), (zero, zero))
    lax.fori_loop(0, nl, make_step(ql, il, (ffl, fbl), accl, nl), sts)

    def readout(acc, g_ref, o_ref):
        o = _rms(acc[...]) * ng_ref[...]
        g = g_ref[0]
        o_ref[0] = (o * (g * jax.nn.sigmoid(g))).astype(BF16)

    readout(accl, gl, ol_ref)
    if with_ctx_out:
        readout(accc, gc, oc_ref)


def _hgrn(hg_l, hg_c, lb, norm_g, with_ctx_out):
    B, T, _ = hg_l.shape
    Tc = hg_c.shape[1]
    nh = A_HEADS

    def col(n, t):
        return [pl.BlockSpec((1, t, LANE), (lambda b, h, j=j: (b, 0, j * nh + h))) for j in range(n)]

    in_specs = col(5, T) + col(5, Tc) + [pl.BlockSpec((2, LANE), lambda b, h: (0, h)),
                                         pl.BlockSpec((1, LANE), lambda b, h: (0, h))]
    out_specs = [pl.BlockSpec((1, T, LANE), lambda b, h: (b, 0, h))]
    out_shape = [jax.ShapeDtypeStruct((B, T, nh * LANE), BF16)]
    if with_ctx_out:
        out_specs.append(pl.BlockSpec((1, Tc, LANE), lambda b, h: (b, 0, h)))
        out_shape.append(jax.ShapeDtypeStruct((B, Tc, nh * LANE), BF16))
    res = pl.pallas_call(
        functools.partial(_hgrn_kernel, with_ctx_out=with_ctx_out),
        grid=(B, nh),
        in_specs=in_specs,
        out_specs=out_specs,
        out_shape=out_shape,
        scratch_shapes=[pltpu.VMEM((T, LANE), F32), pltpu.VMEM((Tc, LANE), F32),
                        pltpu.VMEM((2, HGRN_CHUNK, LANE), F32), pltpu.VMEM((2, HGRN_CHUNK, LANE), F32)],
        compiler_params=_params("parallel", "parallel"),
        name="hgrn2",
    )(*([hg_l] * 5), *([hg_c] * 5), lb, norm_g.reshape(1, -1))
    return (res[0], res[1]) if with_ctx_out else (res[0], None)


ATTN_KEY_CHUNK = 256


def _key_chunks(kv):
    off = 0
    for k_ref, v_ref in kv:
        tk = k_ref.shape[1]
        kc = min(tk, ATTN_KEY_CHUNK)
        for c in range(tk // kc):
            yield off, kc, k_ref, v_ref, c
            off += kc


def _score_phase(q, kv, kcols, s_scr):
    mx = None
    for off, kc, k_ref, _, c in _key_chunks(kv):
        s = _dot_nt(q, k_ref[0, c * kc:(c + 1) * kc, kcols])
        s_scr[:, off:off + kc] = s
        for j in range(kc // LANE):
            blk = s[:, j * LANE:(j + 1) * LANE]
            mx = blk if mx is None else jnp.maximum(mx, blk)
        yield None
    yield mx.max(axis=-1, keepdims=True)


def _pv_phase(m, kv, s_scr):
    lv = acc = None
    for off, kc, _, v_ref, c in _key_chunks(kv):
        p = jnp.exp2(s_scr[:, off:off + kc] - m)
        for j in range(kc // LANE):
            blk = p[:, j * LANE:(j + 1) * LANE]
            lv = blk if lv is None else lv + blk
        pv = _dot(p.astype(BF16), v_ref[0, c * kc:(c + 1) * kc, :])
        acc = pv if acc is None else acc + pv
        yield None
    yield acc, lv.sum(axis=-1, keepdims=True)


def _interleave(*gens):
    last = [None] * len(gens)
    live = list(range(len(gens)))
    while live:
        for i in list(live):
            try:
                v = next(gens[i])
                if v is not None:
                    last[i] = v
            except StopIteration:
                live.remove(i)
    return last


def _softmax_pv_pair(items, kv):
    (qa, ca, sa), (qb, cb, sb) = items
    ma, = _interleave(_score_phase(qa, kv, ca, sa))
    mb, ra = _interleave(_score_phase(qb, kv, cb, sb), _pv_phase(ma, kv, sa))
    rb, = _interleave(_pv_phase(mb, kv, sb))
    return ra, rb


def _diff_attn_kernel(*refs, n_kv, lam_init):
    q_ref, lam_ref, ng_ref = refs[0], refs[1], refs[2]
    kv = [(refs[3 + 2 * i], refs[4 + 2 * i]) for i in range(n_kv)]
    o_ref, s1_scr, s2_scr = refs[3 + 2 * n_kv:]
    lp = lam_ref[...]
    lam = (jnp.exp(jnp.sum(lp[0:1] * lp[1:2], axis=-1, keepdims=True))
           - jnp.exp(jnp.sum(lp[2:3] * lp[3:4], axis=-1, keepdims=True)) + lam_init)
    q = q_ref[0]
    lane = lax.broadcasted_iota(jnp.int32, q.shape, 1)
    zero = jnp.zeros_like(q)
    q1 = jnp.where(lane < B_HD, q, zero)
    q2 = jnp.where(lane < B_HD, zero, q)
    (a1, l1), (a2, l2) = _softmax_pv_pair([(q1, slice(None), s1_scr), (q2, slice(None), s2_scr)], kv)
    o = a1 * (1.0 / l1) - a2 * (lam / l2)
    o_ref[0] = (_rms(o) * ng_ref[...] * (1.0 - lam_init)).astype(BF16)


def _diff_attn(q_src, kvs, lam_params, norm_g, lam_init, tq):
    B, Tq, _ = q_src.shape
    nh = B_HEADS
    in_specs = [pl.BlockSpec((1, tq, LANE), lambda b, h, i: (b, i, h)),
                pl.BlockSpec((4, B_HD), lambda b, h, i: (0, 0)),
                pl.BlockSpec((1, LANE), lambda b, h, i: (0, h))]
    args = [q_src, lam_params, norm_g.reshape(1, -1)]
    n_keys = 0
    for k_arr, v_arr in kvs:
        Tk = k_arr.shape[1]
        n_keys += Tk
        in_specs.append(pl.BlockSpec((1, Tk, LANE), lambda b, h, i: (b, 0, nh + h)))
        in_specs.append(pl.BlockSpec((1, Tk, LANE), lambda b, h, i: (b, 0, h)))
        args += [k_arr, v_arr]
    return pl.pallas_call(
        functools.partial(_diff_attn_kernel, n_kv=len(kvs), lam_init=lam_init),
        grid=(B, nh, Tq // tq),
        in_specs=in_specs,
        out_specs=pl.BlockSpec((1, tq, LANE), lambda b, h, i: (b, i, h)),
        out_shape=jax.ShapeDtypeStruct((B, Tq, nh * LANE), BF16),
        scratch_shapes=[pltpu.VMEM((tq, n_keys), F32), pltpu.VMEM((tq, n_keys), F32)],
        compiler_params=_params("parallel", "parallel", "parallel"),
        name="diff_attn",
    )(*args)


def _mla_attn_kernel(*refs, n_kv):
    q_ref = refs[0]
    kv = [(refs[1 + 2 * i], refs[2 + 2 * i]) for i in range(n_kv)]
    o_ref, s1_scr, s2_scr = refs[1 + 2 * n_kv:]
    c0, c1 = slice(0, LANE), slice(LANE, 2 * LANE)
    (a0, l0), (a1, l1) = _softmax_pv_pair([(q_ref[0, :, c0], c0, s1_scr), (q_ref[0, :, c1], c1, s2_scr)], kv)
    lane = lax.broadcasted_iota(jnp.int32, a0.shape, 1)
    o_ref[0] = jnp.where(lane < C_V, a0 * (1.0 / l0), a1 * (1.0 / l1)).astype(BF16)


def _mla_attn(q, kvs, tq):
    B, Tq, _ = q.shape
    npair = C_HEADS // 2
    in_specs = [pl.BlockSpec((1, tq, 2 * LANE), lambda b, h, i: (b, i, h))]
    args = [q]
    n_keys = 0
    for k_arr, v_arr in kvs:
        Tk = k_arr.shape[1]
        n_keys += Tk
        in_specs.append(pl.BlockSpec((1, Tk, 2 * LANE), lambda b, h, i: (b, 0, h)))
        in_specs.append(pl.BlockSpec((1, Tk, LANE), lambda b, h, i: (b, 0, h)))
        args += [k_arr, v_arr]
    return pl.pallas_call(
        functools.partial(_mla_attn_kernel, n_kv=len(kvs)),
        grid=(B, npair, Tq // tq),
        in_specs=in_specs,
        out_specs=pl.BlockSpec((1, tq, LANE), lambda b, h, i: (b, i, h)),
        out_shape=jax.ShapeDtypeStruct((B, Tq, C_HEADS * C_V), BF16),
        scratch_shapes=[pltpu.VMEM((tq, n_keys), F32), pltpu.VMEM((tq, n_keys), F32)],
        compiler_params=_params("parallel", "parallel", "parallel"),
        name="mla_attn",
    )(*args)


def _mla_proj_kernel(x_ref, gq_ref, gkv_ref, wq_ref, wkv_ref, cq_ref, saq_ref, sbq_ref, ck_ref, sak_ref,
                     sbk_ref, q_ref, k_ref, v_ref):
    x = x_ref[0]
    xkv = x[:, :MLA_KV_IN]
    lane = lax.broadcasted_iota(jnp.int32, xkv.shape, 1)
    lat = jnp.where(lane < C_KV_LORA, xkv, 0.0)
    ms = jnp.sum(lat * lat, axis=-1, keepdims=True) * (1.0 / C_KV_LORA)
    hk = jnp.where(lane < C_KV_LORA, lat * lax.rsqrt(ms + EPS) * gkv_ref[...], xkv).astype(BF16)
    hq = (_rms(x[:, MLA_KV_IN:]) * gq_ref[...]).astype(BF16)
    nk = C_HEADS * LANE
    aq = _dot(hq, wq_ref[...])
    akv = _dot(hk, wkv_ref[...])
    for g in range(C_HEADS):
        sl = slice(g * LANE, (g + 1) * LANE)
        q_ref[0, :, sl] = _rope_apply(aq[:, sl], cq_ref[...], saq_ref[...], sbq_ref[...], C_ROPE // 2).astype(BF16)
        k_ref[0, :, sl] = _rope_apply(akv[:, sl], ck_ref[...], sak_ref[...], sbk_ref[...], C_ROPE // 2).astype(BF16)
    v_ref[0] = akv[:, nk:].astype(BF16)


def _mla_proj(mla, gq, gkv, wq, wkv, tabs_q, tabs_k, tm):
    Bx, Tx, _ = mla.shape
    row = lambda b, i: (b, i, 0)
    const = lambda b, i: (0, 0)
    tab = pl.BlockSpec((tm, LANE), lambda b, i: (i, 0))
    nk = C_HEADS * LANE
    outs = [(nk, BF16), (nk, BF16), (C_HEADS * C_V, BF16)]
    return pl.pallas_call(
        _mla_proj_kernel,
        grid=(Bx, Tx // tm),
        in_specs=[pl.BlockSpec((1, tm, MLA_W), row),
                  pl.BlockSpec((1, C_Q_LORA), const),
                  pl.BlockSpec((1, MLA_KV_IN), const),
                  _resident(wq.shape, const),
                  _resident(wkv.shape, const),
                  tab, tab, tab, tab, tab, tab],
        out_specs=[pl.BlockSpec((1, tm, w), row) for w, _ in outs],
        out_shape=[jax.ShapeDtypeStruct((Bx, Tx, w), dt) for w, dt in outs],
        compiler_params=_params("parallel", "parallel"),
        name="mla_proj",
    )(mla, gq, gkv, wq, wkv, *tabs_q, *tabs_k)


def _merge_kernel(a_ref, b_ref, c_ref, gate_ref, wb_ref, wo_ref, x_ref, mod_ref, o_ref):
    D = x_ref.shape[2]
    z = None
    for n, br in enumerate((a_ref, b_ref, c_ref)):
        y = _dot(br[0], wb_ref[n])
        t = gate_ref[0, :, n * D:(n + 1) * D].astype(F32) * y
        z = t if z is None else z + t
    out = _dot(z.astype(BF16), wo_ref[...])
    o_ref[0] = x_ref[0] + mod_ref[0, 2:3] * out


def _merge(a, b, c, gates, wb, wo, x, mods, mod_row, tm):
    Bx, Tx, D = x.shape
    row = lambda bb, i: (bb, i, 0)
    br = pl.BlockSpec((1, tm, BRANCH_W), row)
    return pl.pallas_call(
        _merge_kernel,
        grid=(Bx, Tx // tm),
        in_specs=[br, br, br,
                  pl.BlockSpec((1, tm, GATE_W), row),
                  _resident(wb.shape, lambda bb, i: (0, 0, 0)),
                  _resident(wo.shape, lambda bb, i: (0, 0)),
                  pl.BlockSpec((1, tm, D), row),
                  pl.BlockSpec((1, 6, D), lambda bb, i: (mod_row(bb), 0, 0))],
        out_specs=pl.BlockSpec((1, tm, D), row),
        out_shape=jax.ShapeDtypeStruct((Bx, Tx, D), F32),
        compiler_params=_params("parallel", "parallel"),
        name="merge_out",
    )(a, b, c, gates, wb, wo, x, mods)


def _prefix_count(mask, blk):
    T = mask.shape[0]
    r = lax.broadcasted_iota(jnp.int32, (blk, blk), 0)
    c = lax.broadcasted_iota(jnp.int32, (blk, blk), 1)
    tri = jnp.where(c < r, 1.0, 0.0).astype(BF16)
    parts = []
    carry = jnp.zeros((1, mask.shape[1]), F32)
    for i in range(T // blk):
        mb = mask[i * blk:(i + 1) * blk]
        parts.append(_dot(tri, mb.astype(BF16)) + carry)
        carry = carry + jnp.sum(mb, axis=0, keepdims=True)
    return jnp.concatenate(parts, axis=0) if len(parts) > 1 else parts[0]


def _router_kernel(x_ref, g_ref, mod_ref, rw_ref, h_ref, aff_ref):
    mod = mod_ref[0]
    h = ((_rms(x_ref[0]) * g_ref[...]) * (1.0 + mod[4:5]) + mod[3:4]).astype(BF16)
    h_ref[0] = h
    logits = _dot(h, rw_ref[...])
    lane = lax.broadcasted_iota(jnp.int32, logits.shape, 1)
    lg = jnp.where(lane < N_EXPERTS, logits, -jnp.inf)
    e = jnp.exp(lg - lg.max(axis=-1, keepdims=True))
    aff_ref[0] = e / e.sum(axis=-1, keepdims=True)


def _router(x, norm_g, mods, mod_row, rw, tm):
    Bx, Tx, D = x.shape
    row = lambda b, i: (b, i, 0)
    outs = [(D, BF16), (LANE, F32)]
    return pl.pallas_call(
        _router_kernel,
        grid=(Bx, Tx // tm),
        in_specs=[pl.BlockSpec((1, tm, D), row),
                  pl.BlockSpec((1, D), lambda b, i: (0, 0)),
                  pl.BlockSpec((1, 6, D), lambda b, i: (mod_row(b), 0, 0)),
                  pl.BlockSpec((D, LANE), lambda b, i: (0, 0))],
        out_specs=[pl.BlockSpec((1, tm, w), row) for w, _ in outs],
        out_shape=[jax.ShapeDtypeStruct((Bx, Tx, w), dt) for w, dt in outs],
        compiler_params=_params("parallel", "parallel"),
        name="router",
    )(x, norm_g.reshape(1, D), mods, rw)


def _select_kernel(aff_ref, pos_ref, post_ref, *, cap):
    aff = aff_ref[0]
    T = aff.shape[0]
    real = lax.broadcasted_iota(jnp.int32, aff.shape, 1) < N_EXPERTS
    bits = pltpu.bitcast(aff, jnp.int32)

    def search(i, thr):
        cand = thr | (jnp.int32(1) << (30 - i))
        cnt = jnp.sum(jnp.where(bits >= cand, 1.0, 0.0), axis=0, keepdims=True)
        return jnp.where(cnt >= cap, cand, thr)

    thr = lax.fori_loop(0, 31, search, jnp.zeros((1, LANE), jnp.int32))
    gt = jnp.where(bits > thr, 1.0, 0.0)
    eq = jnp.where(bits == thr, 1.0, 0.0)
    need = cap - jnp.sum(gt, axis=0, keepdims=True)
    blk = min(T, 256)
    sel = gt + eq * jnp.where(_prefix_count(eq, blk) < need, 1.0, 0.0)
    pos = _prefix_count(sel, blk)
    pos = jnp.where(jnp.logical_and(sel > 0.0, real), pos, -1.0)
    pos_ref[0] = pos
    post_ref[0] = jnp.transpose(pos)[:N_EXPERTS]


def _select(aff, cap):
    Bx, Tx, _ = aff.shape
    blk = pl.BlockSpec((1, Tx, LANE), lambda b: (b, 0, 0))
    return pl.pallas_call(
        functools.partial(_select_kernel, cap=cap),
        grid=(Bx,),
        in_specs=[blk],
        out_specs=[blk, pl.BlockSpec((1, N_EXPERTS, Tx), lambda b: (b, 0, 0))],
        out_shape=[jax.ShapeDtypeStruct((Bx, Tx, LANE), F32),
                   jax.ShapeDtypeStruct((Bx, N_EXPERTS, Tx), F32)],
        compiler_params=_params("parallel"),
        name="moe_select",
    )(aff)


DISPATCH_GROUP = 4


def _dispatch_kernel(h_ref, post_ref, xs_ref, *, cap):
    g = pl.program_id(1)
    T = h_ref.shape[1]
    slot = lax.broadcasted_iota(jnp.int32, (cap, T), 0).astype(F32)
    for j in range(DISPATCH_GROUP):
        row = post_ref[0, pl.ds(g * DISPATCH_GROUP + j, 1), :]
        onehot = jnp.where(row == slot, 1.0, 0.0).astype(BF16)
        xs_ref[j] = _dot(onehot, h_ref[0]).astype(BF16)


def _dispatch(h, post, cap):
    Bx, Tx, D = h.shape
    return pl.pallas_call(
        functools.partial(_dispatch_kernel, cap=cap),
        grid=(Bx, N_EXPERTS // DISPATCH_GROUP),
        in_specs=[pl.BlockSpec((1, Tx, D), lambda b, g: (b, 0, 0)),
                  pl.BlockSpec((1, N_EXPERTS, Tx), lambda b, g: (b, 0, 0))],
        out_specs=pl.BlockSpec((DISPATCH_GROUP, cap, D), lambda b, g: (g, b, 0)),
        out_shape=jax.ShapeDtypeStruct((N_EXPERTS, Bx * cap, D), BF16),
        compiler_params=_params("parallel", "parallel"),
        name="moe_dispatch",
    )(h, post)


def _expert_kernel(x_ref, wg_ref, wu_ref, wd_ref, y_ref):
    x = x_ref[0]
    ff = wg_ref.shape[2]
    fc = 512
    acc = None
    for f in range(ff // fc):
        a = _dot(x, wg_ref[0, :, f * fc:(f + 1) * fc])
        u = _dot(x, wu_ref[0, :, f * fc:(f + 1) * fc])
        hm = (a * jax.nn.sigmoid(a) * u).astype(BF16)
        c = _dot(hm, wd_ref[0, f * fc:(f + 1) * fc, :])
        acc = c if acc is None else acc + c
    y_ref[0] = acc.astype(BF16)


def _experts(xs, wg, wu, wd, tm):
    E, M, D = xs.shape
    FF = wg.shape[2]
    return pl.pallas_call(
        _expert_kernel,
        grid=(E, M // tm),
        in_specs=[pl.BlockSpec((1, tm, D), lambda e, i: (e, i, 0)),
                  pl.BlockSpec((1, D, FF), lambda e, i: (e, 0, 0)),
                  pl.BlockSpec((1, D, FF), lambda e, i: (e, 0, 0)),
                  pl.BlockSpec((1, FF, D), lambda e, i: (e, 0, 0))],
        out_specs=pl.BlockSpec((1, tm, D), lambda e, i: (e, i, 0)),
        out_shape=jax.ShapeDtypeStruct((E, M, D), BF16),
        compiler_params=_params("parallel", "parallel"),
        name="moe_experts",
    )(xs, wg, wu, wd)


def _combine_kernel(y_ref, pos_ref, aff_ref, x_ref, mod_ref, o_ref, *, cap, group):
    width = group * cap
    pos = pos_ref[0].astype(BF16)
    gate = aff_ref[0].astype(BF16)
    src = lax.broadcasted_iota(jnp.int32, (LANE, width), 0)
    dst = lax.broadcasted_iota(jnp.int32, (LANE, width), 1)
    slot = (lax.broadcasted_iota(jnp.int32, (1, width), 1) % cap).astype(F32)
    out = None
    for g in range(N_EXPERTS // group):
        spread = jnp.where(dst // cap + g * group == src, 1.0, 0.0).astype(BF16)
        hit = _dot(pos, spread) == slot
        onehot = jnp.where(hit, _dot(gate, spread), 0.0).astype(BF16)
        c = _dot(onehot, y_ref[g * group:(g + 1) * group].reshape(width, y_ref.shape[2]))
        out = c if out is None else out + c
    o_ref[0] = x_ref[0] + mod_ref[0, 5:6] * out


def _combine(y, pos, aff, x, mods, mod_row, cap, tm):
    Bx, Tx, D = x.shape
    row = lambda b, i: (b, i, 0)
    group = max(4, LANE // cap)
    return pl.pallas_call(
        functools.partial(_combine_kernel, cap=cap, group=group),
        grid=(Bx, Tx // tm),
        in_specs=[pl.BlockSpec((N_EXPERTS, cap, D), lambda b, i: (0, b, 0)),
                  pl.BlockSpec((1, tm, LANE), row),
                  pl.BlockSpec((1, tm, LANE), row),
                  pl.BlockSpec((1, tm, D), row),
                  pl.BlockSpec((1, 6, D), lambda b, i: (mod_row(b), 0, 0))],
        out_specs=pl.BlockSpec((1, tm, D), row),
        out_shape=jax.ShapeDtypeStruct((Bx, Tx, D), F32),
        compiler_params=_params("parallel", "parallel"),
        name="moe_combine",
    )(y, pos, aff, x, mods)


def _moe(x, norm_g, mods, mod_row, rw, wg, wu, wd, tm):
    Bx, Tx, D = x.shape
    cap = CAPACITY_FACTOR * Tx // N_EXPERTS
    h, aff = _router(x, norm_g, mods, mod_row, rw, tm)
    pos, post = _select(aff, cap)
    xs = _dispatch(h, post, cap)
    y = _experts(xs, wg, wu, wd, min(Bx * cap, 512))
    return _combine(y, pos, aff, x, mods, mod_row, cap, tm)


def _final_norm_kernel(x_ref, g_ref, o_ref):
    o_ref[0] = _rms(x_ref[0]) * g_ref[...]


def _final_norm(x, g, tm):
    Bx, Tx, D = x.shape
    return pl.pallas_call(
        _final_norm_kernel,
        grid=(Bx, Tx // tm),
        in_specs=[pl.BlockSpec((1, tm, D), lambda b, i: (b, i, 0)),
                  pl.BlockSpec((1, D), lambda b, i: (0, 0))],
        out_specs=pl.BlockSpec((1, tm, D), lambda b, i: (b, i, 0)),
        out_shape=jax.ShapeDtypeStruct((Bx, Tx, D), F32),
        compiler_params=_params("parallel", "parallel"),
        name="final_norm",
    )(x, g.reshape(1, D))


def _rope_tables(T, rot_dim, lane_lo, period, scale):
    rows = T // GRID_W
    row = jnp.repeat(jnp.arange(rows, dtype=F32), GRID_W)
    colp = jnp.tile(jnp.arange(GRID_W, dtype=F32), rows)
    n_freq = rot_dim // 4
    inv_freq = ROPE_BASE ** (-jnp.arange(n_freq, dtype=F32) / n_freq)
    ang = jnp.concatenate([row[:, None] * inv_freq, colp[:, None] * inv_freq], axis=-1)
    cos_h, sin_h = jnp.cos(ang), jnp.sin(ang)
    half = rot_dim // 2
    cos_g = jnp.ones((T, period), F32)
    sa_g = jnp.zeros((T, period), F32)
    sb_g = jnp.zeros((T, period), F32)
    cos_g = cos_g.at[:, lane_lo:lane_lo + rot_dim].set(jnp.concatenate([cos_h, cos_h], axis=-1))
    sa_g = sa_g.at[:, lane_lo:lane_lo + half].set(-sin_h)
    sb_g = sb_g.at[:, lane_lo + half:lane_lo + rot_dim].set(sin_h)
    rep = LANE // period
    return tuple(jnp.tile(t, (1, rep)) * scale for t in (cos_g, sa_g, sb_g))


def _identity_tables(T, scale):
    return (jnp.full((T, LANE), scale, F32), jnp.zeros((T, LANE), F32), jnp.zeros((T, LANE), F32))


def _prep_w_in(w):
    D = w.shape[0]
    o_diff = HGRN_W
    o_cq = o_diff + 1536
    o_ckv = o_cq + C_Q_LORA
    o_kr = o_ckv + C_KV_LORA
    o_gate = o_kr + C_ROPE
    pad = jnp.zeros((D, MLA_KV_IN - C_KV_LORA - C_ROPE), w.dtype)
    return jnp.concatenate([w[:, :o_cq], w[:, o_ckv:o_kr], w[:, o_kr:o_gate], pad, w[:, o_cq:o_ckv],
                            w[:, o_gate:]], axis=1).astype(BF16)


def _prep_mla_w(w_uq, w_ukv):
    wq = w_uq.reshape(C_Q_LORA, C_HEADS, C_NOPE + C_ROPE)
    wq = jnp.pad(wq, ((0, 0), (0, 0), (0, LANE - C_NOPE - C_ROPE))).reshape(C_Q_LORA, C_HEADS * LANE)
    wkv = w_ukv.reshape(C_KV_LORA, C_HEADS, C_NOPE + C_V)
    wk = jnp.pad(wkv[:, :, :C_NOPE], ((0, 0), (0, 0), (0, LANE - C_NOPE)))
    place = jnp.zeros((C_ROPE, C_HEADS, LANE), F32).at[:, :, C_NOPE:C_NOPE + C_ROPE].set(
        jnp.broadcast_to(jnp.eye(C_ROPE, dtype=F32)[:, None, :], (C_ROPE, C_HEADS, C_ROPE)))
    wk = jnp.concatenate([wk, place, jnp.zeros((MLA_KV_IN - C_KV_LORA - C_ROPE, C_HEADS, LANE), F32)], axis=0)
    wv = jnp.pad(wkv[:, :, C_NOPE:].reshape(C_KV_LORA, C_HEADS * C_V), ((0, MLA_KV_IN - C_KV_LORA), (0, 0)))
    wkv_comb = jnp.concatenate([wk.reshape(MLA_KV_IN, C_HEADS * LANE), wv], axis=1)
    return wq.astype(BF16), wkv_comb.astype(BF16)


def kernel(x, c, ctx, c_ctx, ada_w, ada_b, norm_mix_g, norm_ffn_g, w_in, hgrn_lb, hgrn_norm_g, diff_lambda,
           diff_norm_g, mla_q_norm_g, mla_w_uq, mla_kv_norm_g, mla_w_ukv, w_branch, w_out, router_w,
           exp_w_gate, exp_w_up, exp_w_down, final_norm_g):
    B, T, D = x.shape
    Tc = ctx.shape[1]
    depth = ada_w.shape[0]
    tm_l = min(T, 512)
    tm_c = min(Tc, 256)
    tq = min(Tc, 256)

    rows = ((B + 1 + SUBLANE - 1) // SUBLANE) * SUBLANE
    vecs = jnp.concatenate([c, c_ctx[None], jnp.zeros((rows - B - 1, D), F32)], axis=0)
    mods_all = _modulation(vecs, ada_w, ada_b).reshape(depth, rows, 6, D)
    lat_row = lambda b: b
    ctx_row = lambda b: B

    p = jax.nn.softmax(hgrn_lb.astype(F32), axis=0)
    lower_bounds = jnp.clip(jnp.cumsum(p, axis=0) - p[0], 0.0, LB_MAX)

    d_scale = B_HD ** -0.5 * math.log2(math.e)
    c_scale = (C_NOPE + C_ROPE) ** -0.5 * math.log2(math.e)
    dq_l = _rope_tables(T, B_HD, 0, B_HD, d_scale)
    dk_l = _rope_tables(T, B_HD, 0, B_HD, 1.0)
    dq_c, dk_c = _identity_tables(Tc, d_scale), _identity_tables(Tc, 1.0)
    mq_l = _rope_tables(T, C_ROPE, C_NOPE, LANE, c_scale)
    mk_l = _rope_tables(T, C_ROPE, C_NOPE, LANE, 1.0)
    mq_c, mk_c = _identity_tables(Tc, c_scale), _identity_tables(Tc, 1.0)

    xl, xc = x, ctx
    for layer in range(depth):
        with_ctx = layer < depth - 1
        mods = mods_all[layer]
        w_perm = _prep_w_in(w_in[layer])
        wq, wkv = _prep_mla_w(mla_w_uq[layer], mla_w_ukv[layer])
        wb = w_branch[layer].astype(BF16)
        wo = w_out[layer].astype(BF16)
        rw = jnp.pad(router_w[layer], ((0, 0), (0, LANE - N_EXPERTS))).astype(BF16)
        wg = exp_w_gate[layer].astype(BF16)
        wu = exp_w_up[layer].astype(BF16)
        wd = exp_w_down[layer].astype(BF16)
        lam_init = 0.8 - 0.6 * math.exp(-0.3 * layer)
        gq = mla_q_norm_g[layer].reshape(1, -1)
        gkv = jnp.pad(mla_kv_norm_g[layer], (0, MLA_KV_IN - C_KV_LORA)).reshape(1, -1)

        hg_l, dqk_l, dv_l, ml_l, gt_l = _layer_in(xl, norm_mix_g[layer], mods, lat_row, w_perm, dq_l, dk_l, tm_l)
        hg_c, dqk_c, dv_c, ml_c, gt_c = _layer_in(xc, norm_mix_g[layer], mods, ctx_row, w_perm, dq_c, dk_c, tm_c)

        a_l, a_c = _hgrn(hg_l, hg_c, lower_bounds[layer], hgrn_norm_g[layer], with_ctx)

        b_l = _diff_attn(dqk_l, [(dqk_c, dv_c), (dqk_l, dv_l)], diff_lambda[layer], diff_norm_g[layer], lam_init, tm_l)
        mq_lat, mk_lat, mv_lat = _mla_proj(ml_l, gq, gkv, wq, wkv, mq_l, mk_l, tm_l)
        mq_ctx, mk_ctx, mv_ctx = _mla_proj(ml_c, gq, gkv, wq, wkv, mq_c, mk_c, tm_c)
        m_l = _mla_attn(mq_lat, [(mk_ctx, mv_ctx), (mk_lat, mv_lat)], tm_l)

        xl = _merge(a_l, b_l, m_l, gt_l, wb, wo, xl, mods, lat_row, tm_l)
        xl = _moe(xl, norm_ffn_g[layer], mods, lat_row, rw, wg, wu, wd, tm_l)
        if with_ctx:
            b_c = _diff_attn(dqk_c, [(dqk_c, dv_c)], diff_lambda[layer], diff_norm_g[layer], lam_init, tq)
            m_c = _mla_attn(mq_ctx, [(mk_ctx, mv_ctx)], tq)
            xc = _merge(a_c, b_c, m_c, gt_c, wb, wo, xc, mods, ctx_row, tm_c)
            xc = _moe(xc, norm_ffn_g[layer], mods, ctx_row, rw, wg, wu, wd, tm_c)
    return _final_norm(xl, final_norm_g, tm_l)
```

```python
import functools
import math

import jax
import jax.numpy as jnp
from jax import lax
from jax.experimental import pallas as pl
from jax.experimental.pallas import tpu as pltpu

F32 = jnp.float32
BF16 = jnp.bfloat16

EPS = 1e-6
ROPE_BASE = 10000.0
GRID_W = 64
F_MIN = 1e-30
LB_MAX = 1.0 - 1e-6

A_HEADS, A_DK = 4, 128
B_HEADS, B_HD = 4, 64
C_HEADS, C_NOPE, C_ROPE, C_V = 8, 64, 32, 64
C_Q_LORA, C_KV_LORA = 384, 256
N_BRANCH, BRANCH_W = 3, 512
N_EXPERTS = 16
CAPACITY_FACTOR = 2

LANE = 128
SUBLANE = 8
VMEM_LIMIT = 56 * 1024 * 1024

HGRN_W = 5 * 512
MLA_W = 768
MLA_KV_IN = 384
GATE_W = N_BRANCH * 1024
HGRN_CHUNK = 128


def _params(*sem):
    return pltpu.CompilerParams(dimension_semantics=sem, vmem_limit_bytes=VMEM_LIMIT)


def _resident(shape, index_map):
    return pl.BlockSpec(shape, index_map, pipeline_mode=pl.Buffered(1))


def _rms(x, eps=EPS):
    return x * lax.rsqrt(jnp.mean(x * x, axis=-1, keepdims=True) + eps)


def _dot(a, b):
    return jnp.dot(a, b, preferred_element_type=F32)


def _dot_nt(a, b):
    return lax.dot_general(a, b, (((1,), (1,)), ((), ())), preferred_element_type=F32)


def _dot_tn(a, b):
    return lax.dot_general(a, b, (((0,), (0,)), ((), ())), preferred_element_type=F32)


def _mods_kernel(v_ref, w_ref, b_ref, o_ref):
    v = v_ref[...]
    s = (v * jax.nn.sigmoid(v)).astype(BF16)
    o_ref[0] = _dot(s, w_ref[0].astype(BF16)) + b_ref[0]


def _modulation(vecs, ada_w, ada_b):
    L, D, N = ada_w.shape
    R = vecs.shape[0]
    tn = 1536
    return pl.pallas_call(
        _mods_kernel,
        grid=(L, N // tn),
        in_specs=[pl.BlockSpec((R, D), lambda l, j: (0, 0)),
                  pl.BlockSpec((1, D, tn), lambda l, j: (l, 0, j)),
                  pl.BlockSpec((1, 1, tn), lambda l, j: (l, 0, j))],
        out_specs=pl.BlockSpec((1, R, tn), lambda l, j: (l, 0, j)),
        out_shape=jax.ShapeDtypeStruct((L, R, N), F32),
        compiler_params=_params("parallel", "parallel"),
        name="adaln_mods",
    )(vecs, ada_w, ada_b.reshape(L, 1, N))


def _rope_apply(x, cos, sin_a, sin_b, half):
    return (x * cos + pltpu.roll(x, LANE - half, axis=1) * sin_a + pltpu.roll(x, half, axis=1) * sin_b)


def _layer_in_kernel(x_ref, g_ref, mod_ref, w_ref, cq_ref, saq_ref, sbq_ref, ck_ref, sak_ref, sbk_ref,
                     hg_ref, dqk_ref, dv_ref, mla_ref, gate_ref):
    x = x_ref[0]
    mod = mod_ref[0]
    h = (_rms(x) * g_ref[...]) * (1.0 + mod[1:2]) + mod[0:1]
    h = h.astype(BF16)
    c0 = 0
    for j in range(HGRN_W // 512):
        hg_ref[0, :, j * 512:(j + 1) * 512] = _dot(h, w_ref[:, c0:c0 + 512])
        c0 += 512
    for j in range(2):
        acc = _dot(h, w_ref[:, c0:c0 + 512])
        cos, sa, sb = (cq_ref, saq_ref, sbq_ref) if j == 0 else (ck_ref, sak_ref, sbk_ref)
        for g in range(4):
            xg = acc[:, g * LANE:(g + 1) * LANE]
            dqk_ref[0, :, j * 512 + g * LANE:j * 512 + (g + 1) * LANE] = _rope_apply(
                xg, cos[...], sa[...], sb[...], B_HD // 2).astype(BF16)
        c0 += 512
    dv_ref[0] = jnp.transpose(_dot(h, w_ref[:, c0:c0 + 512])).astype(BF16)
    c0 += 512
    mla_ref[0] = _dot(h, w_ref[:, c0:c0 + MLA_W])
    c0 += MLA_W
    for j in range(GATE_W // 512):
        gate_ref[0, :, j * 512:(j + 1) * 512] = jax.nn.sigmoid(_dot(h, w_ref[:, c0:c0 + 512])).astype(BF16)
        c0 += 512


def _out_spec(tm, width, transposed):
    if transposed:
        return pl.BlockSpec((1, width, tm), lambda b, i: (b, 0, i))
    return pl.BlockSpec((1, tm, width), lambda b, i: (b, i, 0))


def _out_shape(bx, tx, width, dtype, transposed):
    return jax.ShapeDtypeStruct((bx, width, tx) if transposed else (bx, tx, width), dtype)


def _layer_in(x, norm_g, mods, mod_row, w_perm, tabs_q, tabs_k, tm):
    Bx, Tx, D = x.shape
    NW = w_perm.shape[1]
    row = lambda b, i: (b, i, 0)
    tab = pl.BlockSpec((tm, LANE), lambda b, i: (i, 0))
    outs = [(HGRN_W, F32, False), (1024, BF16, False), (512, BF16, True), (MLA_W, F32, False),
            (GATE_W, BF16, False)]
    return pl.pallas_call(
        _layer_in_kernel,
        grid=(Bx, Tx // tm),
        in_specs=[pl.BlockSpec((1, tm, D), row),
                  pl.BlockSpec((1, D), lambda b, i: (0, 0)),
                  pl.BlockSpec((1, 6, D), lambda b, i: (mod_row(b), 0, 0)),
                  _resident((D, NW), lambda b, i: (0, 0)),
                  tab, tab, tab, tab, tab, tab],
        out_specs=[_out_spec(tm, w, t) for w, _, t in outs],
        out_shape=[_out_shape(Bx, Tx, w, dt, t) for w, dt, t in outs],
        compiler_params=_params("parallel", "parallel"),
        name="layer_in",
    )(x, norm_g.reshape(1, D), mods, w_perm, *tabs_q, *tabs_k)


def _cumsum_rows(x, reverse):
    n = x.shape[0]
    s = 1
    while s < n:
        if s < SUBLANE:
            row = lax.broadcasted_iota(jnp.int32, x.shape, 0)
            if reverse:
                sh = jnp.where(row < n - s, pltpu.roll(x, n - s, axis=0), 0.0)
            else:
                sh = jnp.where(row >= s, pltpu.roll(x, s, axis=0), 0.0)
        else:
            z = jnp.zeros((s, x.shape[1]), x.dtype)
            sh = jnp.concatenate([x[s:], z], axis=0) if reverse else jnp.concatenate([z, x[:n - s]], axis=0)
        x = x + sh
        s *= 2
    return x


def _bcast_rows(load_row, rows, reps):
    parts = []
    for r in rows:
        parts += [load_row(r)] * reps
    return jnp.concatenate(parts, axis=0)


def _hgrn_chunk(q, k_in, v, v_row, lf, st, b_scr, k_scr, reverse):
    C = q.shape[0]
    b = _cumsum_rows(lf, reverse)
    b_scr[...] = b
    k_scr[...] = k_in
    b_row = lambda r: jnp.broadcast_to(b_scr[pl.ds(r, 1), :], (SUBLANE, LANE))
    k_row = lambda r: jnp.broadcast_to(k_scr[pl.ds(r, 1), :], (SUBLANE, LANE))
    b_last = b[0:1] if reverse else b[C - 1:C]
    qd = (q * jnp.exp2(b)).astype(BF16)
    o = _dot_nt(qd, st.astype(BF16))
    kd = (k_in * jnp.exp2(b_last - b)).astype(BF16)
    vb = v.astype(BF16)
    st_new = st * jnp.exp2(b_last) + _dot_tn(vb, kd)

    row = lax.broadcasted_iota(jnp.int32, (C, C), 0)
    col = lax.broadcasted_iota(jnp.int32, (C, C), 1)
    r1 = lax.broadcasted_iota(jnp.int32, (C, 1), 0)
    scores = jnp.zeros((C, C), F32)
    m = C // 2
    while m >= SUBLANE:
        mid = m if reverse else m - 1
        anchor = _bcast_rows(b_row, [i * 2 * m + mid for i in range(C // (2 * m))], 2 * m // SUBLANE)
        e = jnp.exp2(-jnp.abs(b - anchor))
        second = (r1 % (2 * m)) >= m
        late = jnp.logical_not(second) if reverse else second
        qh = jnp.where(late, q * e, 0.0).astype(BF16)
        kh = jnp.where(late, 0.0, k_in * e).astype(BF16)
        same = (row // (2 * m)) == (col // (2 * m))
        scores = scores + jnp.where(same, _dot_nt(qh, kh), 0.0)
        m //= 2
    o = o + _dot(scores.astype(BF16), vb)

    t8 = r1 % SUBLANE
    for j in range(SUBLANE):
        rows = [SUBLANE * i + j for i in range(C // SUBLANE)]
        w = q * jnp.exp2(b - _bcast_rows(b_row, rows, 1)) * _bcast_rows(k_row, rows, 1)
        valid = (t8 <= j) if reverse else (t8 >= j)
        sc = jnp.sum(jnp.where(valid, w, 0.0), axis=-1, keepdims=True)
        o = o + sc * _bcast_rows(v_row, rows, 1)
    return o, st_new


def _hgrn_kernel(*refs, with_ctx_out):
    (ql, il, gl, ffl, fbl, qc, ic, gc, ffc, fbc, lb_ref, ng_ref) = refs[:12]
    if with_ctx_out:
        ol_ref, oc_ref, accl, accc, b_scr, k_scr = refs[12:]
    else:
        ol_ref, accl, accc, b_scr, k_scr = refs[12:]
        oc_ref = None
    C = HGRN_CHUNK
    nl = ql.shape[1] // C
    nc = qc.shape[1] // C

    accl[...] = jnp.zeros_like(accl)
    accc[...] = jnp.zeros_like(accc)

    def make_step(qr, ir, frs, acc, n):
        def step(s, sts):
            new = []
            for d in range(2):
                reverse = d == 1
                lbd = lb_ref[d:d + 1, :]
                one_m = 1.0 - lbd
                c = (n - 1 - s) if reverse else s
                r0 = pl.multiple_of(c * C, C)
                sg = jax.nn.sigmoid(frs[d][0, pl.ds(r0, C), :])
                kk = one_m * (1.0 - sg)
                lf = jnp.log2(jnp.maximum(lbd + one_m * sg, F_MIN))
                v_row = lambda r, r0=r0: jnp.broadcast_to(ir[0, pl.ds(r0 + r, 1), :], (SUBLANE, LANE))
                o, st = _hgrn_chunk(qr[0, pl.ds(r0, C), :], kk, ir[0, pl.ds(r0, C), :], v_row, lf, sts[d],
                                    b_scr.at[d], k_scr.at[d], reverse)
                acc[pl.ds(r0, C), :] += o
                new.append(st)
            return tuple(new)
        return step

    zero = jnp.zeros((LANE, LANE), F32)
    sts = lax.fori_loop(0, nc, make_step(qc, ic, (ffc, fbc), accc, nc), (zero, zero))
    lax.fori_loop(0, nl, make_step(ql, il, (ffl, fbl), accl, nl), sts)

    def readout(acc, g_ref, o_ref):
        o = _rms(acc[...]) * ng_ref[...]
        g = g_ref[0]
        o_ref[0] = (o * (g * jax.nn.sigmoid(g))).astype(BF16)

    readout(accl, gl, ol_ref)
    if with_ctx_out:
        readout(accc, gc, oc_ref)


def _hgrn(hg_l, hg_c, lb, norm_g, with_ctx_out):
    B, T, _ = hg_l.shape
    Tc = hg_c.shape[1]
    nh = A_HEADS

    def col(n, t):
        return [pl.BlockSpec((1, t, LANE), (lambda b, h, j=j: (b, 0, j * nh + h))) for j in range(n)]

    in_specs = col(5, T) + col(5, Tc) + [pl.BlockSpec((2, LANE), lambda b, h: (0, h)),
                                         pl.BlockSpec((1, LANE), lambda b, h: (0, h))]
    out_specs = [pl.BlockSpec((1, T, LANE), lambda b, h: (b, 0, h))]
    out_shape = [jax.ShapeDtypeStruct((B, T, nh * LANE), BF16)]
    if with_ctx_out:
        out_specs.append(pl.BlockSpec((1, Tc, LANE), lambda b, h: (b, 0, h)))
        out_shape.append(jax.ShapeDtypeStruct((B, Tc, nh * LANE), BF16))
    res = pl.pallas_call(
        functools.partial(_hgrn_kernel, with_ctx_out=with_ctx_out),
        grid=(B, nh),
        in_specs=in_specs,
        out_specs=out_specs,
        out_shape=out_shape,
        scratch_shapes=[pltpu.VMEM((T, LANE), F32), pltpu.VMEM((Tc, LANE), F32),
                        pltpu.VMEM((2, HGRN_CHUNK, LANE), F32), pltpu.VMEM((2, HGRN_CHUNK, LANE), F32)],
        compiler_params=_params("parallel", "parallel"),
        name="hgrn2",
    )(*([hg_l] * 5), *([hg_c] * 5), lb, norm_g.reshape(1, -1))
    return (res[0], res[1]) if with_ctx_out else (res[0], None)


ATTN_KEY_CHUNK = 1024


def _key_chunks(kv):
    off = 0
    for k_ref, v_ref in kv:
        tk = k_ref.shape[1]
        kc = min(tk, ATTN_KEY_CHUNK)
        for c in range(tk // kc):
            yield off, kc, k_ref, v_ref, c
            off += kc


def _score_phase(q, kv, kcols, s_scr):
    mx = None
    for off, kc, k_ref, _, c in _key_chunks(kv):
        s = _dot_nt(k_ref[0, c * kc:(c + 1) * kc, kcols], q)
        s_scr[off:off + kc, :] = s
        cm = s.max(axis=0, keepdims=True)
        mx = cm if mx is None else jnp.maximum(mx, cm)
        yield None
    yield mx


def _pv_phase(m, kv, vrows, s_scr):
    lv = acc = None
    for off, kc, _, v_ref, c in _key_chunks(kv):
        p = jnp.exp2(s_scr[off:off + kc, :] - m)
        ps = p.sum(axis=0, keepdims=True)
        lv = ps if lv is None else lv + ps
        pv = _dot(v_ref[0, vrows, c * kc:(c + 1) * kc], p.astype(BF16))
        acc = pv if acc is None else acc + pv
        yield None
    yield acc, lv


def _interleave(*gens):
    last = [None] * len(gens)
    live = list(range(len(gens)))
    while live:
        for i in list(live):
            try:
                v = next(gens[i])
                if v is not None:
                    last[i] = v
            except StopIteration:
                live.remove(i)
    return last


ATTN_SUB_Q = 512


def _softmax_pv_items(items, kv, scrs):
    out = []
    m_prev = None
    for i, (q, kc, vr) in enumerate(items):
        score = _score_phase(q, kv, kc, scrs[i % 2])
        if i == 0:
            m_prev, = _interleave(score)
        else:
            m_prev, res = _interleave(score, _pv_phase(m_prev, kv, items[i - 1][2], scrs[(i - 1) % 2]))
            out.append(res)
    res, = _interleave(_pv_phase(m_prev, kv, items[-1][2], scrs[(len(items) - 1) % 2]))
    out.append(res)
    return out


def _diff_attn_kernel(*refs, n_kv, lam_init):
    q_ref, lam_ref, ng_ref = refs[0], refs[1], refs[2]
    kv = [(refs[3 + 2 * i], refs[4 + 2 * i]) for i in range(n_kv)]
    o_ref, s1_scr, s2_scr = refs[3 + 2 * n_kv:]
    lp = lam_ref[...]
    lam = (jnp.exp(jnp.sum(lp[0:1] * lp[1:2], axis=-1, keepdims=True))
           - jnp.exp(jnp.sum(lp[2:3] * lp[3:4], axis=-1, keepdims=True)) + lam_init)
    sub = s1_scr.shape[1]
    full = slice(None)
    items = []
    for r in range(q_ref.shape[1] // sub):
        q = q_ref[0, r * sub:(r + 1) * sub, :]
        lane = lax.broadcasted_iota(jnp.int32, q.shape, 1)
        zero = jnp.zeros_like(q)
        items += [(jnp.where(lane < B_HD, q, zero), full, full), (jnp.where(lane < B_HD, zero, q), full, full)]
    res = _softmax_pv_items(items, kv, (s1_scr, s2_scr))
    for r in range(q_ref.shape[1] // sub):
        (a1, l1), (a2, l2) = res[2 * r], res[2 * r + 1]
        o = jnp.transpose(a1 * (1.0 / l1) - a2 * (lam / l2))
        o_ref[0, r * sub:(r + 1) * sub, :] = (_rms(o) * ng_ref[...] * (1.0 - lam_init)).astype(BF16)


def _diff_attn(q_src, kvs, lam_params, norm_g, lam_init, tq):
    B, Tq, _ = q_src.shape
    nh = B_HEADS
    in_specs = [pl.BlockSpec((1, tq, LANE), lambda b, h, i: (b, i, h)),
                pl.BlockSpec((4, B_HD), lambda b, h, i: (0, 0)),
                pl.BlockSpec((1, LANE), lambda b, h, i: (0, h))]
    args = [q_src, lam_params, norm_g.reshape(1, -1)]
    n_keys = 0
    for k_arr, v_arr in kvs:
        Tk = k_arr.shape[1]
        n_keys += Tk
        in_specs.append(pl.BlockSpec((1, Tk, LANE), lambda b, h, i: (b, 0, nh + h)))
        in_specs.append(pl.BlockSpec((1, LANE, Tk), lambda b, h, i: (b, h, 0)))
        args += [k_arr, v_arr]
    return pl.pallas_call(
        functools.partial(_diff_attn_kernel, n_kv=len(kvs), lam_init=lam_init),
        grid=(B, nh, Tq // tq),
        in_specs=in_specs,
        out_specs=pl.BlockSpec((1, tq, LANE), lambda b, h, i: (b, i, h)),
        out_shape=jax.ShapeDtypeStruct((B, Tq, nh * LANE), BF16),
        scratch_shapes=[pltpu.VMEM((n_keys, min(tq, ATTN_SUB_Q)), F32)] * 2,
        compiler_params=_params("parallel", "parallel", "parallel"),
        name="diff_attn",
    )(*args)


def _mla_attn_kernel(*refs, n_kv):
    q_ref = refs[0]
    kv = [(refs[1 + 2 * i], refs[2 + 2 * i]) for i in range(n_kv)]
    o_ref, s1_scr, s2_scr = refs[1 + 2 * n_kv:]
    sub = s1_scr.shape[1]
    c0, c1 = slice(0, LANE), slice(LANE, 2 * LANE)
    v0, v1 = slice(0, C_V), slice(C_V, 2 * C_V)
    items = []
    for r in range(q_ref.shape[1] // sub):
        rows = slice(r * sub, (r + 1) * sub)
        items += [(q_ref[0, rows, c0], c0, v0), (q_ref[0, rows, c1], c1, v1)]
    res = _softmax_pv_items(items, kv, (s1_scr, s2_scr))
    for r in range(q_ref.shape[1] // sub):
        (a0, l0), (a1, l1) = res[2 * r], res[2 * r + 1]
        ot = jnp.concatenate([a0 * (1.0 / l0), a1 * (1.0 / l1)], axis=0)
        o_ref[0, r * sub:(r + 1) * sub, :] = jnp.transpose(ot).astype(BF16)


def _mla_attn(q, kvs, tq):
    B, Tq, _ = q.shape
    npair = C_HEADS // 2
    in_specs = [pl.BlockSpec((1, tq, 2 * LANE), lambda b, h, i: (b, i, h))]
    args = [q]
    n_keys = 0
    for k_arr, v_arr in kvs:
        Tk = k_arr.shape[1]
        n_keys += Tk
        in_specs.append(pl.BlockSpec((1, Tk, 2 * LANE), lambda b, h, i: (b, 0, h)))
        in_specs.append(pl.BlockSpec((1, LANE, Tk), lambda b, h, i: (b, h, 0)))
        args += [k_arr, v_arr]
    return pl.pallas_call(
        functools.partial(_mla_attn_kernel, n_kv=len(kvs)),
        grid=(B, npair, Tq // tq),
        in_specs=in_specs,
        out_specs=pl.BlockSpec((1, tq, LANE), lambda b, h, i: (b, i, h)),
        out_shape=jax.ShapeDtypeStruct((B, Tq, C_HEADS * C_V), BF16),
        scratch_shapes=[pltpu.VMEM((n_keys, min(tq, ATTN_SUB_Q)), F32)] * 2,
        compiler_params=_params("parallel", "parallel", "parallel"),
        name="mla_attn",
    )(*args)


def _mla_proj_kernel(x_ref, gq_ref, gkv_ref, wq_ref, wkv_ref, cq_ref, saq_ref, sbq_ref, ck_ref, sak_ref,
                     sbk_ref, q_ref, k_ref, v_ref):
    x = x_ref[0]
    xkv = x[:, :MLA_KV_IN]
    lane = lax.broadcasted_iota(jnp.int32, xkv.shape, 1)
    lat = jnp.where(lane < C_KV_LORA, xkv, 0.0)
    ms = jnp.sum(lat * lat, axis=-1, keepdims=True) * (1.0 / C_KV_LORA)
    hk = jnp.where(lane < C_KV_LORA, lat * lax.rsqrt(ms + EPS) * gkv_ref[...], xkv).astype(BF16)
    hq = (_rms(x[:, MLA_KV_IN:]) * gq_ref[...]).astype(BF16)
    nk = C_HEADS * LANE
    aq = _dot(hq, wq_ref[...])
    akv = _dot(hk, wkv_ref[...])
    for g in range(C_HEADS):
        sl = slice(g * LANE, (g + 1) * LANE)
        q_ref[0, :, sl] = _rope_apply(aq[:, sl], cq_ref[...], saq_ref[...], sbq_ref[...], C_ROPE // 2).astype(BF16)
        k_ref[0, :, sl] = _rope_apply(akv[:, sl], ck_ref[...], sak_ref[...], sbk_ref[...], C_ROPE // 2).astype(BF16)
    v_ref[0] = jnp.transpose(akv[:, nk:]).astype(BF16)


def _mla_proj(mla, gq, gkv, wq, wkv, tabs_q, tabs_k, tm):
    Bx, Tx, _ = mla.shape
    row = lambda b, i: (b, i, 0)
    const = lambda b, i: (0, 0)
    tab = pl.BlockSpec((tm, LANE), lambda b, i: (i, 0))
    nk = C_HEADS * LANE
    outs = [(nk, BF16, False), (nk, BF16, False), (C_HEADS * C_V, BF16, True)]
    return pl.pallas_call(
        _mla_proj_kernel,
        grid=(Bx, Tx // tm),
        in_specs=[pl.BlockSpec((1, tm, MLA_W), row),
                  pl.BlockSpec((1, C_Q_LORA), const),
                  pl.BlockSpec((1, MLA_KV_IN), const),
                  _resident(wq.shape, const),
                  _resident(wkv.shape, const),
                  tab, tab, tab, tab, tab, tab],
        out_specs=[_out_spec(tm, w, t) for w, _, t in outs],
        out_shape=[_out_shape(Bx, Tx, w, dt, t) for w, dt, t in outs],
        compiler_params=_params("parallel", "parallel"),
        name="mla_proj",
    )(mla, gq, gkv, wq, wkv, *tabs_q, *tabs_k)


def _merge_kernel(a_ref, b_ref, c_ref, gate_ref, wb_ref, wo_ref, x_ref, mod_ref, o_ref):
    D = x_ref.shape[2]
    z = None
    for n, br in enumerate((a_ref, b_ref, c_ref)):
        y = _dot(br[0], wb_ref[n])
        t = gate_ref[0, :, n * D:(n + 1) * D].astype(F32) * y
        z = t if z is None else z + t
    out = _dot(z.astype(BF16), wo_ref[...])
    o_ref[0] = x_ref[0] + mod_ref[0, 2:3] * out


def _merge(a, b, c, gates, wb, wo, x, mods, mod_row, tm):
    Bx, Tx, D = x.shape
    row = lambda bb, i: (bb, i, 0)
    br = pl.BlockSpec((1, tm, BRANCH_W), row)
    return pl.pallas_call(
        _merge_kernel,
        grid=(Bx, Tx // tm),
        in_specs=[br, br, br,
                  pl.BlockSpec((1, tm, GATE_W), row),
                  _resident(wb.shape, lambda bb, i: (0, 0, 0)),
                  _resident(wo.shape, lambda bb, i: (0, 0)),
                  pl.BlockSpec((1, tm, D), row),
                  pl.BlockSpec((1, 6, D), lambda bb, i: (mod_row(bb), 0, 0))],
        out_specs=pl.BlockSpec((1, tm, D), row),
        out_shape=jax.ShapeDtypeStruct((Bx, Tx, D), F32),
        compiler_params=_params("parallel", "parallel"),
        name="merge_out",
    )(a, b, c, gates, wb, wo, x, mods)


def _prefix_count(mask, blk):
    T = mask.shape[0]
    r = lax.broadcasted_iota(jnp.int32, (blk, blk), 0)
    c = lax.broadcasted_iota(jnp.int32, (blk, blk), 1)
    tri = jnp.where(c < r, 1.0, 0.0).astype(BF16)
    parts = []
    carry = jnp.zeros((1, mask.shape[1]), F32)
    for i in range(T // blk):
        mb = mask[i * blk:(i + 1) * blk]
        parts.append(_dot(tri, mb.astype(BF16)) + carry)
        carry = carry + jnp.sum(mb, axis=0, keepdims=True)
    return jnp.concatenate(parts, axis=0) if len(parts) > 1 else parts[0]


def _router_kernel(x_ref, g_ref, mod_ref, rw_ref, h_ref, aff_ref):
    mod = mod_ref[0]
    h = ((_rms(x_ref[0]) * g_ref[...]) * (1.0 + mod[4:5]) + mod[3:4]).astype(BF16)
    h_ref[0] = h
    logits = _dot(h, rw_ref[...])
    lane = lax.broadcasted_iota(jnp.int32, logits.shape, 1)
    lg = jnp.where(lane < N_EXPERTS, logits, -jnp.inf)
    e = jnp.exp(lg - lg.max(axis=-1, keepdims=True))
    aff_ref[0] = e / e.sum(axis=-1, keepdims=True)


def _router(x, norm_g, mods, mod_row, rw, tm):
    Bx, Tx, D = x.shape
    row = lambda b, i: (b, i, 0)
    outs = [(D, BF16), (LANE, F32)]
    return pl.pallas_call(
        _router_kernel,
        grid=(Bx, Tx // tm),
        in_specs=[pl.BlockSpec((1, tm, D), row),
                  pl.BlockSpec((1, D), lambda b, i: (0, 0)),
                  pl.BlockSpec((1, 6, D), lambda b, i: (mod_row(b), 0, 0)),
                  pl.BlockSpec((D, LANE), lambda b, i: (0, 0))],
        out_specs=[pl.BlockSpec((1, tm, w), row) for w, _ in outs],
        out_shape=[jax.ShapeDtypeStruct((Bx, Tx, w), dt) for w, dt in outs],
        compiler_params=_params("parallel", "parallel"),
        name="router",
    )(x, norm_g.reshape(1, D), mods, rw)


def _select_kernel(aff_ref, pos_ref, post_ref, *, cap):
    aff = aff_ref[0]
    T = aff.shape[0]
    real = lax.broadcasted_iota(jnp.int32, aff.shape, 1) < N_EXPERTS
    bits = pltpu.bitcast(aff, jnp.int32)

    def search(i, thr):
        cand = thr | (jnp.int32(1) << (30 - i))
        cnt = jnp.sum(jnp.where(bits >= cand, 1.0, 0.0), axis=0, keepdims=True)
        return jnp.where(cnt >= cap, cand, thr)

    thr = lax.fori_loop(0, 31, search, jnp.zeros((1, LANE), jnp.int32))
    gt = jnp.where(bits > thr, 1.0, 0.0)
    eq = jnp.where(bits == thr, 1.0, 0.0)
    need = cap - jnp.sum(gt, axis=0, keepdims=True)
    blk = min(T, 256)
    sel = gt + eq * jnp.where(_prefix_count(eq, blk) < need, 1.0, 0.0)
    pos = _prefix_count(sel, blk)
    pos = jnp.where(jnp.logical_and(sel > 0.0, real), pos, -1.0)
    pos_ref[0] = pos
    post_ref[0] = jnp.transpose(pos)[:N_EXPERTS]


def _select(aff, cap):
    Bx, Tx, _ = aff.shape
    blk = pl.BlockSpec((1, Tx, LANE), lambda b: (b, 0, 0))
    return pl.pallas_call(
        functools.partial(_select_kernel, cap=cap),
        grid=(Bx,),
        in_specs=[blk],
        out_specs=[blk, pl.BlockSpec((1, N_EXPERTS, Tx), lambda b: (b, 0, 0))],
        out_shape=[jax.ShapeDtypeStruct((Bx, Tx, LANE), F32),
                   jax.ShapeDtypeStruct((Bx, N_EXPERTS, Tx), F32)],
        compiler_params=_params("parallel"),
        name="moe_select",
    )(aff)


DISPATCH_GROUP = 4


def _dispatch_kernel(h_ref, post_ref, xs_ref, *, cap):
    g = pl.program_id(1)
    T = h_ref.shape[1]
    slot = lax.broadcasted_iota(jnp.int32, (cap, T), 0).astype(F32)
    for j in range(DISPATCH_GROUP):
        row = post_ref[0, pl.ds(g * DISPATCH_GROUP + j, 1), :]
        onehot = jnp.where(row == slot, 1.0, 0.0).astype(BF16)
        xs_ref[j] = _dot(onehot, h_ref[0]).astype(BF16)


def _dispatch(h, post, cap):
    Bx, Tx, D = h.shape
    return pl.pallas_call(
        functools.partial(_dispatch_kernel, cap=cap),
        grid=(Bx, N_EXPERTS // DISPATCH_GROUP),
        in_specs=[pl.BlockSpec((1, Tx, D), lambda b, g: (b, 0, 0)),
                  pl.BlockSpec((1, N_EXPERTS, Tx), lambda b, g: (b, 0, 0))],
        out_specs=pl.BlockSpec((DISPATCH_GROUP, cap, D), lambda b, g: (g, b, 0)),
        out_shape=jax.ShapeDtypeStruct((N_EXPERTS, Bx * cap, D), BF16),
        compiler_params=_params("parallel", "parallel"),
        name="moe_dispatch",
    )(h, post)


def _expert_kernel(x_ref, wg_ref, wu_ref, wd_ref, y_ref):
    x = x_ref[0]
    ff = wg_ref.shape[2]
    fc = 512
    acc = None
    for f in range(ff // fc):
        a = _dot(x, wg_ref[0, :, f * fc:(f + 1) * fc])
        u = _dot(x, wu_ref[0, :, f * fc:(f + 1) * fc])
        hm = (a * jax.nn.sigmoid(a) * u).astype(BF16)
        c = _dot(hm, wd_ref[0, f * fc:(f + 1) * fc, :])
        acc = c if acc is None else acc + c
    y_ref[0] = acc.astype(BF16)


def _experts(xs, wg, wu, wd, tm):
    E, M, D = xs.shape
    FF = wg.shape[2]
    return pl.pallas_call(
        _expert_kernel,
        grid=(E, M // tm),
        in_specs=[pl.BlockSpec((1, tm, D), lambda e, i: (e, i, 0)),
                  pl.BlockSpec((1, D, FF), lambda e, i: (e, 0, 0)),
                  pl.BlockSpec((1, D, FF), lambda e, i: (e, 0, 0)),
                  pl.BlockSpec((1, FF, D), lambda e, i: (e, 0, 0))],
        out_specs=pl.BlockSpec((1, tm, D), lambda e, i: (e, i, 0)),
        out_shape=jax.ShapeDtypeStruct((E, M, D), BF16),
        compiler_params=_params("parallel", "parallel"),
        name="moe_experts",
    )(xs, wg, wu, wd)


def _combine_kernel(y_ref, pos_ref, aff_ref, x_ref, mod_ref, o_ref, *, cap, group):
    width = group * cap
    pos = pos_ref[0].astype(BF16)
    gate = aff_ref[0].astype(BF16)
    src = lax.broadcasted_iota(jnp.int32, (LANE, width), 0)
    dst = lax.broadcasted_iota(jnp.int32, (LANE, width), 1)
    slot = (lax.broadcasted_iota(jnp.int32, (1, width), 1) % cap).astype(F32)
    out = None
    for g in range(N_EXPERTS // group):
        spread = jnp.where(dst // cap + g * group == src, 1.0, 0.0).astype(BF16)
        hit = _dot(pos, spread) == slot
        onehot = jnp.where(hit, _dot(gate, spread), 0.0).astype(BF16)
        c = _dot(onehot, y_ref[g * group:(g + 1) * group].reshape(width, y_ref.shape[2]))
        out = c if out is None else out + c
    o_ref[0] = x_ref[0] + mod_ref[0, 5:6] * out


def _combine(y, pos, aff, x, mods, mod_row, cap, tm):
    Bx, Tx, D = x.shape
    row = lambda b, i: (b, i, 0)
    group = max(4, LANE // cap)
    return pl.pallas_call(
        functools.partial(_combine_kernel, cap=cap, group=group),
        grid=(Bx, Tx // tm),
        in_specs=[pl.BlockSpec((N_EXPERTS, cap, D), lambda b, i: (0, b, 0)),
                  pl.BlockSpec((1, tm, LANE), row),
                  pl.BlockSpec((1, tm, LANE), row),
                  pl.BlockSpec((1, tm, D), row),
                  pl.BlockSpec((1, 6, D), lambda b, i: (mod_row(b), 0, 0))],
        out_specs=pl.BlockSpec((1, tm, D), row),
        out_shape=jax.ShapeDtypeStruct((Bx, Tx, D), F32),
        compiler_params=_params("parallel", "parallel"),
        name="moe_combine",
    )(y, pos, aff, x, mods)


def _moe(x, norm_g, mods, mod_row, rw, wg, wu, wd, tm):
    Bx, Tx, D = x.shape
    cap = CAPACITY_FACTOR * Tx // N_EXPERTS
    h, aff = _router(x, norm_g, mods, mod_row, rw, tm)
    pos, post = _select(aff, cap)
    xs = _dispatch(h, post, cap)
    y = _experts(xs, wg, wu, wd, min(Bx * cap, 512))
    return _combine(y, pos, aff, x, mods, mod_row, cap, tm)


def _final_norm_kernel(x_ref, g_ref, o_ref):
    o_ref[0] = _rms(x_ref[0]) * g_ref[...]


def _final_norm(x, g, tm):
    Bx, Tx, D = x.shape
    return pl.pallas_call(
        _final_norm_kernel,
        grid=(Bx, Tx // tm),
        in_specs=[pl.BlockSpec((1, tm, D), lambda b, i: (b, i, 0)),
                  pl.BlockSpec((1, D), lambda b, i: (0, 0))],
        out_specs=pl.BlockSpec((1, tm, D), lambda b, i: (b, i, 0)),
        out_shape=jax.ShapeDtypeStruct((Bx, Tx, D), F32),
        compiler_params=_params("parallel", "parallel"),
        name="final_norm",
    )(x, g.reshape(1, D))


def _rope_tables(T, rot_dim, lane_lo, period, scale):
    rows = T // GRID_W
    row = jnp.repeat(jnp.arange(rows, dtype=F32), GRID_W)
    colp = jnp.tile(jnp.arange(GRID_W, dtype=F32), rows)
    n_freq = rot_dim // 4
    inv_freq = ROPE_BASE ** (-jnp.arange(n_freq, dtype=F32) / n_freq)
    ang = jnp.concatenate([row[:, None] * inv_freq, colp[:, None] * inv_freq], axis=-1)
    cos_h, sin_h = jnp.cos(ang), jnp.sin(ang)
    half = rot_dim // 2
    cos_g = jnp.ones((T, period), F32)
    sa_g = jnp.zeros((T, period), F32)
    sb_g = jnp.zeros((T, period), F32)
    cos_g = cos_g.at[:, lane_lo:lane_lo + rot_dim].set(jnp.concatenate([cos_h, cos_h], axis=-1))
    sa_g = sa_g.at[:, lane_lo:lane_lo + half].set(-sin_h)
    sb_g = sb_g.at[:, lane_lo + half:lane_lo + rot_dim].set(sin_h)
    rep = LANE // period
    return tuple(jnp.tile(t, (1, rep)) * scale for t in (cos_g, sa_g, sb_g))


def _identity_tables(T, scale):
    return (jnp.full((T, LANE), scale, F32), jnp.zeros((T, LANE), F32), jnp.zeros((T, LANE), F32))


def _prep_w_in(w):
    D = w.shape[0]
    o_diff = HGRN_W
    o_cq = o_diff + 1536
    o_ckv = o_cq + C_Q_LORA
    o_kr = o_ckv + C_KV_LORA
    o_gate = o_kr + C_ROPE
    pad = jnp.zeros((D, MLA_KV_IN - C_KV_LORA - C_ROPE), w.dtype)
    return jnp.concatenate([w[:, :o_cq], w[:, o_ckv:o_kr], w[:, o_kr:o_gate], pad, w[:, o_cq:o_ckv],
                            w[:, o_gate:]], axis=1).astype(BF16)


def _prep_mla_w(w_uq, w_ukv):
    wq = w_uq.reshape(C_Q_LORA, C_HEADS, C_NOPE + C_ROPE)
    wq = jnp.pad(wq, ((0, 0), (0, 0), (0, LANE - C_NOPE - C_ROPE))).reshape(C_Q_LORA, C_HEADS * LANE)
    wkv = w_ukv.reshape(C_KV_LORA, C_HEADS, C_NOPE + C_V)
    wk = jnp.pad(wkv[:, :, :C_NOPE], ((0, 0), (0, 0), (0, LANE - C_NOPE)))
    place = jnp.zeros((C_ROPE, C_HEADS, LANE), F32).at[:, :, C_NOPE:C_NOPE + C_ROPE].set(
        jnp.broadcast_to(jnp.eye(C_ROPE, dtype=F32)[:, None, :], (C_ROPE, C_HEADS, C_ROPE)))
    wk = jnp.concatenate([wk, place, jnp.zeros((MLA_KV_IN - C_KV_LORA - C_ROPE, C_HEADS, LANE), F32)], axis=0)
    wv = jnp.pad(wkv[:, :, C_NOPE:].reshape(C_KV_LORA, C_HEADS * C_V), ((0, MLA_KV_IN - C_KV_LORA), (0, 0)))
    wkv_comb = jnp.concatenate([wk.reshape(MLA_KV_IN, C_HEADS * LANE), wv], axis=1)
    return wq.astype(BF16), wkv_comb.astype(BF16)


def kernel(x, c, ctx, c_ctx, ada_w, ada_b, norm_mix_g, norm_ffn_g, w_in, hgrn_lb, hgrn_norm_g, diff_lambda,
           diff_norm_g, mla_q_norm_g, mla_w_uq, mla_kv_norm_g, mla_w_ukv, w_branch, w_out, router_w,
           exp_w_gate, exp_w_up, exp_w_down, final_norm_g):
    B, T, D = x.shape
    Tc = ctx.shape[1]
    depth = ada_w.shape[0]
    tm_l = min(T, 512)
    tm_c = min(Tc, 256)
    tq = min(Tc, 256)

    rows = ((B + 1 + SUBLANE - 1) // SUBLANE) * SUBLANE
    vecs = jnp.concatenate([c, c_ctx[None], jnp.zeros((rows - B - 1, D), F32)], axis=0)
    mods_all = _modulation(vecs, ada_w, ada_b).reshape(depth, rows, 6, D)
    lat_row = lambda b: b
    ctx_row = lambda b: B

    p = jax.nn.softmax(hgrn_lb.astype(F32), axis=0)
    lower_bounds = jnp.clip(jnp.cumsum(p, axis=0) - p[0], 0.0, LB_MAX)

    d_scale = B_HD ** -0.5 * math.log2(math.e)
    c_scale = (C_NOPE + C_ROPE) ** -0.5 * math.log2(math.e)
    dq_l = _rope_tables(T, B_HD, 0, B_HD, d_scale)
    dk_l = _rope_tables(T, B_HD, 0, B_HD, 1.0)
    dq_c, dk_c = _identity_tables(Tc, d_scale), _identity_tables(Tc, 1.0)
    mq_l = _rope_tables(T, C_ROPE, C_NOPE, LANE, c_scale)
    mk_l = _rope_tables(T, C_ROPE, C_NOPE, LANE, 1.0)
    mq_c, mk_c = _identity_tables(Tc, c_scale), _identity_tables(Tc, 1.0)

    xl, xc = x, ctx
    for layer in range(depth):
        with_ctx = layer < depth - 1
        mods = mods_all[layer]
        w_perm = _prep_w_in(w_in[layer])
        wq, wkv = _prep_mla_w(mla_w_uq[layer], mla_w_ukv[layer])
        wb = w_branch[layer].astype(BF16)
        wo = w_out[layer].astype(BF16)
        rw = jnp.pad(router_w[layer], ((0, 0), (0, LANE - N_EXPERTS))).astype(BF16)
        wg = exp_w_gate[layer].astype(BF16)
        wu = exp_w_up[layer].astype(BF16)
        wd = exp_w_down[layer].astype(BF16)
        lam_init = 0.8 - 0.6 * math.exp(-0.3 * layer)
        gq = mla_q_norm_g[layer].reshape(1, -1)
        gkv = jnp.pad(mla_kv_norm_g[layer], (0, MLA_KV_IN - C_KV_LORA)).reshape(1, -1)

        hg_l, dqk_l, dv_l, ml_l, gt_l = _layer_in(xl, norm_mix_g[layer], mods, lat_row, w_perm, dq_l, dk_l, tm_l)
        hg_c, dqk_c, dv_c, ml_c, gt_c = _layer_in(xc, norm_mix_g[layer], mods, ctx_row, w_perm, dq_c, dk_c, tm_c)

        a_l, a_c = _hgrn(hg_l, hg_c, lower_bounds[layer], hgrn_norm_g[layer], with_ctx)

        b_l = _diff_attn(dqk_l, [(dqk_c, dv_c), (dqk_l, dv_l)], diff_lambda[layer], diff_norm_g[layer], lam_init, tm_l)
        mq_lat, mk_lat, mv_lat = _mla_proj(ml_l, gq, gkv, wq, wkv, mq_l, mk_l, tm_l)
        mq_ctx, mk_ctx, mv_ctx = _mla_proj(ml_c, gq, gkv, wq, wkv, mq_c, mk_c, tm_c)
        m_l = _mla_attn(mq_lat, [(mk_ctx, mv_ctx), (mk_lat, mv_lat)], tm_l)

        xl = _merge(a_l, b_l, m_l, gt_l, wb, wo, xl, mods, lat_row, tm_l)
        xl = _moe(xl, norm_ffn_g[layer], mods, lat_row, rw, wg, wu, wd, tm_l)
        if with_ctx:
            b_c = _diff_attn(dqk_c, [(dqk_c, dv_c)], diff_lambda[layer], diff_norm_g[layer], lam_init, tq)
            m_c = _mla_attn(mq_ctx, [(mk_ctx, mv_ctx)], tq)
            xc = _merge(a_c, b_c, m_c, gt_c, wb, wo, xc, mods, ctx_row, tm_c)
            xc = _moe(xc, norm_ffn_g[layer], mods, ctx_row, rw, wg, wu, wd, tm_c)
    return _final_norm(xl, final_norm_g, tm_l)
```

```python
import functools
import math

import jax
import jax.numpy as jnp
from jax import lax
from jax.experimental import pallas as pl
from jax.experimental.pallas import tpu as pltpu

F32 = jnp.float32
BF16 = jnp.bfloat16

EPS = 1e-6
ROPE_BASE = 10000.0
GRID_W = 64
F_MIN = 1e-30
LB_MAX = 1.0 - 1e-6

A_HEADS, A_DK = 4, 128
B_HEADS, B_HD = 4, 64
C_HEADS, C_NOPE, C_ROPE, C_V = 8, 64, 32, 64
C_Q_LORA, C_KV_LORA = 384, 256
N_BRANCH, BRANCH_W = 3, 512
N_EXPERTS = 16
CAPACITY_FACTOR = 2

LANE = 128
SUBLANE = 8
VMEM_LIMIT = 56 * 1024 * 1024

HGRN_W = 5 * 512
MLA_W = 768
MLA_KV_IN = 384
GATE_W = N_BRANCH * 1024
HGRN_CHUNK = 128


def _params(*sem):
    return pltpu.CompilerParams(dimension_semantics=sem, vmem_limit_bytes=VMEM_LIMIT)


def _resident(shape, index_map):
    return pl.BlockSpec(shape, index_map, pipeline_mode=pl.Buffered(1))


def _rms(x, eps=EPS):
    return x * lax.rsqrt(jnp.mean(x * x, axis=-1, keepdims=True) + eps)


def _dot(a, b):
    return jnp.dot(a, b, preferred_element_type=F32)


def _dot_nt(a, b):
    return lax.dot_general(a, b, (((1,), (1,)), ((), ())), preferred_element_type=F32)


def _dot_tn(a, b):
    return lax.dot_general(a, b, (((0,), (0,)), ((), ())), preferred_element_type=F32)


def _mods_kernel(v_ref, w_ref, b_ref, o_ref):
    v = v_ref[...]
    s = (v * jax.nn.sigmoid(v)).astype(BF16)
    o_ref[0] = _dot(s, w_ref[0].astype(BF16)) + b_ref[0]


def _modulation(vecs, ada_w, ada_b):
    L, D, N = ada_w.shape
    R = vecs.shape[0]
    tn = 1536
    return pl.pallas_call(
        _mods_kernel,
        grid=(L, N // tn),
        in_specs=[pl.BlockSpec((R, D), lambda l, j: (0, 0)),
                  pl.BlockSpec((1, D, tn), lambda l, j: (l, 0, j)),
                  pl.BlockSpec((1, 1, tn), lambda l, j: (l, 0, j))],
        out_specs=pl.BlockSpec((1, R, tn), lambda l, j: (l, 0, j)),
        out_shape=jax.ShapeDtypeStruct((L, R, N), F32),
        compiler_params=_params("parallel", "parallel"),
        name="adaln_mods",
    )(vecs, ada_w, ada_b.reshape(L, 1, N))


def _rope_apply(x, cos, sin_a, sin_b, half):
    return (x * cos + pltpu.roll(x, LANE - half, axis=1) * sin_a + pltpu.roll(x, half, axis=1) * sin_b)


def _layer_in_kernel(x_ref, g_ref, mod_ref, w_ref, cq_ref, saq_ref, sbq_ref, ck_ref, sak_ref, sbk_ref,
                     hg_ref, dqk_ref, dv_ref, mla_ref, gate_ref):
    x = x_ref[0]
    mod = mod_ref[0]
    h = (_rms(x) * g_ref[...]) * (1.0 + mod[1:2]) + mod[0:1]
    h = h.astype(BF16)
    c0 = 0
    for j in range(HGRN_W // 512):
        hg_ref[0, :, j * 512:(j + 1) * 512] = _dot(h, w_ref[:, c0:c0 + 512])
        c0 += 512
    for j in range(2):
        acc = _dot(h, w_ref[:, c0:c0 + 512])
        cos, sa, sb = (cq_ref, saq_ref, sbq_ref) if j == 0 else (ck_ref, sak_ref, sbk_ref)
        for g in range(4):
            xg = acc[:, g * LANE:(g + 1) * LANE]
            dqk_ref[0, :, j * 512 + g * LANE:j * 512 + (g + 1) * LANE] = _rope_apply(
                xg, cos[...], sa[...], sb[...], B_HD // 2).astype(BF16)
        c0 += 512
    dv_ref[0] = jnp.transpose(_dot(h, w_ref[:, c0:c0 + 512])).astype(BF16)
    c0 += 512
    mla_ref[0] = _dot(h, w_ref[:, c0:c0 + MLA_W])
    c0 += MLA_W
    for j in range(GATE_W // 512):
        gate_ref[0, :, j * 512:(j + 1) * 512] = jax.nn.sigmoid(_dot(h, w_ref[:, c0:c0 + 512])).astype(BF16)
        c0 += 512


def _out_spec(tm, width, transposed):
    if transposed:
        return pl.BlockSpec((1, width, tm), lambda b, i: (b, 0, i))
    return pl.BlockSpec((1, tm, width), lambda b, i: (b, i, 0))


def _out_shape(bx, tx, width, dtype, transposed):
    return jax.ShapeDtypeStruct((bx, width, tx) if transposed else (bx, tx, width), dtype)


def _layer_in(x, norm_g, mods, mod_row, w_perm, tabs_q, tabs_k, tm):
    Bx, Tx, D = x.shape
    NW = w_perm.shape[1]
    row = lambda b, i: (b, i, 0)
    tab = pl.BlockSpec((tm, LANE), lambda b, i: (i, 0))
    outs = [(HGRN_W, F32, False), (1024, BF16, False), (512, BF16, True), (MLA_W, F32, False),
            (GATE_W, BF16, False)]
    return pl.pallas_call(
        _layer_in_kernel,
        grid=(Bx, Tx // tm),
        in_specs=[pl.BlockSpec((1, tm, D), row),
                  pl.BlockSpec((1, D), lambda b, i: (0, 0)),
                  pl.BlockSpec((1, 6, D), lambda b, i: (mod_row(b), 0, 0)),
                  _resident((D, NW), lambda b, i: (0, 0)),
                  tab, tab, tab, tab, tab, tab],
        out_specs=[_out_spec(tm, w, t) for w, _, t in outs],
        out_shape=[_out_shape(Bx, Tx, w, dt, t) for w, dt, t in outs],
        compiler_params=_params("parallel", "parallel"),
        name="layer_in",
    )(x, norm_g.reshape(1, D), mods, w_perm, *tabs_q, *tabs_k)


def _cumsum_rows(x, reverse):
    n = x.shape[0]
    s = 1
    while s < n:
        if s < SUBLANE:
            row = lax.broadcasted_iota(jnp.int32, x.shape, 0)
            if reverse:
                sh = jnp.where(row < n - s, pltpu.roll(x, n - s, axis=0), 0.0)
            else:
                sh = jnp.where(row >= s, pltpu.roll(x, s, axis=0), 0.0)
        else:
            z = jnp.zeros((s, x.shape[1]), x.dtype)
            sh = jnp.concatenate([x[s:], z], axis=0) if reverse else jnp.concatenate([z, x[:n - s]], axis=0)
        x = x + sh
        s *= 2
    return x


def _bcast_rows(load_row, rows, reps):
    parts = []
    for r in rows:
        parts += [load_row(r)] * reps
    return jnp.concatenate(parts, axis=0)


def _hgrn_chunk(q, k_in, v, v_row, lf, st, b_scr, k_scr, mask_ref, reverse):
    C = q.shape[0]
    b = _cumsum_rows(lf, reverse)
    b_scr[...] = b
    k_scr[...] = k_in
    b_row = lambda r: jnp.broadcast_to(b_scr[pl.ds(r, 1), :], (SUBLANE, LANE))
    k_row = lambda r: jnp.broadcast_to(k_scr[pl.ds(r, 1), :], (SUBLANE, LANE))
    b_last = b[0:1] if reverse else b[C - 1:C]
    qd = (q * jnp.exp2(b)).astype(BF16)
    o = _dot_nt(qd, st.astype(BF16))
    kd = (k_in * jnp.exp2(b_last - b)).astype(BF16)
    vb = v.astype(BF16)
    st_new = st * jnp.exp2(b_last) + _dot_tn(vb, kd)

    scores = None
    for li, m in enumerate(_hgrn_levels(C)):
        mid = m if reverse else m - 1
        anchor = _bcast_rows(b_row, [i * 2 * m + mid for i in range(C // (2 * m))], 2 * m // SUBLANE)
        e = jnp.exp2(-jnp.abs(b - anchor))
        sc = _dot_nt((q * e).astype(BF16), (k_in * e).astype(BF16)) * mask_ref[li]
        scores = sc if scores is None else scores + sc
    o = o + _dot(scores.astype(BF16), vb)

    t8 = lax.broadcasted_iota(jnp.int32, (C, 1), 0) % SUBLANE
    for j in range(SUBLANE):
        rows = [SUBLANE * i + j for i in range(C // SUBLANE)]
        w = q * jnp.exp2(b - _bcast_rows(b_row, rows, 1)) * _bcast_rows(k_row, rows, 1)
        valid = (t8 <= j) if reverse else (t8 >= j)
        sc = jnp.sum(jnp.where(valid, w, 0.0), axis=-1, keepdims=True)
        o = o + sc * _bcast_rows(v_row, rows, 1)
    return o, st_new


def _hgrn_levels(C):
    out = []
    m = C // 2
    while m >= SUBLANE:
        out.append(m)
        m //= 2
    return out


def _hgrn_masks(C, reverse):
    row = lax.broadcasted_iota(jnp.int32, (C, C), 0)
    col = lax.broadcasted_iota(jnp.int32, (C, C), 1)
    masks = []
    for m in _hgrn_levels(C):
        same = (row // (2 * m)) == (col // (2 * m))
        t_second = (row % (2 * m)) >= m
        s_second = (col % (2 * m)) >= m
        if reverse:
            ok = same & jnp.logical_not(t_second) & s_second
        else:
            ok = same & t_second & jnp.logical_not(s_second)
        masks.append(jnp.where(ok, 1.0, 0.0))
    return masks


def _hgrn_kernel(*refs, with_ctx_out):
    (ql, il, gl, ffl, fbl, qc, ic, gc, ffc, fbc, lb_ref, ng_ref) = refs[:12]
    if with_ctx_out:
        ol_ref, oc_ref, accl, accc, b_scr, k_scr, mask_scr = refs[12:]
    else:
        ol_ref, accl, accc, b_scr, k_scr, mask_scr = refs[12:]
        oc_ref = None
    C = HGRN_CHUNK
    nl = ql.shape[1] // C
    nc = qc.shape[1] // C

    accl[...] = jnp.zeros_like(accl)
    accc[...] = jnp.zeros_like(accc)
    for d in range(2):
        for li, mk in enumerate(_hgrn_masks(C, d == 1)):
            mask_scr[d, li] = mk

    def make_step(qr, ir, frs, acc, n):
        def step(s, sts):
            new = []
            for d in range(2):
                reverse = d == 1
                lbd = lb_ref[d:d + 1, :]
                one_m = 1.0 - lbd
                c = (n - 1 - s) if reverse else s
                r0 = pl.multiple_of(c * C, C)
                sg = jax.nn.sigmoid(frs[d][0, pl.ds(r0, C), :])
                kk = one_m * (1.0 - sg)
                lf = jnp.log2(jnp.maximum(lbd + one_m * sg, F_MIN))
                v_row = lambda r, r0=r0: jnp.broadcast_to(ir[0, pl.ds(r0 + r, 1), :], (SUBLANE, LANE))
                o, st = _hgrn_chunk(qr[0, pl.ds(r0, C), :], kk, ir[0, pl.ds(r0, C), :], v_row, lf, sts[d],
                                    b_scr.at[d], k_scr.at[d], mask_scr.at[d], reverse)
                acc[pl.ds(r0, C), :] += o
                new.append(st)
            return tuple(new)
        return step

    zero = jnp.zeros((LANE, LANE), F32)
    unroll = lambda n: 2 if n % 2 == 0 else 1
    sts = lax.fori_loop(0, nc, make_step(qc, ic, (ffc, fbc), accc, nc), (zero, zero), unroll=unroll(nc))
    lax.fori_loop(0, nl, make_step(ql, il, (ffl, fbl), accl, nl), sts, unroll=unroll(nl))

    def readout(acc, g_ref, o_ref):
        o = _rms(acc[...]) * ng_ref[...]
        g = g_ref[0]
        o_ref[0] = (o * (g * jax.nn.sigmoid(g))).astype(BF16)

    readout(accl, gl, ol_ref)
    if with_ctx_out:
        readout(accc, gc, oc_ref)


def _hgrn(hg_l, hg_c, lb, norm_g, with_ctx_out):
    B, T, _ = hg_l.shape
    Tc = hg_c.shape[1]
    nh = A_HEADS

    def col(n, t):
        return [pl.BlockSpec((1, t, LANE), (lambda b, h, j=j: (b, 0, j * nh + h))) for j in range(n)]

    in_specs = col(5, T) + col(5, Tc) + [pl.BlockSpec((2, LANE), lambda b, h: (0, h)),
                                         pl.BlockSpec((1, LANE), lambda b, h: (0, h))]
    out_specs = [pl.BlockSpec((1, T, LANE), lambda b, h: (b, 0, h))]
    out_shape = [jax.ShapeDtypeStruct((B, T, nh * LANE), BF16)]
    if with_ctx_out:
        out_specs.append(pl.BlockSpec((1, Tc, LANE), lambda b, h: (b, 0, h)))
        out_shape.append(jax.ShapeDtypeStruct((B, Tc, nh * LANE), BF16))
    res = pl.pallas_call(
        functools.partial(_hgrn_kernel, with_ctx_out=with_ctx_out),
        grid=(B, nh),
        in_specs=in_specs,
        out_specs=out_specs,
        out_shape=out_shape,
        scratch_shapes=[pltpu.VMEM((T, LANE), F32), pltpu.VMEM((Tc, LANE), F32),
                        pltpu.VMEM((2, HGRN_CHUNK, LANE), F32), pltpu.VMEM((2, HGRN_CHUNK, LANE), F32),
                        pltpu.VMEM((2, len(_hgrn_levels(HGRN_CHUNK)), HGRN_CHUNK, HGRN_CHUNK), F32)],
        compiler_params=_params("parallel", "parallel"),
        name="hgrn2",
    )(*([hg_l] * 5), *([hg_c] * 5), lb, norm_g.reshape(1, -1))
    return (res[0], res[1]) if with_ctx_out else (res[0], None)


ATTN_KEY_CHUNK = 1024


def _key_chunks(kv):
    off = 0
    for k_ref, v_ref in kv:
        tk = k_ref.shape[1]
        kc = min(tk, ATTN_KEY_CHUNK)
        for c in range(tk // kc):
            yield off, kc, k_ref, v_ref, c
            off += kc


def _score_phase(q, kv, kcols, s_scr):
    mx = None
    for off, kc, k_ref, _, c in _key_chunks(kv):
        s = _dot_nt(k_ref[0, c * kc:(c + 1) * kc, kcols], q)
        s_scr[off:off + kc, :] = s
        cm = s.max(axis=0, keepdims=True)
        mx = cm if mx is None else jnp.maximum(mx, cm)
        yield None
    yield mx


def _pv_phase(m, kv, vrows, s_scr):
    lv = acc = None
    for off, kc, _, v_ref, c in _key_chunks(kv):
        p = jnp.exp2(s_scr[off:off + kc, :] - m)
        ps = p.sum(axis=0, keepdims=True)
        lv = ps if lv is None else lv + ps
        pv = _dot(v_ref[0, vrows, c * kc:(c + 1) * kc], p.astype(BF16))
        acc = pv if acc is None else acc + pv
        yield None
    yield acc, lv


def _interleave(*gens):
    last = [None] * len(gens)
    live = list(range(len(gens)))
    while live:
        for i in list(live):
            try:
                v = next(gens[i])
                if v is not None:
                    last[i] = v
            except StopIteration:
                live.remove(i)
    return last


ATTN_SUB_Q = 512


def _softmax_pv_items(items, kv, scrs):
    out = []
    m_prev = None
    for i, (q, kc, vr) in enumerate(items):
        score = _score_phase(q, kv, kc, scrs[i % 2])
        if i == 0:
            m_prev, = _interleave(score)
        else:
            m_prev, res = _interleave(score, _pv_phase(m_prev, kv, items[i - 1][2], scrs[(i - 1) % 2]))
            out.append(res)
    res, = _interleave(_pv_phase(m_prev, kv, items[-1][2], scrs[(len(items) - 1) % 2]))
    out.append(res)
    return out


def _diff_attn_kernel(*refs, n_kv, lam_init):
    q_ref, lam_ref, ng_ref = refs[0], refs[1], refs[2]
    kv = [(refs[3 + 2 * i], refs[4 + 2 * i]) for i in range(n_kv)]
    o_ref, s1_scr, s2_scr = refs[3 + 2 * n_kv:]
    lp = lam_ref[...]
    lam = (jnp.exp(jnp.sum(lp[0:1] * lp[1:2], axis=-1, keepdims=True))
           - jnp.exp(jnp.sum(lp[2:3] * lp[3:4], axis=-1, keepdims=True)) + lam_init)
    sub = s1_scr.shape[1]
    full = slice(None)
    items = []
    for r in range(q_ref.shape[1] // sub):
        q = q_ref[0, r * sub:(r + 1) * sub, :]
        lane = lax.broadcasted_iota(jnp.int32, q.shape, 1)
        zero = jnp.zeros_like(q)
        items += [(jnp.where(lane < B_HD, q, zero), full, full), (jnp.where(lane < B_HD, zero, q), full, full)]
    res = _softmax_pv_items(items, kv, (s1_scr, s2_scr))
    for r in range(q_ref.shape[1] // sub):
        (a1, l1), (a2, l2) = res[2 * r], res[2 * r + 1]
        o = jnp.transpose(a1 * (1.0 / l1) - a2 * (lam / l2))
        o_ref[0, r * sub:(r + 1) * sub, :] = (_rms(o) * ng_ref[...] * (1.0 - lam_init)).astype(BF16)


def _diff_attn(q_src, kvs, lam_params, norm_g, lam_init, tq):
    B, Tq, _ = q_src.shape
    nh = B_HEADS
    in_specs = [pl.BlockSpec((1, tq, LANE), lambda b, h, i: (b, i, h)),
                pl.BlockSpec((4, B_HD), lambda b, h, i: (0, 0)),
                pl.BlockSpec((1, LANE), lambda b, h, i: (0, h))]
    args = [q_src, lam_params, norm_g.reshape(1, -1)]
    n_keys = 0
    for k_arr, v_arr in kvs:
        Tk = k_arr.shape[1]
        n_keys += Tk
        in_specs.append(pl.BlockSpec((1, Tk, LANE), lambda b, h, i: (b, 0, nh + h)))
        in_specs.append(pl.BlockSpec((1, LANE, Tk), lambda b, h, i: (b, h, 0)))
        args += [k_arr, v_arr]
    return pl.pallas_call(
        functools.partial(_diff_attn_kernel, n_kv=len(kvs), lam_init=lam_init),
        grid=(B, nh, Tq // tq),
        in_specs=in_specs,
        out_specs=pl.BlockSpec((1, tq, LANE), lambda b, h, i: (b, i, h)),
        out_shape=jax.ShapeDtypeStruct((B, Tq, nh * LANE), BF16),
        scratch_shapes=[pltpu.VMEM((n_keys, min(tq, ATTN_SUB_Q)), F32)] * 2,
        compiler_params=_params("parallel", "parallel", "parallel"),
        name="diff_attn",
    )(*args)


def _mla_attn_kernel(*refs, n_kv):
    q_ref = refs[0]
    kv = [(refs[1 + 2 * i], refs[2 + 2 * i]) for i in range(n_kv)]
    o_ref, s1_scr, s2_scr = refs[1 + 2 * n_kv:]
    sub = s1_scr.shape[1]
    c0, c1 = slice(0, LANE), slice(LANE, 2 * LANE)
    v0, v1 = slice(0, C_V), slice(C_V, 2 * C_V)
    items = []
    for r in range(q_ref.shape[1] // sub):
        rows = slice(r * sub, (r + 1) * sub)
        items += [(q_ref[0, rows, c0], c0, v0), (q_ref[0, rows, c1], c1, v1)]
    res = _softmax_pv_items(items, kv, (s1_scr, s2_scr))
    for r in range(q_ref.shape[1] // sub):
        (a0, l0), (a1, l1) = res[2 * r], res[2 * r + 1]
        ot = jnp.concatenate([a0 * (1.0 / l0), a1 * (1.0 / l1)], axis=0)
        o_ref[0, r * sub:(r + 1) * sub, :] = jnp.transpose(ot).astype(BF16)


def _mla_attn(q, kvs, tq):
    B, Tq, _ = q.shape
    npair = C_HEADS // 2
    in_specs = [pl.BlockSpec((1, tq, 2 * LANE), lambda b, h, i: (b, i, h))]
    args = [q]
    n_keys = 0
    for k_arr, v_arr in kvs:
        Tk = k_arr.shape[1]
        n_keys += Tk
        in_specs.append(pl.BlockSpec((1, Tk, 2 * LANE), lambda b, h, i: (b, 0, h)))
        in_specs.append(pl.BlockSpec((1, LANE, Tk), lambda b, h, i: (b, h, 0)))
        args += [k_arr, v_arr]
    return pl.pallas_call(
        functools.partial(_mla_attn_kernel, n_kv=len(kvs)),
        grid=(B, npair, Tq // tq),
        in_specs=in_specs,
        out_specs=pl.BlockSpec((1, tq, LANE), lambda b, h, i: (b, i, h)),
        out_shape=jax.ShapeDtypeStruct((B, Tq, C_HEADS * C_V), BF16),
        scratch_shapes=[pltpu.VMEM((n_keys, min(tq, ATTN_SUB_Q)), F32)] * 2,
        compiler_params=_params("parallel", "parallel", "parallel"),
        name="mla_attn",
    )(*args)


def _mla_proj_kernel(x_ref, gq_ref, gkv_ref, wq_ref, wkv_ref, cq_ref, saq_ref, sbq_ref, ck_ref, sak_ref,
                     sbk_ref, q_ref, k_ref, v_ref):
    x = x_ref[0]
    xkv = x[:, :MLA_KV_IN]
    lane = lax.broadcasted_iota(jnp.int32, xkv.shape, 1)
    lat = jnp.where(lane < C_KV_LORA, xkv, 0.0)
    ms = jnp.sum(lat * lat, axis=-1, keepdims=True) * (1.0 / C_KV_LORA)
    hk = jnp.where(lane < C_KV_LORA, lat * lax.rsqrt(ms + EPS) * gkv_ref[...], xkv).astype(BF16)
    hq = (_rms(x[:, MLA_KV_IN:]) * gq_ref[...]).astype(BF16)
    nk = C_HEADS * LANE
    aq = _dot(hq, wq_ref[...])
    akv = _dot(hk, wkv_ref[...])
    for g in range(C_HEADS):
        sl = slice(g * LANE, (g + 1) * LANE)
        q_ref[0, :, sl] = _rope_apply(aq[:, sl], cq_ref[...], saq_ref[...], sbq_ref[...], C_ROPE // 2).astype(BF16)
        k_ref[0, :, sl] = _rope_apply(akv[:, sl], ck_ref[...], sak_ref[...], sbk_ref[...], C_ROPE // 2).astype(BF16)
    v_ref[0] = jnp.transpose(akv[:, nk:]).astype(BF16)


def _mla_proj(mla, gq, gkv, wq, wkv, tabs_q, tabs_k, tm):
    Bx, Tx, _ = mla.shape
    row = lambda b, i: (b, i, 0)
    const = lambda b, i: (0, 0)
    tab = pl.BlockSpec((tm, LANE), lambda b, i: (i, 0))
    nk = C_HEADS * LANE
    outs = [(nk, BF16, False), (nk, BF16, False), (C_HEADS * C_V, BF16, True)]
    return pl.pallas_call(
        _mla_proj_kernel,
        grid=(Bx, Tx // tm),
        in_specs=[pl.BlockSpec((1, tm, MLA_W), row),
                  pl.BlockSpec((1, C_Q_LORA), const),
                  pl.BlockSpec((1, MLA_KV_IN), const),
                  _resident(wq.shape, const),
                  _resident(wkv.shape, const),
                  tab, tab, tab, tab, tab, tab],
        out_specs=[_out_spec(tm, w, t) for w, _, t in outs],
        out_shape=[_out_shape(Bx, Tx, w, dt, t) for w, dt, t in outs],
        compiler_params=_params("parallel", "parallel"),
        name="mla_proj",
    )(mla, gq, gkv, wq, wkv, *tabs_q, *tabs_k)


def _merge_kernel(a_ref, b_ref, c_ref, gate_ref, wb_ref, wo_ref, x_ref, mod_ref, o_ref):
    D = x_ref.shape[2]
    z = None
    for n, br in enumerate((a_ref, b_ref, c_ref)):
        y = _dot(br[0], wb_ref[n])
        t = gate_ref[0, :, n * D:(n + 1) * D].astype(F32) * y
        z = t if z is None else z + t
    out = _dot(z.astype(BF16), wo_ref[...])
    o_ref[0] = x_ref[0] + mod_ref[0, 2:3] * out


def _merge(a, b, c, gates, wb, wo, x, mods, mod_row, tm):
    Bx, Tx, D = x.shape
    row = lambda bb, i: (bb, i, 0)
    br = pl.BlockSpec((1, tm, BRANCH_W), row)
    return pl.pallas_call(
        _merge_kernel,
        grid=(Bx, Tx // tm),
        in_specs=[br, br, br,
                  pl.BlockSpec((1, tm, GATE_W), row),
                  _resident(wb.shape, lambda bb, i: (0, 0, 0)),
                  _resident(wo.shape, lambda bb, i: (0, 0)),
                  pl.BlockSpec((1, tm, D), row),
                  pl.BlockSpec((1, 6, D), lambda bb, i: (mod_row(bb), 0, 0))],
        out_specs=pl.BlockSpec((1, tm, D), row),
        out_shape=jax.ShapeDtypeStruct((Bx, Tx, D), F32),
        compiler_params=_params("parallel", "parallel"),
        name="merge_out",
    )(a, b, c, gates, wb, wo, x, mods)


def _prefix_count(mask, blk):
    T = mask.shape[1]
    r = lax.broadcasted_iota(jnp.int32, (blk, blk), 0)
    c = lax.broadcasted_iota(jnp.int32, (blk, blk), 1)
    tri = jnp.where(r < c, 1.0, 0.0).astype(BF16)
    parts = []
    carry = jnp.zeros((mask.shape[0], 1), F32)
    for i in range(T // blk):
        mb = mask[:, i * blk:(i + 1) * blk]
        parts.append(_dot(mb.astype(BF16), tri) + carry)
        carry = carry + jnp.sum(mb, axis=1, keepdims=True)
    return jnp.concatenate(parts, axis=1) if len(parts) > 1 else parts[0]


def _router_kernel(x_ref, g_ref, mod_ref, rw_ref, h_ref, aff_ref):
    mod = mod_ref[0]
    h = ((_rms(x_ref[0]) * g_ref[...]) * (1.0 + mod[4:5]) + mod[3:4]).astype(BF16)
    h_ref[0] = h
    logits = _dot(h, rw_ref[...])
    lane = lax.broadcasted_iota(jnp.int32, logits.shape, 1)
    lg = jnp.where(lane < N_EXPERTS, logits, -jnp.inf)
    e = jnp.exp(lg - lg.max(axis=-1, keepdims=True))
    aff_ref[0] = e / e.sum(axis=-1, keepdims=True)


def _router(x, norm_g, mods, mod_row, rw, tm):
    Bx, Tx, D = x.shape
    row = lambda b, i: (b, i, 0)
    outs = [(D, BF16), (LANE, F32)]
    return pl.pallas_call(
        _router_kernel,
        grid=(Bx, Tx // tm),
        in_specs=[pl.BlockSpec((1, tm, D), row),
                  pl.BlockSpec((1, D), lambda b, i: (0, 0)),
                  pl.BlockSpec((1, 6, D), lambda b, i: (mod_row(b), 0, 0)),
                  pl.BlockSpec((D, LANE), lambda b, i: (0, 0))],
        out_specs=[pl.BlockSpec((1, tm, w), row) for w, _ in outs],
        out_shape=[jax.ShapeDtypeStruct((Bx, Tx, w), dt) for w, dt in outs],
        compiler_params=_params("parallel", "parallel"),
        name="router",
    )(x, norm_g.reshape(1, D), mods, rw)


def _select_kernel(aff_ref, pos_ref, post_ref, afft_ref, *, cap):
    T = aff_ref.shape[1]
    aff = jnp.transpose(aff_ref[0])[:N_EXPERTS]
    bits = pltpu.bitcast(aff, jnp.int32)

    def search(i, thr):
        cand = thr | (jnp.int32(1) << (30 - i))
        cnt = jnp.sum(jnp.where(bits >= cand, 1.0, 0.0), axis=1, keepdims=True)
        return jnp.where(cnt >= cap, cand, thr)

    thr = lax.fori_loop(0, 31, search, jnp.zeros((N_EXPERTS, 1), jnp.int32))
    gt = jnp.where(bits > thr, 1.0, 0.0)
    eq = jnp.where(bits == thr, 1.0, 0.0)
    need = cap - jnp.sum(gt, axis=1, keepdims=True)
    blk = min(T, 256)
    sel = gt + eq * jnp.where(_prefix_count(eq, blk) < need, 1.0, 0.0)
    post = jnp.where(sel > 0.0, _prefix_count(sel, blk), -1.0)
    post_ref[0] = post
    afft_ref[0] = aff
    pad = jnp.full((LANE - N_EXPERTS, T), -1.0, F32)
    pos_ref[0] = jnp.transpose(jnp.concatenate([post, pad], axis=0))


def _select(aff, cap):
    Bx, Tx, _ = aff.shape
    blk = pl.BlockSpec((1, Tx, LANE), lambda b: (b, 0, 0))
    blk_t = pl.BlockSpec((1, N_EXPERTS, Tx), lambda b: (b, 0, 0))
    return pl.pallas_call(
        functools.partial(_select_kernel, cap=cap),
        grid=(Bx,),
        in_specs=[blk],
        out_specs=[blk, blk_t, blk_t],
        out_shape=[jax.ShapeDtypeStruct((Bx, Tx, LANE), F32),
                   jax.ShapeDtypeStruct((Bx, N_EXPERTS, Tx), F32),
                   jax.ShapeDtypeStruct((Bx, N_EXPERTS, Tx), F32)],
        compiler_params=_params("parallel"),
        name="moe_select",
    )(aff)


DISPATCH_GROUP = 4


def _dispatch_kernel(h_ref, post_ref, afft_ref, xs_ref, gs_ref, *, cap):
    g = pl.program_id(1)
    T = h_ref.shape[1]
    slot = lax.broadcasted_iota(jnp.int32, (cap, T), 0).astype(F32)
    for j in range(DISPATCH_GROUP):
        e = g * DISPATCH_GROUP + j
        hit = post_ref[0, pl.ds(e, 1), :] == slot
        xs_ref[j] = _dot(jnp.where(hit, 1.0, 0.0).astype(BF16), h_ref[0]).astype(BF16)
        gs_ref[j] = jnp.sum(jnp.where(hit, afft_ref[0, pl.ds(e, 1), :], 0.0), axis=1, keepdims=True)


def _dispatch(h, post, afft, cap):
    Bx, Tx, D = h.shape
    tok = pl.BlockSpec((1, N_EXPERTS, Tx), lambda b, g: (b, 0, 0))
    return pl.pallas_call(
        functools.partial(_dispatch_kernel, cap=cap),
        grid=(Bx, N_EXPERTS // DISPATCH_GROUP),
        in_specs=[pl.BlockSpec((1, Tx, D), lambda b, g: (b, 0, 0)), tok, tok],
        out_specs=[pl.BlockSpec((DISPATCH_GROUP, cap, D), lambda b, g: (g, b, 0)),
                   pl.BlockSpec((DISPATCH_GROUP, cap, 1), lambda b, g: (g, b, 0))],
        out_shape=[jax.ShapeDtypeStruct((N_EXPERTS, Bx * cap, D), BF16),
                   jax.ShapeDtypeStruct((N_EXPERTS, Bx * cap, 1), F32)],
        compiler_params=_params("parallel", "parallel"),
        name="moe_dispatch",
    )(h, post, afft)


def _expert_kernel(x_ref, gs_ref, wg_ref, wu_ref, wd_ref, y_ref):
    x = x_ref[0]
    ff = wg_ref.shape[2]
    fc = 512
    acc = None
    for f in range(ff // fc):
        a = _dot(x, wg_ref[0, :, f * fc:(f + 1) * fc])
        u = _dot(x, wu_ref[0, :, f * fc:(f + 1) * fc])
        hm = (a * jax.nn.sigmoid(a) * u).astype(BF16)
        c = _dot(hm, wd_ref[0, f * fc:(f + 1) * fc, :])
        acc = c if acc is None else acc + c
    y_ref[0] = (acc * gs_ref[0]).astype(BF16)


def _experts(xs, gs, wg, wu, wd, tm):
    E, M, D = xs.shape
    FF = wg.shape[2]
    return pl.pallas_call(
        _expert_kernel,
        grid=(E, M // tm),
        in_specs=[pl.BlockSpec((1, tm, D), lambda e, i: (e, i, 0)),
                  pl.BlockSpec((1, tm, 1), lambda e, i: (e, i, 0)),
                  pl.BlockSpec((1, D, FF), lambda e, i: (e, 0, 0)),
                  pl.BlockSpec((1, D, FF), lambda e, i: (e, 0, 0)),
                  pl.BlockSpec((1, FF, D), lambda e, i: (e, 0, 0))],
        out_specs=pl.BlockSpec((1, tm, D), lambda e, i: (e, i, 0)),
        out_shape=jax.ShapeDtypeStruct((E, M, D), BF16),
        compiler_params=_params("parallel", "parallel"),
        name="moe_experts",
    )(xs, gs, wg, wu, wd)


def _combine_kernel(y_ref, pos_ref, x_ref, mod_ref, o_ref, *, cap, group):
    width = group * cap
    pos = pos_ref[0].astype(BF16)
    src = lax.broadcasted_iota(jnp.int32, (LANE, width), 0)
    dst = lax.broadcasted_iota(jnp.int32, (LANE, width), 1)
    slot = (lax.broadcasted_iota(jnp.int32, (1, width), 1) % cap).astype(F32)
    out = None
    for g in range(N_EXPERTS // group):
        spread = jnp.where(dst // cap + g * group == src, 1.0, 0.0).astype(BF16)
        onehot = jnp.where(_dot(pos, spread) == slot, 1.0, 0.0).astype(BF16)
        c = _dot(onehot, y_ref[g * group:(g + 1) * group].reshape(width, y_ref.shape[2]))
        out = c if out is None else out + c
    o_ref[0] = x_ref[0] + mod_ref[0, 5:6] * out


def _combine(y, pos, x, mods, mod_row, cap, tm):
    Bx, Tx, D = x.shape
    row = lambda b, i: (b, i, 0)
    group = max(4, LANE // cap)
    return pl.pallas_call(
        functools.partial(_combine_kernel, cap=cap, group=group),
        grid=(Bx, Tx // tm),
        in_specs=[pl.BlockSpec((N_EXPERTS, cap, D), lambda b, i: (0, b, 0)),
                  pl.BlockSpec((1, tm, LANE), row),
                  pl.BlockSpec((1, tm, D), row),
                  pl.BlockSpec((1, 6, D), lambda b, i: (mod_row(b), 0, 0))],
        out_specs=pl.BlockSpec((1, tm, D), row),
        out_shape=jax.ShapeDtypeStruct((Bx, Tx, D), F32),
        compiler_params=_params("parallel", "parallel"),
        name="moe_combine",
    )(y, pos, x, mods)


def _moe(x, norm_g, mods, mod_row, rw, wg, wu, wd, tm):
    Bx, Tx, D = x.shape
    cap = CAPACITY_FACTOR * Tx // N_EXPERTS
    h, aff = _router(x, norm_g, mods, mod_row, rw, tm)
    pos, post, afft = _select(aff, cap)
    xs, gs = _dispatch(h, post, afft, cap)
    y = _experts(xs, gs, wg, wu, wd, min(Bx * cap, 512))
    return _combine(y, pos, x, mods, mod_row, cap, tm)


def _final_norm_kernel(x_ref, g_ref, o_ref):
    o_ref[0] = _rms(x_ref[0]) * g_ref[...]


def _final_norm(x, g, tm):
    Bx, Tx, D = x.shape
    return pl.pallas_call(
        _final_norm_kernel,
        grid=(Bx, Tx // tm),
        in_specs=[pl.BlockSpec((1, tm, D), lambda b, i: (b, i, 0)),
                  pl.BlockSpec((1, D), lambda b, i: (0, 0))],
        out_specs=pl.BlockSpec((1, tm, D), lambda b, i: (b, i, 0)),
        out_shape=jax.ShapeDtypeStruct((Bx, Tx, D), F32),
        compiler_params=_params("parallel", "parallel"),
        name="final_norm",
    )(x, g.reshape(1, D))


def _rope_tables(T, rot_dim, lane_lo, period, scale):
    rows = T // GRID_W
    row = jnp.repeat(jnp.arange(rows, dtype=F32), GRID_W)
    colp = jnp.tile(jnp.arange(GRID_W, dtype=F32), rows)
    n_freq = rot_dim // 4
    inv_freq = ROPE_BASE ** (-jnp.arange(n_freq, dtype=F32) / n_freq)
    ang = jnp.concatenate([row[:, None] * inv_freq, colp[:, None] * inv_freq], axis=-1)
    cos_h, sin_h = jnp.cos(ang), jnp.sin(ang)
    half = rot_dim // 2
    cos_g = jnp.ones((T, period), F32)
    sa_g = jnp.zeros((T, period), F32)
    sb_g = jnp.zeros((T, period), F32)
    cos_g = cos_g.at[:, lane_lo:lane_lo + rot_dim].set(jnp.concatenate([cos_h, cos_h], axis=-1))
    sa_g = sa_g.at[:, lane_lo:lane_lo + half].set(-sin_h)
    sb_g = sb_g.at[:, lane_lo + half:lane_lo + rot_dim].set(sin_h)
    rep = LANE // period
    return tuple(jnp.tile(t, (1, rep)) * scale for t in (cos_g, sa_g, sb_g))


def _identity_tables(T, scale):
    return (jnp.full((T, LANE), scale, F32), jnp.zeros((T, LANE), F32), jnp.zeros((T, LANE), F32))


def _prep_w_in(w):
    D = w.shape[0]
    o_diff = HGRN_W
    o_cq = o_diff + 1536
    o_ckv = o_cq + C_Q_LORA
    o_kr = o_ckv + C_KV_LORA
    o_gate = o_kr + C_ROPE
    pad = jnp.zeros((D, MLA_KV_IN - C_KV_LORA - C_ROPE), w.dtype)
    return jnp.concatenate([w[:, :o_cq], w[:, o_ckv:o_kr], w[:, o_kr:o_gate], pad, w[:, o_cq:o_ckv],
                            w[:, o_gate:]], axis=1).astype(BF16)


def _prep_mla_w(w_uq, w_ukv):
    wq = w_uq.reshape(C_Q_LORA, C_HEADS, C_NOPE + C_ROPE)
    wq = jnp.pad(wq, ((0, 0), (0, 0), (0, LANE - C_NOPE - C_ROPE))).reshape(C_Q_LORA, C_HEADS * LANE)
    wkv = w_ukv.reshape(C_KV_LORA, C_HEADS, C_NOPE + C_V)
    wk = jnp.pad(wkv[:, :, :C_NOPE], ((0, 0), (0, 0), (0, LANE - C_NOPE)))
    place = jnp.zeros((C_ROPE, C_HEADS, LANE), F32).at[:, :, C_NOPE:C_NOPE + C_ROPE].set(
        jnp.broadcast_to(jnp.eye(C_ROPE, dtype=F32)[:, None, :], (C_ROPE, C_HEADS, C_ROPE)))
    wk = jnp.concatenate([wk, place, jnp.zeros((MLA_KV_IN - C_KV_LORA - C_ROPE, C_HEADS, LANE), F32)], axis=0)
    wv = jnp.pad(wkv[:, :, C_NOPE:].reshape(C_KV_LORA, C_HEADS * C_V), ((0, MLA_KV_IN - C_KV_LORA), (0, 0)))
    wkv_comb = jnp.concatenate([wk.reshape(MLA_KV_IN, C_HEADS * LANE), wv], axis=1)
    return wq.astype(BF16), wkv_comb.astype(BF16)


def kernel(x, c, ctx, c_ctx, ada_w, ada_b, norm_mix_g, norm_ffn_g, w_in, hgrn_lb, hgrn_norm_g, diff_lambda,
           diff_norm_g, mla_q_norm_g, mla_w_uq, mla_kv_norm_g, mla_w_ukv, w_branch, w_out, router_w,
           exp_w_gate, exp_w_up, exp_w_down, final_norm_g):
    B, T, D = x.shape
    Tc = ctx.shape[1]
    depth = ada_w.shape[0]
    tm_l = min(T, 512)
    tm_c = min(Tc, 256)
    tq = min(Tc, 256)

    rows = ((B + 1 + SUBLANE - 1) // SUBLANE) * SUBLANE
    vecs = jnp.concatenate([c, c_ctx[None], jnp.zeros((rows - B - 1, D), F32)], axis=0)
    mods_all = _modulation(vecs, ada_w, ada_b).reshape(depth, rows, 6, D)
    lat_row = lambda b: b
    ctx_row = lambda b: B

    p = jax.nn.softmax(hgrn_lb.astype(F32), axis=0)
    lower_bounds = jnp.clip(jnp.cumsum(p, axis=0) - p[0], 0.0, LB_MAX)

    d_scale = B_HD ** -0.5 * math.log2(math.e)
    c_scale = (C_NOPE + C_ROPE) ** -0.5 * math.log2(math.e)
    dq_l = _rope_tables(T, B_HD, 0, B_HD, d_scale)
    dk_l = _rope_tables(T, B_HD, 0, B_HD, 1.0)
    dq_c, dk_c = _identity_tables(Tc, d_scale), _identity_tables(Tc, 1.0)
    mq_l = _rope_tables(T, C_ROPE, C_NOPE, LANE, c_scale)
    mk_l = _rope_tables(T, C_ROPE, C_NOPE, LANE, 1.0)
    mq_c, mk_c = _identity_tables(Tc, c_scale), _identity_tables(Tc, 1.0)

    xl, xc = x, ctx
    for layer in range(depth):
        with_ctx = layer < depth - 1
        mods = mods_all[layer]
        w_perm = _prep_w_in(w_in[layer])
        wq, wkv = _prep_mla_w(mla_w_uq[layer], mla_w_ukv[layer])
        wb = w_branch[layer].astype(BF16)
        wo = w_out[layer].astype(BF16)
        rw = jnp.pad(router_w[layer], ((0, 0), (0, LANE - N_EXPERTS))).astype(BF16)
        wg = exp_w_gate[layer].astype(BF16)
        wu = exp_w_up[layer].astype(BF16)
        wd = exp_w_down[layer].astype(BF16)
        lam_init = 0.8 - 0.6 * math.exp(-0.3 * layer)
        gq = mla_q_norm_g[layer].reshape(1, -1)
        gkv = jnp.pad(mla_kv_norm_g[layer], (0, MLA_KV_IN - C_KV_LORA)).reshape(1, -1)

        hg_l, dqk_l, dv_l, ml_l, gt_l = _layer_in(xl, norm_mix_g[layer], mods, lat_row, w_perm, dq_l, dk_l, tm_l)
        hg_c, dqk_c, dv_c, ml_c, gt_c = _layer_in(xc, norm_mix_g[layer], mods, ctx_row, w_perm, dq_c, dk_c, tm_c)

        a_l, a_c = _hgrn(hg_l, hg_c, lower_bounds[layer], hgrn_norm_g[layer], with_ctx)

        b_l = _diff_attn(dqk_l, [(dqk_c, dv_c), (dqk_l, dv_l)], diff_lambda[layer], diff_norm_g[layer], lam_init, tm_l)
        mq_lat, mk_lat, mv_lat = _mla_proj(ml_l, gq, gkv, wq, wkv, mq_l, mk_l, tm_l)
        mq_ctx, mk_ctx, mv_ctx = _mla_proj(ml_c, gq, gkv, wq, wkv, mq_c, mk_c, tm_c)
        m_l = _mla_attn(mq_lat, [(mk_ctx, mv_ctx), (mk_lat, mv_lat)], tm_l)

        xl = _merge(a_l, b_l, m_l, gt_l, wb, wo, xl, mods, lat_row, tm_l)
        xl = _moe(xl, norm_ffn_g[layer], mods, lat_row, rw, wg, wu, wd, tm_l)
        if with_ctx:
            b_c = _diff_attn(dqk_c, [(dqk_c, dv_c)], diff_lambda[layer], diff_norm_g[layer], lam_init, tq)
            m_c = _mla_attn(mq_ctx, [(mk_ctx, mv_ctx)], tq)
            xc = _merge(a_c, b_c, m_c, gt_c, wb, wo, xc, mods, ctx_row, tm_c)
            xc = _moe(xc, norm_ffn_g[layer], mods, ctx_row, rw, wg, wu, wd, tm_c)
    return _final_norm(xl, final_norm_g, tm_l)
```

```python
import functools
import math

import jax
import jax.numpy as jnp
from jax import lax
from jax.experimental import pallas as pl
from jax.experimental.pallas import tpu as pltpu

F32 = jnp.float32
BF16 = jnp.bfloat16

EPS = 1e-6
ROPE_BASE = 10000.0
GRID_W = 64
F_MIN = 1e-30
LB_MAX = 1.0 - 1e-6

A_HEADS, A_DK = 4, 128
B_HEADS, B_HD = 4, 64
C_HEADS, C_NOPE, C_ROPE, C_V = 8, 64, 32, 64
C_Q_LORA, C_KV_LORA = 384, 256
N_BRANCH, BRANCH_W = 3, 512
N_EXPERTS = 16
CAPACITY_FACTOR = 2

LANE = 128
SUBLANE = 8
VMEM_LIMIT = 56 * 1024 * 1024

HGRN_W = 5 * 512
MLA_W = 768
MLA_KV_IN = 384
GATE_W = N_BRANCH * 1024
HGRN_CHUNK = 128


def _params(*sem):
    return pltpu.CompilerParams(dimension_semantics=sem, vmem_limit_bytes=VMEM_LIMIT)


def _resident(shape, index_map):
    return pl.BlockSpec(shape, index_map, pipeline_mode=pl.Buffered(1))


def _rms(x, eps=EPS):
    return x * lax.rsqrt(jnp.mean(x * x, axis=-1, keepdims=True) + eps)


def _dot(a, b):
    return jnp.dot(a, b, preferred_element_type=F32)


def _dot_nt(a, b):
    return lax.dot_general(a, b, (((1,), (1,)), ((), ())), preferred_element_type=F32)


def _dot_tn(a, b):
    return lax.dot_general(a, b, (((0,), (0,)), ((), ())), preferred_element_type=F32)


def _mods_kernel(v_ref, w_ref, b_ref, o_ref):
    v = v_ref[...]
    s = (v * jax.nn.sigmoid(v)).astype(BF16)
    o_ref[0] = _dot(s, w_ref[0].astype(BF16)) + b_ref[0]


def _modulation(vecs, ada_w, ada_b):
    L, D, N = ada_w.shape
    R = vecs.shape[0]
    tn = 1536
    return pl.pallas_call(
        _mods_kernel,
        grid=(L, N // tn),
        in_specs=[pl.BlockSpec((R, D), lambda l, j: (0, 0)),
                  pl.BlockSpec((1, D, tn), lambda l, j: (l, 0, j)),
                  pl.BlockSpec((1, 1, tn), lambda l, j: (l, 0, j))],
        out_specs=pl.BlockSpec((1, R, tn), lambda l, j: (l, 0, j)),
        out_shape=jax.ShapeDtypeStruct((L, R, N), F32),
        compiler_params=_params("parallel", "parallel"),
        name="adaln_mods",
    )(vecs, ada_w, ada_b.reshape(L, 1, N))


def _rope_apply(x, cos, sin_a, sin_b, half):
    return (x * cos + pltpu.roll(x, LANE - half, axis=1) * sin_a + pltpu.roll(x, half, axis=1) * sin_b)


def _layer_in_kernel(x_ref, g_ref, mod_ref, w_ref, cq_ref, saq_ref, sbq_ref, ck_ref, sak_ref, sbk_ref,
                     hg_ref, dqk_ref, dv_ref, mla_ref, gate_ref):
    x = x_ref[0]
    mod = mod_ref[0]
    h = (_rms(x) * g_ref[...]) * (1.0 + mod[1:2]) + mod[0:1]
    h = h.astype(BF16)
    c0 = 0
    for j in range(HGRN_W // 512):
        hg_ref[0, :, j * 512:(j + 1) * 512] = _dot(h, w_ref[:, c0:c0 + 512])
        c0 += 512
    for j in range(2):
        acc = _dot(h, w_ref[:, c0:c0 + 512])
        cos, sa, sb = (cq_ref, saq_ref, sbq_ref) if j == 0 else (ck_ref, sak_ref, sbk_ref)
        for g in range(4):
            xg = acc[:, g * LANE:(g + 1) * LANE]
            dqk_ref[0, :, j * 512 + g * LANE:j * 512 + (g + 1) * LANE] = _rope_apply(
                xg, cos[...], sa[...], sb[...], B_HD // 2).astype(BF16)
        c0 += 512
    dv_ref[0] = jnp.transpose(_dot(h, w_ref[:, c0:c0 + 512])).astype(BF16)
    c0 += 512
    mla_ref[0] = _dot(h, w_ref[:, c0:c0 + MLA_W])
    c0 += MLA_W
    for j in range(GATE_W // 512):
        gate_ref[0, :, j * 512:(j + 1) * 512] = jax.nn.sigmoid(_dot(h, w_ref[:, c0:c0 + 512])).astype(BF16)
        c0 += 512


def _out_spec(tm, width, transposed):
    if transposed:
        return pl.BlockSpec((1, width, tm), lambda b, i: (b, 0, i))
    return pl.BlockSpec((1, tm, width), lambda b, i: (b, i, 0))


def _out_shape(bx, tx, width, dtype, transposed):
    return jax.ShapeDtypeStruct((bx, width, tx) if transposed else (bx, tx, width), dtype)


def _layer_in(x, norm_g, mods, mod_row, w_perm, tabs_q, tabs_k, tm):
    Bx, Tx, D = x.shape
    NW = w_perm.shape[1]
    row = lambda b, i: (b, i, 0)
    tab = pl.BlockSpec((tm, LANE), lambda b, i: (i, 0))
    outs = [(HGRN_W, F32, False), (1024, BF16, False), (512, BF16, True), (MLA_W, F32, False),
            (GATE_W, BF16, False)]
    return pl.pallas_call(
        _layer_in_kernel,
        grid=(Bx, Tx // tm),
        in_specs=[pl.BlockSpec((1, tm, D), row),
                  pl.BlockSpec((1, D), lambda b, i: (0, 0)),
                  pl.BlockSpec((1, 6, D), lambda b, i: (mod_row(b), 0, 0)),
                  _resident((D, NW), lambda b, i: (0, 0)),
                  tab, tab, tab, tab, tab, tab],
        out_specs=[_out_spec(tm, w, t) for w, _, t in outs],
        out_shape=[_out_shape(Bx, Tx, w, dt, t) for w, dt, t in outs],
        compiler_params=_params("parallel", "parallel"),
        name="layer_in",
    )(x, norm_g.reshape(1, D), mods, w_perm, *tabs_q, *tabs_k)


def _cumsum_rows(x, reverse):
    n = x.shape[0]
    s = 1
    while s < n:
        if s < SUBLANE:
            row = lax.broadcasted_iota(jnp.int32, x.shape, 0)
            if reverse:
                sh = jnp.where(row < n - s, pltpu.roll(x, n - s, axis=0), 0.0)
            else:
                sh = jnp.where(row >= s, pltpu.roll(x, s, axis=0), 0.0)
        else:
            z = jnp.zeros((s, x.shape[1]), x.dtype)
            sh = jnp.concatenate([x[s:], z], axis=0) if reverse else jnp.concatenate([z, x[:n - s]], axis=0)
        x = x + sh
        s *= 2
    return x


def _bcast_rows(load_row, rows, reps):
    parts = []
    for r in rows:
        parts += [load_row(r)] * reps
    return jnp.concatenate(parts, axis=0)


def _hgrn_chunk(q, k_in, v, v_row, lf, st, b_scr, k_scr, mask_ref, reverse):
    C = q.shape[0]
    b = _cumsum_rows(lf, reverse)
    b_scr[...] = b
    k_scr[...] = k_in
    b_row = lambda r: jnp.broadcast_to(b_scr[pl.ds(r, 1), :], (SUBLANE, LANE))
    k_row = lambda r: jnp.broadcast_to(k_scr[pl.ds(r, 1), :], (SUBLANE, LANE))
    b_last = b[0:1] if reverse else b[C - 1:C]
    qd = (q * jnp.exp2(b)).astype(BF16)
    o = _dot_nt(qd, st.astype(BF16))
    kd = (k_in * jnp.exp2(b_last - b)).astype(BF16)
    vb = v.astype(BF16)
    st_new = st * jnp.exp2(b_last) + _dot_tn(vb, kd)

    scores = None
    for li, m in enumerate(_hgrn_levels(C)):
        mid = m if reverse else m - 1
        anchor = _bcast_rows(b_row, [i * 2 * m + mid for i in range(C // (2 * m))], 2 * m // SUBLANE)
        e = jnp.exp2(-jnp.abs(b - anchor))
        sc = _dot_nt((q * e).astype(BF16), (k_in * e).astype(BF16)) * mask_ref[li]
        scores = sc if scores is None else scores + sc
    o = o + _dot(scores.astype(BF16), vb)

    t8 = lax.broadcasted_iota(jnp.int32, (C, 1), 0) % SUBLANE
    for j in range(SUBLANE):
        rows = [SUBLANE * i + j for i in range(C // SUBLANE)]
        w = q * jnp.exp2(b - _bcast_rows(b_row, rows, 1)) * _bcast_rows(k_row, rows, 1)
        valid = (t8 <= j) if reverse else (t8 >= j)
        sc = jnp.sum(jnp.where(valid, w, 0.0), axis=-1, keepdims=True)
        o = o + sc * _bcast_rows(v_row, rows, 1)
    return o, st_new


def _hgrn_levels(C):
    out = []
    m = C // 2
    while m >= SUBLANE:
        out.append(m)
        m //= 2
    return out


def _hgrn_masks(C, reverse):
    row = lax.broadcasted_iota(jnp.int32, (C, C), 0)
    col = lax.broadcasted_iota(jnp.int32, (C, C), 1)
    masks = []
    for m in _hgrn_levels(C):
        same = (row // (2 * m)) == (col // (2 * m))
        t_second = (row % (2 * m)) >= m
        s_second = (col % (2 * m)) >= m
        if reverse:
            ok = same & jnp.logical_not(t_second) & s_second
        else:
            ok = same & t_second & jnp.logical_not(s_second)
        masks.append(jnp.where(ok, 1.0, 0.0))
    return masks


def _hgrn_kernel(*refs, with_ctx_out):
    (ql, il, gl, ffl, fbl, qc, ic, gc, ffc, fbc, lb_ref, ng_ref) = refs[:12]
    if with_ctx_out:
        ol_ref, oc_ref, accl, accc, b_scr, k_scr, mask_scr = refs[12:]
    else:
        ol_ref, accl, accc, b_scr, k_scr, mask_scr = refs[12:]
        oc_ref = None
    C = HGRN_CHUNK
    nl = ql.shape[1] // C
    nc = qc.shape[1] // C

    accl[...] = jnp.zeros_like(accl)
    accc[...] = jnp.zeros_like(accc)
    for d in range(2):
        for li, mk in enumerate(_hgrn_masks(C, d == 1)):
            mask_scr[d, li] = mk

    def make_step(qr, ir, frs, acc, n):
        def step(s, sts):
            new = []
            for d in range(2):
                reverse = d == 1
                lbd = lb_ref[d:d + 1, :]
                one_m = 1.0 - lbd
                c = (n - 1 - s) if reverse else s
                r0 = pl.multiple_of(c * C, C)
                sg = jax.nn.sigmoid(frs[d][0, pl.ds(r0, C), :])
                kk = one_m * (1.0 - sg)
                lf = jnp.log2(jnp.maximum(lbd + one_m * sg, F_MIN))
                v_row = lambda r, r0=r0: jnp.broadcast_to(ir[0, pl.ds(r0 + r, 1), :], (SUBLANE, LANE))
                o, st = _hgrn_chunk(qr[0, pl.ds(r0, C), :], kk, ir[0, pl.ds(r0, C), :], v_row, lf, sts[d],
                                    b_scr.at[d], k_scr.at[d], mask_scr.at[d], reverse)
                acc[pl.ds(r0, C), :] += o
                new.append(st)
            return tuple(new)
        return step

    zero = jnp.zeros((LANE, LANE), F32)
    unroll = lambda n: 2 if n % 2 == 0 else 1
    sts = lax.fori_loop(0, nc, make_step(qc, ic, (ffc, fbc), accc, nc), (zero, zero), unroll=unroll(nc))
    lax.fori_loop(0, nl, make_step(ql, il, (ffl, fbl), accl, nl), sts, unroll=unroll(nl))

    def readout(acc, g_ref, o_ref):
        o = _rms(acc[...]) * ng_ref[...]
        g = g_ref[0]
        o_ref[0] = (o * (g * jax.nn.sigmoid(g))).astype(BF16)

    readout(accl, gl, ol_ref)
    if with_ctx_out:
        readout(accc, gc, oc_ref)


def _hgrn(hg_l, hg_c, lb, norm_g, with_ctx_out):
    B, T, _ = hg_l.shape
    Tc = hg_c.shape[1]
    nh = A_HEADS

    def col(n, t):
        return [pl.BlockSpec((1, t, LANE), (lambda b, h, j=j: (b, 0, j * nh + h))) for j in range(n)]

    in_specs = col(5, T) + col(5, Tc) + [pl.BlockSpec((2, LANE), lambda b, h: (0, h)),
                                         pl.BlockSpec((1, LANE), lambda b, h: (0, h))]
    out_specs = [pl.BlockSpec((1, T, LANE), lambda b, h: (b, 0, h))]
    out_shape = [jax.ShapeDtypeStruct((B, T, nh * LANE), BF16)]
    if with_ctx_out:
        out_specs.append(pl.BlockSpec((1, Tc, LANE), lambda b, h: (b, 0, h)))
        out_shape.append(jax.ShapeDtypeStruct((B, Tc, nh * LANE), BF16))
    res = pl.pallas_call(
        functools.partial(_hgrn_kernel, with_ctx_out=with_ctx_out),
        grid=(B, nh),
        in_specs=in_specs,
        out_specs=out_specs,
        out_shape=out_shape,
        scratch_shapes=[pltpu.VMEM((T, LANE), F32), pltpu.VMEM((Tc, LANE), F32),
                        pltpu.VMEM((2, HGRN_CHUNK, LANE), F32), pltpu.VMEM((2, HGRN_CHUNK, LANE), F32),
                        pltpu.VMEM((2, len(_hgrn_levels(HGRN_CHUNK)), HGRN_CHUNK, HGRN_CHUNK), F32)],
        compiler_params=_params("parallel", "parallel"),
        name="hgrn2",
    )(*([hg_l] * 5), *([hg_c] * 5), lb, norm_g.reshape(1, -1))
    return (res[0], res[1]) if with_ctx_out else (res[0], None)


ATTN_KEY_CHUNK = 1024


def _key_chunks(kv):
    off = 0
    for k_ref, v_ref in kv:
        tk = k_ref.shape[1]
        kc = min(tk, ATTN_KEY_CHUNK)
        for c in range(tk // kc):
            yield off, kc, k_ref, v_ref, c
            off += kc


def _score_phase(q, kv, kcols, s_scr):
    mx = None
    for off, kc, k_ref, _, c in _key_chunks(kv):
        s = _dot_nt(k_ref[0, c * kc:(c + 1) * kc, kcols], q)
        s_scr[off:off + kc, :] = s
        cm = s.max(axis=0, keepdims=True)
        mx = cm if mx is None else jnp.maximum(mx, cm)
        yield None
    yield mx


def _pv_phase(m, kv, vrows, s_scr):
    lv = acc = None
    for off, kc, _, v_ref, c in _key_chunks(kv):
        p = jnp.exp2(s_scr[off:off + kc, :] - m)
        ps = p.sum(axis=0, keepdims=True)
        lv = ps if lv is None else lv + ps
        pv = _dot(v_ref[0, vrows, c * kc:(c + 1) * kc], p.astype(BF16))
        acc = pv if acc is None else acc + pv
        yield None
    yield acc, lv


def _interleave(*gens):
    last = [None] * len(gens)
    live = list(range(len(gens)))
    while live:
        for i in list(live):
            try:
                v = next(gens[i])
                if v is not None:
                    last[i] = v
            except StopIteration:
                live.remove(i)
    return last


ATTN_SUB_Q = 512


def _softmax_pv_items(items, kv, scrs):
    out = []
    m_prev = None
    for i, (q, kc, vr) in enumerate(items):
        score = _score_phase(q, kv, kc, scrs[i % 2])
        if i == 0:
            m_prev, = _interleave(score)
        else:
            m_prev, res = _interleave(score, _pv_phase(m_prev, kv, items[i - 1][2], scrs[(i - 1) % 2]))
            out.append(res)
    res, = _interleave(_pv_phase(m_prev, kv, items[-1][2], scrs[(len(items) - 1) % 2]))
    out.append(res)
    return out


def _diff_attn_kernel(*refs, n_kv, lam_init):
    q_ref, lam_ref, ng_ref = refs[0], refs[1], refs[2]
    kv = [(refs[3 + 2 * i], refs[4 + 2 * i]) for i in range(n_kv)]
    o_ref, s1_scr, s2_scr = refs[3 + 2 * n_kv:]
    lp = lam_ref[...]
    lam = (jnp.exp(jnp.sum(lp[0:1] * lp[1:2], axis=-1, keepdims=True))
           - jnp.exp(jnp.sum(lp[2:3] * lp[3:4], axis=-1, keepdims=True)) + lam_init)
    sub = s1_scr.shape[1]
    full = slice(None)
    items = []
    for r in range(q_ref.shape[1] // sub):
        q = q_ref[0, r * sub:(r + 1) * sub, :]
        lane = lax.broadcasted_iota(jnp.int32, q.shape, 1)
        zero = jnp.zeros_like(q)
        items += [(jnp.where(lane < B_HD, q, zero), full, full), (jnp.where(lane < B_HD, zero, q), full, full)]
    res = _softmax_pv_items(items, kv, (s1_scr, s2_scr))
    for r in range(q_ref.shape[1] // sub):
        (a1, l1), (a2, l2) = res[2 * r], res[2 * r + 1]
        o = jnp.transpose(a1 * (1.0 / l1) - a2 * (lam / l2))
        o_ref[0, r * sub:(r + 1) * sub, :] = (_rms(o) * ng_ref[...] * (1.0 - lam_init)).astype(BF16)


def _diff_attn(q_src, kvs, lam_params, norm_g, lam_init, tq):
    B, Tq, _ = q_src.shape
    nh = B_HEADS
    in_specs = [pl.BlockSpec((1, tq, LANE), lambda b, h, i: (b, i, h)),
                pl.BlockSpec((4, B_HD), lambda b, h, i: (0, 0)),
                pl.BlockSpec((1, LANE), lambda b, h, i: (0, h))]
    args = [q_src, lam_params, norm_g.reshape(1, -1)]
    n_keys = 0
    for k_arr, v_arr in kvs:
        Tk = k_arr.shape[1]
        n_keys += Tk
        in_specs.append(pl.BlockSpec((1, Tk, LANE), lambda b, h, i: (b, 0, nh + h)))
        in_specs.append(pl.BlockSpec((1, LANE, Tk), lambda b, h, i: (b, h, 0)))
        args += [k_arr, v_arr]
    return pl.pallas_call(
        functools.partial(_diff_attn_kernel, n_kv=len(kvs), lam_init=lam_init),
        grid=(B, nh, Tq // tq),
        in_specs=in_specs,
        out_specs=pl.BlockSpec((1, tq, LANE), lambda b, h, i: (b, i, h)),
        out_shape=jax.ShapeDtypeStruct((B, Tq, nh * LANE), BF16),
        scratch_shapes=[pltpu.VMEM((n_keys, min(tq, ATTN_SUB_Q)), F32)] * 2,
        compiler_params=_params("parallel", "parallel", "parallel"),
        name="diff_attn",
    )(*args)


def _mla_attn_kernel(*refs, n_kv):
    q_ref = refs[0]
    kv = [(refs[1 + 2 * i], refs[2 + 2 * i]) for i in range(n_kv)]
    o_ref, s1_scr, s2_scr = refs[1 + 2 * n_kv:]
    sub = s1_scr.shape[1]
    c0, c1 = slice(0, LANE), slice(LANE, 2 * LANE)
    v0, v1 = slice(0, C_V), slice(C_V, 2 * C_V)
    items = []
    for r in range(q_ref.shape[1] // sub):
        rows = slice(r * sub, (r + 1) * sub)
        items += [(q_ref[0, rows, c0], c0, v0), (q_ref[0, rows, c1], c1, v1)]
    res = _softmax_pv_items(items, kv, (s1_scr, s2_scr))
    for r in range(q_ref.shape[1] // sub):
        (a0, l0), (a1, l1) = res[2 * r], res[2 * r + 1]
        ot = jnp.concatenate([a0 * (1.0 / l0), a1 * (1.0 / l1)], axis=0)
        o_ref[0, r * sub:(r + 1) * sub, :] = jnp.transpose(ot).astype(BF16)


def _mla_attn(q, kvs, tq):
    B, Tq, _ = q.shape
    npair = C_HEADS // 2
    in_specs = [pl.BlockSpec((1, tq, 2 * LANE), lambda b, h, i: (b, i, h))]
    args = [q]
    n_keys = 0
    for k_arr, v_arr in kvs:
        Tk = k_arr.shape[1]
        n_keys += Tk
        in_specs.append(pl.BlockSpec((1, Tk, 2 * LANE), lambda b, h, i: (b, 0, h)))
        in_specs.append(pl.BlockSpec((1, LANE, Tk), lambda b, h, i: (b, h, 0)))
        args += [k_arr, v_arr]
    return pl.pallas_call(
        functools.partial(_mla_attn_kernel, n_kv=len(kvs)),
        grid=(B, npair, Tq // tq),
        in_specs=in_specs,
        out_specs=pl.BlockSpec((1, tq, LANE), lambda b, h, i: (b, i, h)),
        out_shape=jax.ShapeDtypeStruct((B, Tq, C_HEADS * C_V), BF16),
        scratch_shapes=[pltpu.VMEM((n_keys, min(tq, ATTN_SUB_Q)), F32)] * 2,
        compiler_params=_params("parallel", "parallel", "parallel"),
        name="mla_attn",
    )(*args)


def _mla_proj_kernel(x_ref, gq_ref, gkv_ref, wq_ref, wkv_ref, cq_ref, saq_ref, sbq_ref, ck_ref, sak_ref,
                     sbk_ref, q_ref, k_ref, v_ref):
    x = x_ref[0]
    xkv = x[:, :MLA_KV_IN]
    lane = lax.broadcasted_iota(jnp.int32, xkv.shape, 1)
    lat = jnp.where(lane < C_KV_LORA, xkv, 0.0)
    ms = jnp.sum(lat * lat, axis=-1, keepdims=True) * (1.0 / C_KV_LORA)
    hk = jnp.where(lane < C_KV_LORA, lat * lax.rsqrt(ms + EPS) * gkv_ref[...], xkv).astype(BF16)
    hq = (_rms(x[:, MLA_KV_IN:]) * gq_ref[...]).astype(BF16)
    nk = C_HEADS * LANE
    aq = _dot(hq, wq_ref[...])
    akv = _dot(hk, wkv_ref[...])
    for g in range(C_HEADS):
        sl = slice(g * LANE, (g + 1) * LANE)
        q_ref[0, :, sl] = _rope_apply(aq[:, sl], cq_ref[...], saq_ref[...], sbq_ref[...], C_ROPE // 2).astype(BF16)
        k_ref[0, :, sl] = _rope_apply(akv[:, sl], ck_ref[...], sak_ref[...], sbk_ref[...], C_ROPE // 2).astype(BF16)
    v_ref[0] = jnp.transpose(akv[:, nk:]).astype(BF16)


def _mla_proj(mla, gq, gkv, wq, wkv, tabs_q, tabs_k, tm):
    Bx, Tx, _ = mla.shape
    row = lambda b, i: (b, i, 0)
    const = lambda b, i: (0, 0)
    tab = pl.BlockSpec((tm, LANE), lambda b, i: (i, 0))
    nk = C_HEADS * LANE
    outs = [(nk, BF16, False), (nk, BF16, False), (C_HEADS * C_V, BF16, True)]
    return pl.pallas_call(
        _mla_proj_kernel,
        grid=(Bx, Tx // tm),
        in_specs=[pl.BlockSpec((1, tm, MLA_W), row),
                  pl.BlockSpec((1, C_Q_LORA), const),
                  pl.BlockSpec((1, MLA_KV_IN), const),
                  _resident(wq.shape, const),
                  _resident(wkv.shape, const),
                  tab, tab, tab, tab, tab, tab],
        out_specs=[_out_spec(tm, w, t) for w, _, t in outs],
        out_shape=[_out_shape(Bx, Tx, w, dt, t) for w, dt, t in outs],
        compiler_params=_params("parallel", "parallel"),
        name="mla_proj",
    )(mla, gq, gkv, wq, wkv, *tabs_q, *tabs_k)


def _merge_kernel(a_ref, b_ref, c_ref, gate_ref, wb_ref, wo_ref, x_ref, mod_ref, o_ref):
    D = x_ref.shape[2]
    z = None
    for n, br in enumerate((a_ref, b_ref, c_ref)):
        y = _dot(br[0], wb_ref[n])
        t = gate_ref[0, :, n * D:(n + 1) * D].astype(F32) * y
        z = t if z is None else z + t
    out = _dot(z.astype(BF16), wo_ref[...])
    o_ref[0] = x_ref[0] + mod_ref[0, 2:3] * out


def _merge(a, b, c, gates, wb, wo, x, mods, mod_row, tm):
    Bx, Tx, D = x.shape
    row = lambda bb, i: (bb, i, 0)
    br = pl.BlockSpec((1, tm, BRANCH_W), row)
    return pl.pallas_call(
        _merge_kernel,
        grid=(Bx, Tx // tm),
        in_specs=[br, br, br,
                  pl.BlockSpec((1, tm, GATE_W), row),
                  _resident(wb.shape, lambda bb, i: (0, 0, 0)),
                  _resident(wo.shape, lambda bb, i: (0, 0)),
                  pl.BlockSpec((1, tm, D), row),
                  pl.BlockSpec((1, 6, D), lambda bb, i: (mod_row(bb), 0, 0))],
        out_specs=pl.BlockSpec((1, tm, D), row),
        out_shape=jax.ShapeDtypeStruct((Bx, Tx, D), F32),
        compiler_params=_params("parallel", "parallel"),
        name="merge_out",
    )(a, b, c, gates, wb, wo, x, mods)


def _prefix_count(mask, blk):
    T = mask.shape[1]
    r = lax.broadcasted_iota(jnp.int32, (blk, blk), 0)
    c = lax.broadcasted_iota(jnp.int32, (blk, blk), 1)
    tri = jnp.where(r < c, 1.0, 0.0).astype(BF16)
    parts = []
    carry = jnp.zeros((mask.shape[0], 1), F32)
    for i in range(T // blk):
        mb = mask[:, i * blk:(i + 1) * blk]
        parts.append(_dot(mb.astype(BF16), tri) + carry)
        carry = carry + jnp.sum(mb, axis=1, keepdims=True)
    return jnp.concatenate(parts, axis=1) if len(parts) > 1 else parts[0]


def _router_kernel(x_ref, g_ref, mod_ref, rw_ref, h_ref, aff_ref):
    mod = mod_ref[0]
    h = ((_rms(x_ref[0]) * g_ref[...]) * (1.0 + mod[4:5]) + mod[3:4]).astype(BF16)
    h_ref[0] = h
    logits = _dot(h, rw_ref[...])
    lane = lax.broadcasted_iota(jnp.int32, logits.shape, 1)
    lg = jnp.where(lane < N_EXPERTS, logits, -jnp.inf)
    e = jnp.exp(lg - lg.max(axis=-1, keepdims=True))
    aff_ref[0] = e / e.sum(axis=-1, keepdims=True)


def _router(x, norm_g, mods, mod_row, rw, tm):
    Bx, Tx, D = x.shape
    row = lambda b, i: (b, i, 0)
    outs = [(D, BF16), (LANE, F32)]
    return pl.pallas_call(
        _router_kernel,
        grid=(Bx, Tx // tm),
        in_specs=[pl.BlockSpec((1, tm, D), row),
                  pl.BlockSpec((1, D), lambda b, i: (0, 0)),
                  pl.BlockSpec((1, 6, D), lambda b, i: (mod_row(b), 0, 0)),
                  pl.BlockSpec((D, LANE), lambda b, i: (0, 0))],
        out_specs=[pl.BlockSpec((1, tm, w), row) for w, _ in outs],
        out_shape=[jax.ShapeDtypeStruct((Bx, Tx, w), dt) for w, dt in outs],
        compiler_params=_params("parallel", "parallel"),
        name="router",
    )(x, norm_g.reshape(1, D), mods, rw)


def _select_kernel(aff_ref, pos_ref, post_ref, afft_ref, *, cap):
    T = aff_ref.shape[1]
    aff = jnp.transpose(aff_ref[0])[:N_EXPERTS]
    bits = pltpu.bitcast(aff, jnp.int32)

    def search(i, thr):
        cand = thr | (jnp.int32(1) << (30 - i))
        cnt = jnp.sum(jnp.where(bits >= cand, 1.0, 0.0), axis=1, keepdims=True)
        return jnp.where(cnt >= cap, cand, thr)

    thr = lax.fori_loop(0, 31, search, jnp.zeros((N_EXPERTS, 1), jnp.int32))
    gt = jnp.where(bits > thr, 1.0, 0.0)
    eq = jnp.where(bits == thr, 1.0, 0.0)
    need = cap - jnp.sum(gt, axis=1, keepdims=True)
    blk = min(T, 256)
    sel = gt + eq * jnp.where(_prefix_count(eq, blk) < need, 1.0, 0.0)
    post = jnp.where(sel > 0.0, _prefix_count(sel, blk), -1.0)
    post_ref[0] = post
    afft_ref[0] = aff
    pad = jnp.full((LANE - N_EXPERTS, T), -1.0, F32)
    pos_ref[0] = jnp.transpose(jnp.concatenate([post, pad], axis=0))


def _select(aff, cap):
    Bx, Tx, _ = aff.shape
    blk = pl.BlockSpec((1, Tx, LANE), lambda b: (b, 0, 0))
    blk_t = pl.BlockSpec((1, N_EXPERTS, Tx), lambda b: (b, 0, 0))
    return pl.pallas_call(
        functools.partial(_select_kernel, cap=cap),
        grid=(Bx,),
        in_specs=[blk],
        out_specs=[blk, blk_t, blk_t],
        out_shape=[jax.ShapeDtypeStruct((Bx, Tx, LANE), F32),
                   jax.ShapeDtypeStruct((Bx, N_EXPERTS, Tx), F32),
                   jax.ShapeDtypeStruct((Bx, N_EXPERTS, Tx), F32)],
        compiler_params=_params("parallel"),
        name="moe_select",
    )(aff)


DISPATCH_GROUP = 4


def _dispatch_kernel(h_ref, post_ref, afft_ref, xs_ref, gs_ref, *, cap):
    g = pl.program_id(1)
    T = h_ref.shape[1]
    slot = lax.broadcasted_iota(jnp.int32, (cap, T), 0).astype(F32)
    for j in range(DISPATCH_GROUP):
        e = g * DISPATCH_GROUP + j
        hit = post_ref[0, pl.ds(e, 1), :] == slot
        xs_ref[j] = _dot(jnp.where(hit, 1.0, 0.0).astype(BF16), h_ref[0]).astype(BF16)
        gs_ref[j] = jnp.sum(jnp.where(hit, afft_ref[0, pl.ds(e, 1), :], 0.0), axis=1, keepdims=True)


def _dispatch(h, post, afft, cap):
    Bx, Tx, D = h.shape
    tok = pl.BlockSpec((1, N_EXPERTS, Tx), lambda b, g: (b, 0, 0))
    return pl.pallas_call(
        functools.partial(_dispatch_kernel, cap=cap),
        grid=(Bx, N_EXPERTS // DISPATCH_GROUP),
        in_specs=[pl.BlockSpec((1, Tx, D), lambda b, g: (b, 0, 0)), tok, tok],
        out_specs=[pl.BlockSpec((DISPATCH_GROUP, cap, D), lambda b, g: (g, b, 0)),
                   pl.BlockSpec((DISPATCH_GROUP, cap, 1), lambda b, g: (g, b, 0))],
        out_shape=[jax.ShapeDtypeStruct((N_EXPERTS, Bx * cap, D), BF16),
                   jax.ShapeDtypeStruct((N_EXPERTS, Bx * cap, 1), F32)],
        compiler_params=_params("parallel", "parallel"),
        name="moe_dispatch",
    )(h, post, afft)


def _expert_kernel(x_ref, gs_ref, wg_ref, wu_ref, wd_ref, y_ref):
    x = x_ref[0]
    ff = wg_ref.shape[2]
    fc = 512
    acc = None
    for f in range(ff // fc):
        a = _dot(x, wg_ref[0, :, f * fc:(f + 1) * fc])
        u = _dot(x, wu_ref[0, :, f * fc:(f + 1) * fc])
        hm = (a * jax.nn.sigmoid(a) * u).astype(BF16)
        c = _dot(hm, wd_ref[0, f * fc:(f + 1) * fc, :])
        acc = c if acc is None else acc + c
    y_ref[0] = (acc * gs_ref[0]).astype(BF16)


def _experts(xs, gs, wg, wu, wd, tm):
    E, M, D = xs.shape
    FF = wg.shape[2]
    return pl.pallas_call(
        _expert_kernel,
        grid=(E, M // tm),
        in_specs=[pl.BlockSpec((1, tm, D), lambda e, i: (e, i, 0)),
                  pl.BlockSpec((1, tm, 1), lambda e, i: (e, i, 0)),
                  pl.BlockSpec((1, D, FF), lambda e, i: (e, 0, 0)),
                  pl.BlockSpec((1, D, FF), lambda e, i: (e, 0, 0)),
                  pl.BlockSpec((1, FF, D), lambda e, i: (e, 0, 0))],
        out_specs=pl.BlockSpec((1, tm, D), lambda e, i: (e, i, 0)),
        out_shape=jax.ShapeDtypeStruct((E, M, D), BF16),
        compiler_params=_params("parallel", "parallel"),
        name="moe_experts",
    )(xs, gs, wg, wu, wd)


def _combine_kernel(y_ref, pos_ref, x_ref, mod_ref, o_ref, *, cap, group):
    width = group * cap
    pos = pos_ref[0].astype(BF16)
    src = lax.broadcasted_iota(jnp.int32, (LANE, width), 0)
    dst = lax.broadcasted_iota(jnp.int32, (LANE, width), 1)
    slot = (lax.broadcasted_iota(jnp.int32, (1, width), 1) % cap).astype(F32)
    out = None
    for g in range(N_EXPERTS // group):
        spread = jnp.where(dst // cap + g * group == src, 1.0, 0.0).astype(BF16)
        onehot = jnp.where(_dot(pos, spread) == slot, 1.0, 0.0).astype(BF16)
        c = _dot(onehot, y_ref[g * group:(g + 1) * group].reshape(width, y_ref.shape[2]))
        out = c if out is None else out + c
    o_ref[0] = x_ref[0] + mod_ref[0, 5:6] * out


def _combine(y, pos, x, mods, mod_row, cap, tm):
    Bx, Tx, D = x.shape
    row = lambda b, i: (b, i, 0)
    group = max(4, LANE // cap)
    return pl.pallas_call(
        functools.partial(_combine_kernel, cap=cap, group=group),
        grid=(Bx, Tx // tm),
        in_specs=[pl.BlockSpec((N_EXPERTS, cap, D), lambda b, i: (0, b, 0)),
                  pl.BlockSpec((1, tm, LANE), row),
                  pl.BlockSpec((1, tm, D), row),
                  pl.BlockSpec((1, 6, D), lambda b, i: (mod_row(b), 0, 0))],
        out_specs=pl.BlockSpec((1, tm, D), row),
        out_shape=jax.ShapeDtypeStruct((Bx, Tx, D), F32),
        compiler_params=_params("parallel", "parallel"),
        name="moe_combine",
    )(y, pos, x, mods)


def _moe(x, norm_g, mods, mod_row, rw, wg, wu, wd, tm):
    Bx, Tx, D = x.shape
    cap = CAPACITY_FACTOR * Tx // N_EXPERTS
    h, aff = _router(x, norm_g, mods, mod_row, rw, tm)
    pos, post, afft = _select(aff, cap)
    xs, gs = _dispatch(h, post, afft, cap)
    y = _experts(xs, gs, wg, wu, wd, min(Bx * cap, 512))
    return _combine(y, pos, x, mods, mod_row, cap, tm)


def _final_norm_kernel(x_ref, g_ref, o_ref):
    o_ref[0] = _rms(x_ref[0]) * g_ref[...]


def _final_norm(x, g, tm):
    Bx, Tx, D = x.shape
    return pl.pallas_call(
        _final_norm_kernel,
        grid=(Bx, Tx // tm),
        in_specs=[pl.BlockSpec((1, tm, D), lambda b, i: (b, i, 0)),
                  pl.BlockSpec((1, D), lambda b, i: (0, 0))],
        out_specs=pl.BlockSpec((1, tm, D), lambda b, i: (b, i, 0)),
        out_shape=jax.ShapeDtypeStruct((Bx, Tx, D), F32),
        compiler_params=_params("parallel", "parallel"),
        name="final_norm",
    )(x, g.reshape(1, D))


def _rope_tables(T, rot_dim, lane_lo, period, scale):
    rows = T // GRID_W
    row = jnp.repeat(jnp.arange(rows, dtype=F32), GRID_W)
    colp = jnp.tile(jnp.arange(GRID_W, dtype=F32), rows)
    n_freq = rot_dim // 4
    inv_freq = ROPE_BASE ** (-jnp.arange(n_freq, dtype=F32) / n_freq)
    ang = jnp.concatenate([row[:, None] * inv_freq, colp[:, None] * inv_freq], axis=-1)
    cos_h, sin_h = jnp.cos(ang), jnp.sin(ang)
    half = rot_dim // 2
    cos_g = jnp.ones((T, period), F32)
    sa_g = jnp.zeros((T, period), F32)
    sb_g = jnp.zeros((T, period), F32)
    cos_g = cos_g.at[:, lane_lo:lane_lo + rot_dim].set(jnp.concatenate([cos_h, cos_h], axis=-1))
    sa_g = sa_g.at[:, lane_lo:lane_lo + half].set(-sin_h)
    sb_g = sb_g.at[:, lane_lo + half:lane_lo + rot_dim].set(sin_h)
    rep = LANE // period
    return tuple(jnp.tile(t, (1, rep)) * scale for t in (cos_g, sa_g, sb_g))


def _identity_tables(T, scale):
    return (jnp.full((T, LANE), scale, F32), jnp.zeros((T, LANE), F32), jnp.zeros((T, LANE), F32))


def _prep_w_in(w):
    D = w.shape[0]
    o_diff = HGRN_W
    o_cq = o_diff + 1536
    o_ckv = o_cq + C_Q_LORA
    o_kr = o_ckv + C_KV_LORA
    o_gate = o_kr + C_ROPE
    pad = jnp.zeros((D, MLA_KV_IN - C_KV_LORA - C_ROPE), w.dtype)
    return jnp.concatenate([w[:, :o_cq], w[:, o_ckv:o_kr], w[:, o_kr:o_gate], pad, w[:, o_cq:o_ckv],
                            w[:, o_gate:]], axis=1).astype(BF16)


def _prep_mla_w(w_uq, w_ukv):
    wq = w_uq.reshape(C_Q_LORA, C_HEADS, C_NOPE + C_ROPE)
    wq = jnp.pad(wq, ((0, 0), (0, 0), (0, LANE - C_NOPE - C_ROPE))).reshape(C_Q_LORA, C_HEADS * LANE)
    wkv = w_ukv.reshape(C_KV_LORA, C_HEADS, C_NOPE + C_V)
    wk = jnp.pad(wkv[:, :, :C_NOPE], ((0, 0), (0, 0), (0, LANE - C_NOPE)))
    place = jnp.zeros((C_ROPE, C_HEADS, LANE), F32).at[:, :, C_NOPE:C_NOPE + C_ROPE].set(
        jnp.broadcast_to(jnp.eye(C_ROPE, dtype=F32)[:, None, :], (C_ROPE, C_HEADS, C_ROPE)))
    wk = jnp.concatenate([wk, place, jnp.zeros((MLA_KV_IN - C_KV_LORA - C_ROPE, C_HEADS, LANE), F32)], axis=0)
    wv = jnp.pad(wkv[:, :, C_NOPE:].reshape(C_KV_LORA, C_HEADS * C_V), ((0, MLA_KV_IN - C_KV_LORA), (0, 0)))
    wkv_comb = jnp.concatenate([wk.reshape(MLA_KV_IN, C_HEADS * LANE), wv], axis=1)
    return wq.astype(BF16), wkv_comb.astype(BF16)


def kernel(x, c, ctx, c_ctx, ada_w, ada_b, norm_mix_g, norm_ffn_g, w_in, hgrn_lb, hgrn_norm_g, diff_lambda,
           diff_norm_g, mla_q_norm_g, mla_w_uq, mla_kv_norm_g, mla_w_ukv, w_branch, w_out, router_w,
           exp_w_gate, exp_w_up, exp_w_down, final_norm_g):
    B, T, D = x.shape
    Tc = ctx.shape[1]
    depth = ada_w.shape[0]
    tm_l = min(T, 512)
    tm_c = min(Tc, 256)
    tq = min(Tc, 256)
    tq_l = min(T, 2048)

    rows = ((B + 1 + SUBLANE - 1) // SUBLANE) * SUBLANE
    vecs = jnp.concatenate([c, c_ctx[None], jnp.zeros((rows - B - 1, D), F32)], axis=0)
    mods_all = _modulation(vecs, ada_w, ada_b).reshape(depth, rows, 6, D)
    lat_row = lambda b: b
    ctx_row = lambda b: B

    p = jax.nn.softmax(hgrn_lb.astype(F32), axis=0)
    lower_bounds = jnp.clip(jnp.cumsum(p, axis=0) - p[0], 0.0, LB_MAX)

    d_scale = B_HD ** -0.5 * math.log2(math.e)
    c_scale = (C_NOPE + C_ROPE) ** -0.5 * math.log2(math.e)
    dq_l = _rope_tables(T, B_HD, 0, B_HD, d_scale)
    dk_l = _rope_tables(T, B_HD, 0, B_HD, 1.0)
    dq_c, dk_c = _identity_tables(Tc, d_scale), _identity_tables(Tc, 1.0)
    mq_l = _rope_tables(T, C_ROPE, C_NOPE, LANE, c_scale)
    mk_l = _rope_tables(T, C_ROPE, C_NOPE, LANE, 1.0)
    mq_c, mk_c = _identity_tables(Tc, c_scale), _identity_tables(Tc, 1.0)

    xl, xc = x, ctx
    for layer in range(depth):
        with_ctx = layer < depth - 1
        mods = mods_all[layer]
        w_perm = _prep_w_in(w_in[layer])
        wq, wkv = _prep_mla_w(mla_w_uq[layer], mla_w_ukv[layer])
        wb = w_branch[layer].astype(BF16)
        wo = w_out[layer].astype(BF16)
        rw = jnp.pad(router_w[layer], ((0, 0), (0, LANE - N_EXPERTS))).astype(BF16)
        wg = exp_w_gate[layer].astype(BF16)
        wu = exp_w_up[layer].astype(BF16)
        wd = exp_w_down[layer].astype(BF16)
        lam_init = 0.8 - 0.6 * math.exp(-0.3 * layer)
        gq = mla_q_norm_g[layer].reshape(1, -1)
        gkv = jnp.pad(mla_kv_norm_g[layer], (0, MLA_KV_IN - C_KV_LORA)).reshape(1, -1)

        hg_l, dqk_l, dv_l, ml_l, gt_l = _layer_in(xl, norm_mix_g[layer], mods, lat_row, w_perm, dq_l, dk_l, tm_l)
        hg_c, dqk_c, dv_c, ml_c, gt_c = _layer_in(xc, norm_mix_g[layer], mods, ctx_row, w_perm, dq_c, dk_c, tm_c)

        a_l, a_c = _hgrn(hg_l, hg_c, lower_bounds[layer], hgrn_norm_g[layer], with_ctx)

        b_l = _diff_attn(dqk_l, [(dqk_c, dv_c), (dqk_l, dv_l)], diff_lambda[layer], diff_norm_g[layer], lam_init, tq_l)
        mq_lat, mk_lat, mv_lat = _mla_proj(ml_l, gq, gkv, wq, wkv, mq_l, mk_l, tm_l)
        mq_ctx, mk_ctx, mv_ctx = _mla_proj(ml_c, gq, gkv, wq, wkv, mq_c, mk_c, tm_c)
        m_l = _mla_attn(mq_lat, [(mk_ctx, mv_ctx), (mk_lat, mv_lat)], tq_l)

        xl = _merge(a_l, b_l, m_l, gt_l, wb, wo, xl, mods, lat_row, tm_l)
        xl = _moe(xl, norm_ffn_g[layer], mods, lat_row, rw, wg, wu, wd, tm_l)
        if with_ctx:
            b_c = _diff_attn(dqk_c, [(dqk_c, dv_c)], diff_lambda[layer], diff_norm_g[layer], lam_init, tq)
            m_c = _mla_attn(mq_ctx, [(mk_ctx, mv_ctx)], tq)
            xc = _merge(a_c, b_c, m_c, gt_c, wb, wo, xc, mods, ctx_row, tm_c)
            xc = _moe(xc, norm_ffn_g[layer], mods, ctx_row, rw, wg, wu, wd, tm_c)
    return _final_norm(xl, final_norm_g, tm_l)
```

```python
import functools
import math

import jax
import jax.numpy as jnp
from jax import lax
from jax.experimental import pallas as pl
from jax.experimental.pallas import tpu as pltpu

F32 = jnp.float32
BF16 = jnp.bfloat16

EPS = 1e-6
ROPE_BASE = 10000.0
GRID_W = 64
F_MIN = 1e-30
LB_MAX = 1.0 - 1e-6

A_HEADS, A_DK = 4, 128
B_HEADS, B_HD = 4, 64
C_HEADS, C_NOPE, C_ROPE, C_V = 8, 64, 32, 64
C_Q_LORA, C_KV_LORA = 384, 256
N_BRANCH, BRANCH_W = 3, 512
N_EXPERTS = 16
CAPACITY_FACTOR = 2

LANE = 128
SUBLANE = 8
VMEM_LIMIT = 56 * 1024 * 1024

HGRN_W = 5 * 512
MLA_W = 768
MLA_KV_IN = 384
GATE_W = N_BRANCH * 1024
HGRN_CHUNK = 128


def _params(*sem):
    return pltpu.CompilerParams(dimension_semantics=sem, vmem_limit_bytes=VMEM_LIMIT)


def _resident(shape, index_map):
    return pl.BlockSpec(shape, index_map, pipeline_mode=pl.Buffered(1))


def _rms(x, eps=EPS):
    return x * lax.rsqrt(jnp.mean(x * x, axis=-1, keepdims=True) + eps)


def _dot(a, b):
    return jnp.dot(a, b, preferred_element_type=F32)


def _dot_nt(a, b):
    return lax.dot_general(a, b, (((1,), (1,)), ((), ())), preferred_element_type=F32)


def _dot_tn(a, b):
    return lax.dot_general(a, b, (((0,), (0,)), ((), ())), preferred_element_type=F32)


def _mods_kernel(v_ref, w_ref, b_ref, o_ref):
    v = v_ref[...]
    s = (v * jax.nn.sigmoid(v)).astype(BF16)
    o_ref[0] = _dot(s, w_ref[0].astype(BF16)) + b_ref[0]


def _modulation(vecs, ada_w, ada_b):
    L, D, N = ada_w.shape
    R = vecs.shape[0]
    tn = 1536
    return pl.pallas_call(
        _mods_kernel,
        grid=(L, N // tn),
        in_specs=[pl.BlockSpec((R, D), lambda l, j: (0, 0)),
                  pl.BlockSpec((1, D, tn), lambda l, j: (l, 0, j)),
                  pl.BlockSpec((1, 1, tn), lambda l, j: (l, 0, j))],
        out_specs=pl.BlockSpec((1, R, tn), lambda l, j: (l, 0, j)),
        out_shape=jax.ShapeDtypeStruct((L, R, N), F32),
        compiler_params=_params("parallel", "parallel"),
        name="adaln_mods",
    )(vecs, ada_w, ada_b.reshape(L, 1, N))


def _rope_apply(x, cos, sin_a, sin_b, half):
    return (x * cos + pltpu.roll(x, LANE - half, axis=1) * sin_a + pltpu.roll(x, half, axis=1) * sin_b)


def _layer_in_kernel(x_ref, g_ref, mod_ref, w_ref, cq_ref, saq_ref, sbq_ref, ck_ref, sak_ref, sbk_ref,
                     hg_ref, dqk_ref, dv_ref, mla_ref, gate_ref, h_scr):
    mod = mod_ref[0]
    half = x_ref.shape[1] // 2
    c0 = 0
    for r in range(2):
        rows = slice(r * half, (r + 1) * half)
        h_scr[rows] = ((_rms(x_ref[0, rows]) * g_ref[...]) * (1.0 + mod[1:2]) + mod[0:1]).astype(BF16)
        hg_ref[0, rows] = _dot(h_scr[rows], w_ref[:, c0:c0 + HGRN_W])
    h = h_scr[...]
    c0 += HGRN_W
    for j in range(2):
        acc = _dot(h, w_ref[:, c0:c0 + 512])
        cos, sa, sb = (cq_ref, saq_ref, sbq_ref) if j == 0 else (ck_ref, sak_ref, sbk_ref)
        for g in range(4):
            xg = acc[:, g * LANE:(g + 1) * LANE]
            dqk_ref[0, :, j * 512 + g * LANE:j * 512 + (g + 1) * LANE] = _rope_apply(
                xg, cos[...], sa[...], sb[...], B_HD // 2).astype(BF16)
        c0 += 512
    dv_ref[0] = jnp.transpose(_dot(h, w_ref[:, c0:c0 + 512])).astype(BF16)
    c0 += 512
    mla_ref[0] = _dot(h, w_ref[:, c0:c0 + MLA_W])
    c0 += MLA_W
    for j in range(GATE_W // 1024):
        gate_ref[0, :, j * 1024:(j + 1) * 1024] = jax.nn.sigmoid(_dot(h, w_ref[:, c0:c0 + 1024])).astype(BF16)
        c0 += 1024


def _out_spec(tm, width, transposed):
    if transposed:
        return pl.BlockSpec((1, width, tm), lambda b, i: (b, 0, i))
    return pl.BlockSpec((1, tm, width), lambda b, i: (b, i, 0))


def _out_shape(bx, tx, width, dtype, transposed):
    return jax.ShapeDtypeStruct((bx, width, tx) if transposed else (bx, tx, width), dtype)


def _layer_in(x, norm_g, mods, mod_row, w_perm, tabs_q, tabs_k, tm):
    Bx, Tx, D = x.shape
    NW = w_perm.shape[1]
    row = lambda b, i: (b, i, 0)
    tab = pl.BlockSpec((tm, LANE), lambda b, i: (i, 0))
    outs = [(HGRN_W, F32, False), (1024, BF16, False), (512, BF16, True), (MLA_W, F32, False),
            (GATE_W, BF16, False)]
    return pl.pallas_call(
        _layer_in_kernel,
        grid=(Bx, Tx // tm),
        in_specs=[pl.BlockSpec((1, tm, D), row),
                  pl.BlockSpec((1, D), lambda b, i: (0, 0)),
                  pl.BlockSpec((1, 6, D), lambda b, i: (mod_row(b), 0, 0)),
                  _resident((D, NW), lambda b, i: (0, 0)),
                  tab, tab, tab, tab, tab, tab],
        out_specs=[_out_spec(tm, w, t) for w, _, t in outs],
        out_shape=[_out_shape(Bx, Tx, w, dt, t) for w, dt, t in outs],
        scratch_shapes=[pltpu.VMEM((tm, D), BF16)],
        compiler_params=_params("parallel", "parallel"),
        name="layer_in",
    )(x, norm_g.reshape(1, D), mods, w_perm, *tabs_q, *tabs_k)


def _cumsum_rows(x, reverse):
    n = x.shape[0]
    s = 1
    while s < n:
        if s < SUBLANE:
            row = lax.broadcasted_iota(jnp.int32, x.shape, 0)
            if reverse:
                sh = jnp.where(row < n - s, pltpu.roll(x, n - s, axis=0), 0.0)
            else:
                sh = jnp.where(row >= s, pltpu.roll(x, s, axis=0), 0.0)
        else:
            z = jnp.zeros((s, x.shape[1]), x.dtype)
            sh = jnp.concatenate([x[s:], z], axis=0) if reverse else jnp.concatenate([z, x[:n - s]], axis=0)
        x = x + sh
        s *= 2
    return x


def _bcast_rows(load_row, rows, reps):
    parts = []
    for r in rows:
        parts += [load_row(r)] * reps
    return jnp.concatenate(parts, axis=0)


def _hgrn_chunk(q, k_in, v, v_row, lf, st, b_scr, k_scr, mask_ref, reverse):
    C = q.shape[0]
    b = _cumsum_rows(lf, reverse)
    b_scr[...] = b
    k_scr[...] = k_in
    b_row = lambda r: jnp.broadcast_to(b_scr[pl.ds(r, 1), :], (SUBLANE, LANE))
    k_row = lambda r: jnp.broadcast_to(k_scr[pl.ds(r, 1), :], (SUBLANE, LANE))
    b_last = b[0:1] if reverse else b[C - 1:C]
    qd = (q * jnp.exp2(b)).astype(BF16)
    o = _dot_nt(qd, st.astype(BF16))
    kd = (k_in * jnp.exp2(b_last - b)).astype(BF16)
    vb = v.astype(BF16)
    st_new = st * jnp.exp2(b_last) + _dot_tn(vb, kd)

    scores = None
    for li, m in enumerate(_hgrn_levels(C)):
        mid = m if reverse else m - 1
        anchor = _bcast_rows(b_row, [i * 2 * m + mid for i in range(C // (2 * m))], 2 * m // SUBLANE)
        e = jnp.exp2(-jnp.abs(b - anchor))
        sc = _dot_nt((q * e).astype(BF16), (k_in * e).astype(BF16)) * mask_ref[li]
        scores = sc if scores is None else scores + sc
    o = o + _dot(scores.astype(BF16), vb)

    t8 = lax.broadcasted_iota(jnp.int32, (C, 1), 0) % SUBLANE
    for j in range(SUBLANE):
        rows = [SUBLANE * i + j for i in range(C // SUBLANE)]
        w = q * jnp.exp2(b - _bcast_rows(b_row, rows, 1)) * _bcast_rows(k_row, rows, 1)
        valid = (t8 <= j) if reverse else (t8 >= j)
        sc = jnp.sum(jnp.where(valid, w, 0.0), axis=-1, keepdims=True)
        o = o + sc * _bcast_rows(v_row, rows, 1)
    return o, st_new


def _hgrn_levels(C):
    out = []
    m = C // 2
    while m >= SUBLANE:
        out.append(m)
        m //= 2
    return out


def _hgrn_masks(C, reverse):
    row = lax.broadcasted_iota(jnp.int32, (C, C), 0)
    col = lax.broadcasted_iota(jnp.int32, (C, C), 1)
    masks = []
    for m in _hgrn_levels(C):
        same = (row // (2 * m)) == (col // (2 * m))
        t_second = (row % (2 * m)) >= m
        s_second = (col % (2 * m)) >= m
        if reverse:
            ok = same & jnp.logical_not(t_second) & s_second
        else:
            ok = same & t_second & jnp.logical_not(s_second)
        masks.append(jnp.where(ok, 1.0, 0.0))
    return masks


def _hgrn_kernel(*refs, with_ctx_out):
    (ql, il, gl, ffl, fbl, qc, ic, gc, ffc, fbc, lb_ref, ng_ref) = refs[:12]
    if with_ctx_out:
        ol_ref, oc_ref, accl, accc, b_scr, k_scr, mask_scr = refs[12:]
    else:
        ol_ref, accl, accc, b_scr, k_scr, mask_scr = refs[12:]
        oc_ref = None
    C = HGRN_CHUNK
    nl = ql.shape[1] // C
    nc = qc.shape[1] // C

    accl[...] = jnp.zeros_like(accl)
    accc[...] = jnp.zeros_like(accc)
    for d in range(2):
        for li, mk in enumerate(_hgrn_masks(C, d == 1)):
            mask_scr[d, li] = mk

    def make_step(qr, ir, frs, acc, n):
        def step(s, sts):
            new = []
            for d in range(2):
                reverse = d == 1
                lbd = lb_ref[d:d + 1, :]
                one_m = 1.0 - lbd
                c = (n - 1 - s) if reverse else s
                r0 = pl.multiple_of(c * C, C)
                sg = jax.nn.sigmoid(frs[d][0, pl.ds(r0, C), :])
                kk = one_m * (1.0 - sg)
                lf = jnp.log2(jnp.maximum(lbd + one_m * sg, F_MIN))
                v_row = lambda r, r0=r0: jnp.broadcast_to(ir[0, pl.ds(r0 + r, 1), :], (SUBLANE, LANE))
                o, st = _hgrn_chunk(qr[0, pl.ds(r0, C), :], kk, ir[0, pl.ds(r0, C), :], v_row, lf, sts[d],
                                    b_scr.at[d], k_scr.at[d], mask_scr.at[d], reverse)
                acc[pl.ds(r0, C), :] += o
                new.append(st)
            return tuple(new)
        return step

    zero = jnp.zeros((LANE, LANE), F32)
    unroll = lambda n: 2 if n % 2 == 0 else 1
    sts = lax.fori_loop(0, nc, make_step(qc, ic, (ffc, fbc), accc, nc), (zero, zero), unroll=unroll(nc))
    lax.fori_loop(0, nl, make_step(ql, il, (ffl, fbl), accl, nl), sts, unroll=unroll(nl))

    def readout(acc, g_ref, o_ref):
        o = _rms(acc[...]) * ng_ref[...]
        g = g_ref[0]
        o_ref[0] = (o * (g * jax.nn.sigmoid(g))).astype(BF16)

    readout(accl, gl, ol_ref)
    if with_ctx_out:
        readout(accc, gc, oc_ref)


def _hgrn(hg_l, hg_c, lb, norm_g, with_ctx_out):
    B, T, _ = hg_l.shape
    Tc = hg_c.shape[1]
    nh = A_HEADS

    def col(n, t):
        return [pl.BlockSpec((1, t, LANE), (lambda b, h, j=j: (b, 0, j * nh + h))) for j in range(n)]

    in_specs = col(5, T) + col(5, Tc) + [pl.BlockSpec((2, LANE), lambda b, h: (0, h)),
                                         pl.BlockSpec((1, LANE), lambda b, h: (0, h))]
    out_specs = [pl.BlockSpec((1, T, LANE), lambda b, h: (b, 0, h))]
    out_shape = [jax.ShapeDtypeStruct((B, T, nh * LANE), BF16)]
    if with_ctx_out:
        out_specs.append(pl.BlockSpec((1, Tc, LANE), lambda b, h: (b, 0, h)))
        out_shape.append(jax.ShapeDtypeStruct((B, Tc, nh * LANE), BF16))
    res = pl.pallas_call(
        functools.partial(_hgrn_kernel, with_ctx_out=with_ctx_out),
        grid=(B, nh),
        in_specs=in_specs,
        out_specs=out_specs,
        out_shape=out_shape,
        scratch_shapes=[pltpu.VMEM((T, LANE), F32), pltpu.VMEM((Tc, LANE), F32),
                        pltpu.VMEM((2, HGRN_CHUNK, LANE), F32), pltpu.VMEM((2, HGRN_CHUNK, LANE), F32),
                        pltpu.VMEM((2, len(_hgrn_levels(HGRN_CHUNK)), HGRN_CHUNK, HGRN_CHUNK), F32)],
        compiler_params=_params("parallel", "parallel"),
        name="hgrn2",
    )(*([hg_l] * 5), *([hg_c] * 5), lb, norm_g.reshape(1, -1))
    return (res[0], res[1]) if with_ctx_out else (res[0], None)


ATTN_KEY_CHUNK = 1024


def _key_chunks(kv):
    off = 0
    for k_ref, v_ref in kv:
        tk = k_ref.shape[1]
        kc = min(tk, ATTN_KEY_CHUNK)
        for c in range(tk // kc):
            yield off, kc, k_ref, v_ref, c
            off += kc


def _score_phase(q, kv, kcols, s_scr):
    mx = None
    for off, kc, k_ref, _, c in _key_chunks(kv):
        s = _dot_nt(k_ref[0, c * kc:(c + 1) * kc, kcols], q)
        s_scr[off:off + kc, :] = s
        cm = s.max(axis=0, keepdims=True)
        mx = cm if mx is None else jnp.maximum(mx, cm)
        yield None
    yield mx


def _pv_phase(m, kv, vrows, s_scr):
    lv = acc = None
    for off, kc, _, v_ref, c in _key_chunks(kv):
        p = jnp.exp2(s_scr[off:off + kc, :] - m)
        ps = p.sum(axis=0, keepdims=True)
        lv = ps if lv is None else lv + ps
        pv = _dot(v_ref[0, vrows, c * kc:(c + 1) * kc], p.astype(BF16))
        acc = pv if acc is None else acc + pv
        yield None
    yield acc, lv


def _interleave(*gens):
    last = [None] * len(gens)
    live = list(range(len(gens)))
    while live:
        for i in list(live):
            try:
                v = next(gens[i])
                if v is not None:
                    last[i] = v
            except StopIteration:
                live.remove(i)
    return last


ATTN_SUB_Q = 512


def _softmax_pv_items(items, kv, scrs):
    out = []
    m_prev = None
    for i, (q, kc, vr) in enumerate(items):
        score = _score_phase(q, kv, kc, scrs[i % 2])
        if i == 0:
            m_prev, = _interleave(score)
        else:
            m_prev, res = _interleave(score, _pv_phase(m_prev, kv, items[i - 1][2], scrs[(i - 1) % 2]))
            out.append(res)
    res, = _interleave(_pv_phase(m_prev, kv, items[-1][2], scrs[(len(items) - 1) % 2]))
    out.append(res)
    return out


def _diff_attn_kernel(*refs, n_kv, lam_init):
    q_ref, lam_ref, ng_ref = refs[0], refs[1], refs[2]
    kv = [(refs[3 + 2 * i], refs[4 + 2 * i]) for i in range(n_kv)]
    o_ref, s1_scr, s2_scr = refs[3 + 2 * n_kv:]
    lp = lam_ref[...]
    lam = (jnp.exp(jnp.sum(lp[0:1] * lp[1:2], axis=-1, keepdims=True))
           - jnp.exp(jnp.sum(lp[2:3] * lp[3:4], axis=-1, keepdims=True)) + lam_init)
    sub = s1_scr.shape[1]
    full = slice(None)
    items = []
    for r in range(q_ref.shape[1] // sub):
        q = q_ref[0, r * sub:(r + 1) * sub, :]
        lane = lax.broadcasted_iota(jnp.int32, q.shape, 1)
        zero = jnp.zeros_like(q)
        items += [(jnp.where(lane < B_HD, q, zero), full, full), (jnp.where(lane < B_HD, zero, q), full, full)]
    res = _softmax_pv_items(items, kv, (s1_scr, s2_scr))
    for r in range(q_ref.shape[1] // sub):
        (a1, l1), (a2, l2) = res[2 * r], res[2 * r + 1]
        o = jnp.transpose(a1 * (1.0 / l1) - a2 * (lam / l2))
        o_ref[0, r * sub:(r + 1) * sub, :] = (_rms(o) * ng_ref[...] * (1.0 - lam_init)).astype(BF16)


def _diff_attn(q_src, kvs, lam_params, norm_g, lam_init, tq):
    B, Tq, _ = q_src.shape
    nh = B_HEADS
    in_specs = [pl.BlockSpec((1, tq, LANE), lambda b, h, i: (b, i, h)),
                pl.BlockSpec((4, B_HD), lambda b, h, i: (0, 0)),
                pl.BlockSpec((1, LANE), lambda b, h, i: (0, h))]
    args = [q_src, lam_params, norm_g.reshape(1, -1)]
    n_keys = 0
    for k_arr, v_arr in kvs:
        Tk = k_arr.shape[1]
        n_keys += Tk
        in_specs.append(pl.BlockSpec((1, Tk, LANE), lambda b, h, i: (b, 0, nh + h)))
        in_specs.append(pl.BlockSpec((1, LANE, Tk), lambda b, h, i: (b, h, 0)))
        args += [k_arr, v_arr]
    return pl.pallas_call(
        functools.partial(_diff_attn_kernel, n_kv=len(kvs), lam_init=lam_init),
        grid=(B, nh, Tq // tq),
        in_specs=in_specs,
        out_specs=pl.BlockSpec((1, tq, LANE), lambda b, h, i: (b, i, h)),
        out_shape=jax.ShapeDtypeStruct((B, Tq, nh * LANE), BF16),
        scratch_shapes=[pltpu.VMEM((n_keys, min(tq, ATTN_SUB_Q)), F32)] * 2,
        compiler_params=_params("parallel", "parallel", "parallel"),
        name="diff_attn",
    )(*args)


def _mla_attn_kernel(*refs, n_kv):
    q_ref = refs[0]
    kv = [(refs[1 + 2 * i], refs[2 + 2 * i]) for i in range(n_kv)]
    o_ref, s1_scr, s2_scr = refs[1 + 2 * n_kv:]
    sub = s1_scr.shape[1]
    c0, c1 = slice(0, LANE), slice(LANE, 2 * LANE)
    v0, v1 = slice(0, C_V), slice(C_V, 2 * C_V)
    items = []
    for r in range(q_ref.shape[1] // sub):
        rows = slice(r * sub, (r + 1) * sub)
        items += [(q_ref[0, rows, c0], c0, v0), (q_ref[0, rows, c1], c1, v1)]
    res = _softmax_pv_items(items, kv, (s1_scr, s2_scr))
    for r in range(q_ref.shape[1] // sub):
        (a0, l0), (a1, l1) = res[2 * r], res[2 * r + 1]
        ot = jnp.concatenate([a0 * (1.0 / l0), a1 * (1.0 / l1)], axis=0)
        o_ref[0, r * sub:(r + 1) * sub, :] = jnp.transpose(ot).astype(BF16)


def _mla_attn(q, kvs, tq):
    B, Tq, _ = q.shape
    npair = C_HEADS // 2
    in_specs = [pl.BlockSpec((1, tq, 2 * LANE), lambda b, h, i: (b, i, h))]
    args = [q]
    n_keys = 0
    for k_arr, v_arr in kvs:
        Tk = k_arr.shape[1]
        n_keys += Tk
        in_specs.append(pl.BlockSpec((1, Tk, 2 * LANE), lambda b, h, i: (b, 0, h)))
        in_specs.append(pl.BlockSpec((1, LANE, Tk), lambda b, h, i: (b, h, 0)))
        args += [k_arr, v_arr]
    return pl.pallas_call(
        functools.partial(_mla_attn_kernel, n_kv=len(kvs)),
        grid=(B, npair, Tq // tq),
        in_specs=in_specs,
        out_specs=pl.BlockSpec((1, tq, LANE), lambda b, h, i: (b, i, h)),
        out_shape=jax.ShapeDtypeStruct((B, Tq, C_HEADS * C_V), BF16),
        scratch_shapes=[pltpu.VMEM((n_keys, min(tq, ATTN_SUB_Q)), F32)] * 2,
        compiler_params=_params("parallel", "parallel", "parallel"),
        name="mla_attn",
    )(*args)


def _mla_proj_kernel(x_ref, gq_ref, gkv_ref, wq_ref, wkv_ref, cq_ref, saq_ref, sbq_ref, ck_ref, sak_ref,
                     sbk_ref, q_ref, k_ref, v_ref):
    x = x_ref[0]
    xkv = x[:, :MLA_KV_IN]
    lane = lax.broadcasted_iota(jnp.int32, xkv.shape, 1)
    lat = jnp.where(lane < C_KV_LORA, xkv, 0.0)
    ms = jnp.sum(lat * lat, axis=-1, keepdims=True) * (1.0 / C_KV_LORA)
    hk = jnp.where(lane < C_KV_LORA, lat * lax.rsqrt(ms + EPS) * gkv_ref[...], xkv).astype(BF16)
    hq = (_rms(x[:, MLA_KV_IN:]) * gq_ref[...]).astype(BF16)
    nk = C_HEADS * LANE
    aq = _dot(hq, wq_ref[...])
    akv = _dot(hk, wkv_ref[...])
    for g in range(C_HEADS):
        sl = slice(g * LANE, (g + 1) * LANE)
        q_ref[0, :, sl] = _rope_apply(aq[:, sl], cq_ref[...], saq_ref[...], sbq_ref[...], C_ROPE // 2).astype(BF16)
        k_ref[0, :, sl] = _rope_apply(akv[:, sl], ck_ref[...], sak_ref[...], sbk_ref[...], C_ROPE // 2).astype(BF16)
    v_ref[0] = jnp.transpose(akv[:, nk:]).astype(BF16)


def _mla_proj(mla, gq, gkv, wq, wkv, tabs_q, tabs_k, tm):
    Bx, Tx, _ = mla.shape
    row = lambda b, i: (b, i, 0)
    const = lambda b, i: (0, 0)
    tab = pl.BlockSpec((tm, LANE), lambda b, i: (i, 0))
    nk = C_HEADS * LANE
    outs = [(nk, BF16, False), (nk, BF16, False), (C_HEADS * C_V, BF16, True)]
    return pl.pallas_call(
        _mla_proj_kernel,
        grid=(Bx, Tx // tm),
        in_specs=[pl.BlockSpec((1, tm, MLA_W), row),
                  pl.BlockSpec((1, C_Q_LORA), const),
                  pl.BlockSpec((1, MLA_KV_IN), const),
                  _resident(wq.shape, const),
                  _resident(wkv.shape, const),
                  tab, tab, tab, tab, tab, tab],
        out_specs=[_out_spec(tm, w, t) for w, _, t in outs],
        out_shape=[_out_shape(Bx, Tx, w, dt, t) for w, dt, t in outs],
        compiler_params=_params("parallel", "parallel"),
        name="mla_proj",
    )(mla, gq, gkv, wq, wkv, *tabs_q, *tabs_k)


def _merge_kernel(a_ref, b_ref, c_ref, gate_ref, wb_ref, wo_ref, x_ref, mod_ref, gf_ref, rw_ref,
                  o_ref, h_ref, aff_ref):
    D = x_ref.shape[2]
    mod = mod_ref[0]
    z = None
    for n, br in enumerate((a_ref, b_ref, c_ref)):
        y = _dot(br[0], wb_ref[n])
        t = gate_ref[0, :, n * D:(n + 1) * D].astype(F32) * y
        z = t if z is None else z + t
    out = _dot(z.astype(BF16), wo_ref[...])
    x = x_ref[0] + mod[2:3] * out
    o_ref[0] = x
    h = ((_rms(x) * gf_ref[...]) * (1.0 + mod[4:5]) + mod[3:4]).astype(BF16)
    h_ref[0] = h
    logits = _dot(h, rw_ref[...])
    lane = lax.broadcasted_iota(jnp.int32, logits.shape, 1)
    lg = jnp.where(lane < N_EXPERTS, logits, -jnp.inf)
    e = jnp.exp(lg - lg.max(axis=-1, keepdims=True))
    aff_ref[0] = e / e.sum(axis=-1, keepdims=True)


def _merge(a, b, c, gates, wb, wo, x, mods, mod_row, norm_ffn_g, rw, tm):
    Bx, Tx, D = x.shape
    row = lambda bb, i: (bb, i, 0)
    br = pl.BlockSpec((1, tm, BRANCH_W), row)
    outs = [(D, F32), (D, BF16), (LANE, F32)]
    return pl.pallas_call(
        _merge_kernel,
        grid=(Bx, Tx // tm),
        in_specs=[br, br, br,
                  pl.BlockSpec((1, tm, GATE_W), row),
                  _resident(wb.shape, lambda bb, i: (0, 0, 0)),
                  _resident(wo.shape, lambda bb, i: (0, 0)),
                  pl.BlockSpec((1, tm, D), row),
                  pl.BlockSpec((1, 6, D), lambda bb, i: (mod_row(bb), 0, 0)),
                  pl.BlockSpec((1, D), lambda bb, i: (0, 0)),
                  pl.BlockSpec((D, LANE), lambda bb, i: (0, 0))],
        out_specs=[pl.BlockSpec((1, tm, w), row) for w, _ in outs],
        out_shape=[jax.ShapeDtypeStruct((Bx, Tx, w), dt) for w, dt in outs],
        compiler_params=_params("parallel", "parallel"),
        name="merge_out",
    )(a, b, c, gates, wb, wo, x, mods, norm_ffn_g.reshape(1, D), rw)


def _prefix_count(mask, blk):
    T = mask.shape[1]
    r = lax.broadcasted_iota(jnp.int32, (blk, blk), 0)
    c = lax.broadcasted_iota(jnp.int32, (blk, blk), 1)
    tri = jnp.where(r < c, 1.0, 0.0).astype(BF16)
    parts = []
    carry = jnp.zeros((mask.shape[0], 1), F32)
    for i in range(T // blk):
        mb = mask[:, i * blk:(i + 1) * blk]
        parts.append(_dot(mb.astype(BF16), tri) + carry)
        carry = carry + jnp.sum(mb, axis=1, keepdims=True)
    return jnp.concatenate(parts, axis=1) if len(parts) > 1 else parts[0]


def _select_kernel(aff_ref, pos_ref, post_ref, afft_ref, *, cap):
    T = aff_ref.shape[1]
    aff = jnp.transpose(aff_ref[0])[:N_EXPERTS]
    bits = pltpu.bitcast(aff, jnp.int32)

    def search(i, thr):
        cand = thr | (jnp.int32(1) << (30 - i))
        cnt = jnp.sum(jnp.where(bits >= cand, 1.0, 0.0), axis=1, keepdims=True)
        return jnp.where(cnt >= cap, cand, thr)

    thr = lax.fori_loop(0, 31, search, jnp.zeros((N_EXPERTS, 1), jnp.int32))
    gt = jnp.where(bits > thr, 1.0, 0.0)
    eq = jnp.where(bits == thr, 1.0, 0.0)
    need = cap - jnp.sum(gt, axis=1, keepdims=True)
    blk = min(T, 256)
    sel = gt + eq * jnp.where(_prefix_count(eq, blk) < need, 1.0, 0.0)
    post = jnp.where(sel > 0.0, _prefix_count(sel, blk), -1.0)
    post_ref[0] = post
    afft_ref[0] = aff
    pad = jnp.full((LANE - N_EXPERTS, T), -1.0, F32)
    pos_ref[0] = jnp.transpose(jnp.concatenate([post, pad], axis=0))


def _select(aff, cap):
    Bx, Tx, _ = aff.shape
    blk = pl.BlockSpec((1, Tx, LANE), lambda b: (b, 0, 0))
    blk_t = pl.BlockSpec((1, N_EXPERTS, Tx), lambda b: (b, 0, 0))
    return pl.pallas_call(
        functools.partial(_select_kernel, cap=cap),
        grid=(Bx,),
        in_specs=[blk],
        out_specs=[blk, blk_t, blk_t],
        out_shape=[jax.ShapeDtypeStruct((Bx, Tx, LANE), F32),
                   jax.ShapeDtypeStruct((Bx, N_EXPERTS, Tx), F32),
                   jax.ShapeDtypeStruct((Bx, N_EXPERTS, Tx), F32)],
        compiler_params=_params("parallel"),
        name="moe_select",
    )(aff)


DISPATCH_GROUP = 4


def _dispatch_kernel(h_ref, post_ref, afft_ref, xs_ref, gs_ref, *, cap):
    g = pl.program_id(1)
    T = h_ref.shape[1]
    slot = lax.broadcasted_iota(jnp.int32, (cap, T), 0).astype(F32)
    for j in range(DISPATCH_GROUP):
        e = g * DISPATCH_GROUP + j
        hit = post_ref[0, pl.ds(e, 1), :] == slot
        xs_ref[j] = _dot(jnp.where(hit, 1.0, 0.0).astype(BF16), h_ref[0]).astype(BF16)
        gs_ref[j] = jnp.sum(jnp.where(hit, afft_ref[0, pl.ds(e, 1), :], 0.0), axis=1, keepdims=True)


def _dispatch(h, post, afft, cap):
    Bx, Tx, D = h.shape
    tok = pl.BlockSpec((1, N_EXPERTS, Tx), lambda b, g: (b, 0, 0))
    return pl.pallas_call(
        functools.partial(_dispatch_kernel, cap=cap),
        grid=(Bx, N_EXPERTS // DISPATCH_GROUP),
        in_specs=[pl.BlockSpec((1, Tx, D), lambda b, g: (b, 0, 0)), tok, tok],
        out_specs=[pl.BlockSpec((DISPATCH_GROUP, cap, D), lambda b, g: (g, b, 0)),
                   pl.BlockSpec((DISPATCH_GROUP, cap, 1), lambda b, g: (g, b, 0))],
        out_shape=[jax.ShapeDtypeStruct((N_EXPERTS, Bx * cap, D), BF16),
                   jax.ShapeDtypeStruct((N_EXPERTS, Bx * cap, 1), F32)],
        compiler_params=_params("parallel", "parallel"),
        name="moe_dispatch",
    )(h, post, afft)


def _expert_kernel(x_ref, gs_ref, wg_ref, wu_ref, wd_ref, y_ref):
    x = x_ref[0]
    ff = wg_ref.shape[2]
    fc = 512
    acc = None
    for f in range(ff // fc):
        a = _dot(x, wg_ref[0, :, f * fc:(f + 1) * fc])
        u = _dot(x, wu_ref[0, :, f * fc:(f + 1) * fc])
        hm = (a * jax.nn.sigmoid(a) * u).astype(BF16)
        c = _dot(hm, wd_ref[0, f * fc:(f + 1) * fc, :])
        acc = c if acc is None else acc + c
    y_ref[0] = (acc * gs_ref[0]).astype(BF16)


def _experts(xs, gs, wg, wu, wd, tm):
    E, M, D = xs.shape
    FF = wg.shape[2]
    return pl.pallas_call(
        _expert_kernel,
        grid=(E, M // tm),
        in_specs=[pl.BlockSpec((1, tm, D), lambda e, i: (e, i, 0)),
                  pl.BlockSpec((1, tm, 1), lambda e, i: (e, i, 0)),
                  pl.BlockSpec((1, D, FF), lambda e, i: (e, 0, 0)),
                  pl.BlockSpec((1, D, FF), lambda e, i: (e, 0, 0)),
                  pl.BlockSpec((1, FF, D), lambda e, i: (e, 0, 0))],
        out_specs=pl.BlockSpec((1, tm, D), lambda e, i: (e, i, 0)),
        out_shape=jax.ShapeDtypeStruct((E, M, D), BF16),
        compiler_params=_params("parallel", "parallel"),
        name="moe_experts",
    )(xs, gs, wg, wu, wd)


def _combine_kernel(y_ref, pos_ref, x_ref, mod_ref, *rest, cap, group, final_norm):
    o_ref = rest[-1]
    width = group * cap
    pos = pos_ref[0].astype(BF16)
    src = lax.broadcasted_iota(jnp.int32, (LANE, width), 0)
    dst = lax.broadcasted_iota(jnp.int32, (LANE, width), 1)
    slot = (lax.broadcasted_iota(jnp.int32, (1, width), 1) % cap).astype(F32)
    out = None
    for g in range(N_EXPERTS // group):
        spread = jnp.where(dst // cap + g * group == src, 1.0, 0.0).astype(BF16)
        onehot = jnp.where(_dot(pos, spread) == slot, 1.0, 0.0).astype(BF16)
        c = _dot(onehot, y_ref[g * group:(g + 1) * group].reshape(width, y_ref.shape[2]))
        out = c if out is None else out + c
    x = x_ref[0] + mod_ref[0, 5:6] * out
    if final_norm:
        x = _rms(x) * rest[0][...]
    o_ref[0] = x


def _combine(y, pos, x, mods, mod_row, cap, tm, final_g=None):
    Bx, Tx, D = x.shape
    row = lambda b, i: (b, i, 0)
    group = max(4, LANE // cap)
    in_specs = [pl.BlockSpec((N_EXPERTS, cap, D), lambda b, i: (0, b, 0)),
                pl.BlockSpec((1, tm, LANE), row),
                pl.BlockSpec((1, tm, D), row),
                pl.BlockSpec((1, 6, D), lambda b, i: (mod_row(b), 0, 0))]
    args = [y, pos, x, mods]
    if final_g is not None:
        in_specs.append(pl.BlockSpec((1, D), lambda b, i: (0, 0)))
        args.append(final_g.reshape(1, D))
    return pl.pallas_call(
        functools.partial(_combine_kernel, cap=cap, group=group, final_norm=final_g is not None),
        grid=(Bx, Tx // tm),
        in_specs=in_specs,
        out_specs=pl.BlockSpec((1, tm, D), row),
        out_shape=jax.ShapeDtypeStruct((Bx, Tx, D), F32),
        compiler_params=_params("parallel", "parallel"),
        name="moe_combine",
    )(*args)


def _moe(x, h, aff, mods, mod_row, wg, wu, wd, tm, final_g=None):
    Bx, Tx, D = x.shape
    cap = CAPACITY_FACTOR * Tx // N_EXPERTS
    pos, post, afft = _select(aff, cap)
    xs, gs = _dispatch(h, post, afft, cap)
    y = _experts(xs, gs, wg, wu, wd, min(Bx * cap, 512))
    return _combine(y, pos, x, mods, mod_row, cap, tm, final_g)


def _rope_tables(T, rot_dim, lane_lo, period, scale):
    rows = T // GRID_W
    row = jnp.repeat(jnp.arange(rows, dtype=F32), GRID_W)
    colp = jnp.tile(jnp.arange(GRID_W, dtype=F32), rows)
    n_freq = rot_dim // 4
    inv_freq = ROPE_BASE ** (-jnp.arange(n_freq, dtype=F32) / n_freq)
    ang = jnp.concatenate([row[:, None] * inv_freq, colp[:, None] * inv_freq], axis=-1)
    cos_h, sin_h = jnp.cos(ang), jnp.sin(ang)
    half = rot_dim // 2
    cos_g = jnp.ones((T, period), F32)
    sa_g = jnp.zeros((T, period), F32)
    sb_g = jnp.zeros((T, period), F32)
    cos_g = cos_g.at[:, lane_lo:lane_lo + rot_dim].set(jnp.concatenate([cos_h, cos_h], axis=-1))
    sa_g = sa_g.at[:, lane_lo:lane_lo + half].set(-sin_h)
    sb_g = sb_g.at[:, lane_lo + half:lane_lo + rot_dim].set(sin_h)
    rep = LANE // period
    return tuple(jnp.tile(t, (1, rep)) * scale for t in (cos_g, sa_g, sb_g))


def _identity_tables(T, scale):
    return (jnp.full((T, LANE), scale, F32), jnp.zeros((T, LANE), F32), jnp.zeros((T, LANE), F32))


def _prep_w_in(w):
    D = w.shape[0]
    o_diff = HGRN_W
    o_cq = o_diff + 1536
    o_ckv = o_cq + C_Q_LORA
    o_kr = o_ckv + C_KV_LORA
    o_gate = o_kr + C_ROPE
    pad = jnp.zeros((D, MLA_KV_IN - C_KV_LORA - C_ROPE), w.dtype)
    return jnp.concatenate([w[:, :o_cq], w[:, o_ckv:o_kr], w[:, o_kr:o_gate], pad, w[:, o_cq:o_ckv],
                            w[:, o_gate:]], axis=1).astype(BF16)


def _prep_mla_w(w_uq, w_ukv):
    wq = w_uq.reshape(C_Q_LORA, C_HEADS, C_NOPE + C_ROPE)
    wq = jnp.pad(wq, ((0, 0), (0, 0), (0, LANE - C_NOPE - C_ROPE))).reshape(C_Q_LORA, C_HEADS * LANE)
    wkv = w_ukv.reshape(C_KV_LORA, C_HEADS, C_NOPE + C_V)
    wk = jnp.pad(wkv[:, :, :C_NOPE], ((0, 0), (0, 0), (0, LANE - C_NOPE)))
    place = jnp.zeros((C_ROPE, C_HEADS, LANE), F32).at[:, :, C_NOPE:C_NOPE + C_ROPE].set(
        jnp.broadcast_to(jnp.eye(C_ROPE, dtype=F32)[:, None, :], (C_ROPE, C_HEADS, C_ROPE)))
    wk = jnp.concatenate([wk, place, jnp.zeros((MLA_KV_IN - C_KV_LORA - C_ROPE, C_HEADS, LANE), F32)], axis=0)
    wv = jnp.pad(wkv[:, :, C_NOPE:].reshape(C_KV_LORA, C_HEADS * C_V), ((0, MLA_KV_IN - C_KV_LORA), (0, 0)))
    wkv_comb = jnp.concatenate([wk.reshape(MLA_KV_IN, C_HEADS * LANE), wv], axis=1)
    return wq.astype(BF16), wkv_comb.astype(BF16)


def kernel(x, c, ctx, c_ctx, ada_w, ada_b, norm_mix_g, norm_ffn_g, w_in, hgrn_lb, hgrn_norm_g, diff_lambda,
           diff_norm_g, mla_q_norm_g, mla_w_uq, mla_kv_norm_g, mla_w_ukv, w_branch, w_out, router_w,
           exp_w_gate, exp_w_up, exp_w_down, final_norm_g):
    B, T, D = x.shape
    Tc = ctx.shape[1]
    depth = ada_w.shape[0]
    tm_l = min(T, 512)
    tm_c = min(Tc, 256)
    tq = min(Tc, 256)
    tq_l = min(T, 2048)

    rows = ((B + 1 + SUBLANE - 1) // SUBLANE) * SUBLANE
    vecs = jnp.concatenate([c, c_ctx[None], jnp.zeros((rows - B - 1, D), F32)], axis=0)
    mods_all = _modulation(vecs, ada_w, ada_b).reshape(depth, rows, 6, D)
    lat_row = lambda b: b
    ctx_row = lambda b: B

    p = jax.nn.softmax(hgrn_lb.astype(F32), axis=0)
    lower_bounds = jnp.clip(jnp.cumsum(p, axis=0) - p[0], 0.0, LB_MAX)

    d_scale = B_HD ** -0.5 * math.log2(math.e)
    c_scale = (C_NOPE + C_ROPE) ** -0.5 * math.log2(math.e)
    dq_l = _rope_tables(T, B_HD, 0, B_HD, d_scale)
    dk_l = _rope_tables(T, B_HD, 0, B_HD, 1.0)
    dq_c, dk_c = _identity_tables(Tc, d_scale), _identity_tables(Tc, 1.0)
    mq_l = _rope_tables(T, C_ROPE, C_NOPE, LANE, c_scale)
    mk_l = _rope_tables(T, C_ROPE, C_NOPE, LANE, 1.0)
    mq_c, mk_c = _identity_tables(Tc, c_scale), _identity_tables(Tc, 1.0)

    xl, xc = x, ctx
    for layer in range(depth):
        with_ctx = layer < depth - 1
        mods = mods_all[layer]
        w_perm = _prep_w_in(w_in[layer])
        wq, wkv = _prep_mla_w(mla_w_uq[layer], mla_w_ukv[layer])
        wb = w_branch[layer].astype(BF16)
        wo = w_out[layer].astype(BF16)
        rw = jnp.pad(router_w[layer], ((0, 0), (0, LANE - N_EXPERTS))).astype(BF16)
        wg = exp_w_gate[layer].astype(BF16)
        wu = exp_w_up[layer].astype(BF16)
        wd = exp_w_down[layer].astype(BF16)
        lam_init = 0.8 - 0.6 * math.exp(-0.3 * layer)
        gq = mla_q_norm_g[layer].reshape(1, -1)
        gkv = jnp.pad(mla_kv_norm_g[layer], (0, MLA_KV_IN - C_KV_LORA)).reshape(1, -1)

        hg_l, dqk_l, dv_l, ml_l, gt_l = _layer_in(xl, norm_mix_g[layer], mods, lat_row, w_perm, dq_l, dk_l, tm_l)
        hg_c, dqk_c, dv_c, ml_c, gt_c = _layer_in(xc, norm_mix_g[layer], mods, ctx_row, w_perm, dq_c, dk_c, tm_c)

        a_l, a_c = _hgrn(hg_l, hg_c, lower_bounds[layer], hgrn_norm_g[layer], with_ctx)

        b_l = _diff_attn(dqk_l, [(dqk_c, dv_c), (dqk_l, dv_l)], diff_lambda[layer], diff_norm_g[layer], lam_init, tq_l)
        mq_lat, mk_lat, mv_lat = _mla_proj(ml_l, gq, gkv, wq, wkv, mq_l, mk_l, tm_l)
        mq_ctx, mk_ctx, mv_ctx = _mla_proj(ml_c, gq, gkv, wq, wkv, mq_c, mk_c, tm_c)
        m_l = _mla_attn(mq_lat, [(mk_ctx, mv_ctx), (mk_lat, mv_lat)], tq_l)

        xl, h_l, aff_l = _merge(a_l, b_l, m_l, gt_l, wb, wo, xl, mods, lat_row, norm_ffn_g[layer], rw, tm_l)
        xl = _moe(xl, h_l, aff_l, mods, lat_row, wg, wu, wd, tm_l, None if with_ctx else final_norm_g)
        if with_ctx:
            b_c = _diff_attn(dqk_c, [(dqk_c, dv_c)], diff_lambda[layer], diff_norm_g[layer], lam_init, tq)
            m_c = _mla_attn(mq_ctx, [(mk_ctx, mv_ctx)], tq)
            xc, h_c, aff_c = _merge(a_c, b_c, m_c, gt_c, wb, wo, xc, mods, ctx_row, norm_ffn_g[layer], rw, tm_c)
            xc = _moe(xc, h_c, aff_c, mods, ctx_row, wg, wu, wd, tm_c)
    return xl
```

```python
import functools
import math

import jax
import jax.numpy as jnp
from jax import lax
from jax.experimental import pallas as pl
from jax.experimental.pallas import tpu as pltpu

F32 = jnp.float32
BF16 = jnp.bfloat16

EPS = 1e-6
ROPE_BASE = 10000.0
GRID_W = 64
F_MIN = 1e-30
LB_MAX = 1.0 - 1e-6

A_HEADS, A_DK = 4, 128
B_HEADS, B_HD = 4, 64
C_HEADS, C_NOPE, C_ROPE, C_V = 8, 64, 32, 64
C_Q_LORA, C_KV_LORA = 384, 256
N_BRANCH, BRANCH_W = 3, 512
N_EXPERTS = 16
CAPACITY_FACTOR = 2

LANE = 128
SUBLANE = 8
VMEM_LIMIT = 56 * 1024 * 1024

HGRN_W = 5 * 512
MLA_W = 768
MLA_KV_IN = 384
GATE_W = N_BRANCH * 1024
HGRN_CHUNK = 128


def _params(*sem):
    return pltpu.CompilerParams(dimension_semantics=sem, vmem_limit_bytes=VMEM_LIMIT)


def _resident(shape, index_map):
    return pl.BlockSpec(shape, index_map, pipeline_mode=pl.Buffered(1))


def _rms(x, eps=EPS):
    return x * lax.rsqrt(jnp.mean(x * x, axis=-1, keepdims=True) + eps)


def _dot(a, b):
    return jnp.dot(a, b, preferred_element_type=F32)


def _dot_nt(a, b):
    return lax.dot_general(a, b, (((1,), (1,)), ((), ())), preferred_element_type=F32)


def _dot_tn(a, b):
    return lax.dot_general(a, b, (((0,), (0,)), ((), ())), preferred_element_type=F32)


def _mods_kernel(v_ref, w_ref, b_ref, o_ref):
    v = v_ref[...]
    s = (v * jax.nn.sigmoid(v)).astype(BF16)
    o_ref[0] = _dot(s, w_ref[0].astype(BF16)) + b_ref[0]


def _modulation(vecs, ada_w, ada_b):
    L, D, N = ada_w.shape
    R = vecs.shape[0]
    tn = 1536
    return pl.pallas_call(
        _mods_kernel,
        grid=(L, N // tn),
        in_specs=[pl.BlockSpec((R, D), lambda l, j: (0, 0)),
                  pl.BlockSpec((1, D, tn), lambda l, j: (l, 0, j)),
                  pl.BlockSpec((1, 1, tn), lambda l, j: (l, 0, j))],
        out_specs=pl.BlockSpec((1, R, tn), lambda l, j: (l, 0, j)),
        out_shape=jax.ShapeDtypeStruct((L, R, N), F32),
        compiler_params=_params("parallel", "parallel"),
        name="adaln_mods",
    )(vecs, ada_w, ada_b.reshape(L, 1, N))


def _rope_apply(x, cos, sin_a, sin_b, half):
    return (x * cos + pltpu.roll(x, LANE - half, axis=1) * sin_a + pltpu.roll(x, half, axis=1) * sin_b)


def _layer_in_kernel(x_ref, g_ref, mod_ref, w_ref, cq_ref, saq_ref, sbq_ref, ck_ref, sak_ref, sbk_ref,
                     hg_ref, dqk_ref, dv_ref, mla_ref, gate_ref, h_scr):
    mod = mod_ref[0]
    half = x_ref.shape[1] // 2
    c0 = 0
    for r in range(2):
        rows = slice(r * half, (r + 1) * half)
        h_scr[rows] = ((_rms(x_ref[0, rows]) * g_ref[...]) * (1.0 + mod[1:2]) + mod[0:1]).astype(BF16)
        hg_ref[0, rows] = _dot(h_scr[rows], w_ref[:, c0:c0 + HGRN_W])
    h = h_scr[...]
    c0 += HGRN_W
    for j in range(2):
        acc = _dot(h, w_ref[:, c0:c0 + 512])
        cos, sa, sb = (cq_ref, saq_ref, sbq_ref) if j == 0 else (ck_ref, sak_ref, sbk_ref)
        for g in range(4):
            xg = acc[:, g * LANE:(g + 1) * LANE]
            dqk_ref[0, :, j * 512 + g * LANE:j * 512 + (g + 1) * LANE] = _rope_apply(
                xg, cos[...], sa[...], sb[...], B_HD // 2).astype(BF16)
        c0 += 512
    dv_ref[0] = jnp.transpose(_dot(h, w_ref[:, c0:c0 + 512])).astype(BF16)
    c0 += 512
    mla_ref[0] = _dot(h, w_ref[:, c0:c0 + MLA_W])
    c0 += MLA_W
    for j in range(GATE_W // 1024):
        gate_ref[0, :, j * 1024:(j + 1) * 1024] = jax.nn.sigmoid(_dot(h, w_ref[:, c0:c0 + 1024])).astype(BF16)
        c0 += 1024


def _out_spec(tm, width, transposed):
    if transposed:
        return pl.BlockSpec((1, width, tm), lambda b, i: (b, 0, i))
    return pl.BlockSpec((1, tm, width), lambda b, i: (b, i, 0))


def _out_shape(bx, tx, width, dtype, transposed):
    return jax.ShapeDtypeStruct((bx, width, tx) if transposed else (bx, tx, width), dtype)


def _layer_in(x, norm_g, mods, mod_row, w_perm, tabs_q, tabs_k, tm):
    Bx, Tx, D = x.shape
    NW = w_perm.shape[1]
    row = lambda b, i: (b, i, 0)
    tab = pl.BlockSpec((tm, LANE), lambda b, i: (i, 0))
    outs = [(HGRN_W, F32, False), (1024, BF16, False), (512, BF16, True), (MLA_W, F32, False),
            (GATE_W, BF16, False)]
    return pl.pallas_call(
        _layer_in_kernel,
        grid=(Bx, Tx // tm),
        in_specs=[pl.BlockSpec((1, tm, D), row),
                  pl.BlockSpec((1, D), lambda b, i: (0, 0)),
                  pl.BlockSpec((1, 6, D), lambda b, i: (mod_row(b), 0, 0)),
                  _resident((D, NW), lambda b, i: (0, 0)),
                  tab, tab, tab, tab, tab, tab],
        out_specs=[_out_spec(tm, w, t) for w, _, t in outs],
        out_shape=[_out_shape(Bx, Tx, w, dt, t) for w, dt, t in outs],
        scratch_shapes=[pltpu.VMEM((tm, D), BF16)],
        compiler_params=_params("parallel", "parallel"),
        name="layer_in",
    )(x, norm_g.reshape(1, D), mods, w_perm, *tabs_q, *tabs_k)


def _cumsum_rows(x, reverse):
    n = x.shape[0]
    s = 1
    while s < n:
        if s < SUBLANE:
            row = lax.broadcasted_iota(jnp.int32, x.shape, 0)
            if reverse:
                sh = jnp.where(row < n - s, pltpu.roll(x, n - s, axis=0), 0.0)
            else:
                sh = jnp.where(row >= s, pltpu.roll(x, s, axis=0), 0.0)
        else:
            z = jnp.zeros((s, x.shape[1]), x.dtype)
            sh = jnp.concatenate([x[s:], z], axis=0) if reverse else jnp.concatenate([z, x[:n - s]], axis=0)
        x = x + sh
        s *= 2
    return x


def _bcast_rows(load_row, rows, reps):
    parts = []
    for r in rows:
        parts += [load_row(r)] * reps
    return jnp.concatenate(parts, axis=0)


def _hgrn_chunk(q, k_in, v, v_row, lf, st, b_scr, k_scr, mask_ref, reverse):
    C = q.shape[0]
    b = _cumsum_rows(lf, reverse)
    b_scr[...] = b
    k_scr[...] = k_in
    b_row = lambda r: jnp.broadcast_to(b_scr[pl.ds(r, 1), :], (SUBLANE, LANE))
    k_row = lambda r: jnp.broadcast_to(k_scr[pl.ds(r, 1), :], (SUBLANE, LANE))
    b_last = b[0:1] if reverse else b[C - 1:C]
    qd = (q * jnp.exp2(b)).astype(BF16)
    o = _dot_nt(qd, st.astype(BF16))
    kd = (k_in * jnp.exp2(b_last - b)).astype(BF16)
    vb = v.astype(BF16)
    st_new = st * jnp.exp2(b_last) + _dot_tn(vb, kd)

    scores = None
    for li, m in enumerate(_hgrn_levels(C)):
        mid = m if reverse else m - 1
        anchor = _bcast_rows(b_row, [i * 2 * m + mid for i in range(C // (2 * m))], 2 * m // SUBLANE)
        e = jnp.exp2(-jnp.abs(b - anchor))
        sc = _dot_nt((q * e).astype(BF16), (k_in * e).astype(BF16)) * mask_ref[li]
        scores = sc if scores is None else scores + sc
    o = o + _dot(scores.astype(BF16), vb)

    t8 = lax.broadcasted_iota(jnp.int32, (C, 1), 0) % SUBLANE
    for j in range(SUBLANE):
        rows = [SUBLANE * i + j for i in range(C // SUBLANE)]
        w = q * jnp.exp2(b - _bcast_rows(b_row, rows, 1)) * _bcast_rows(k_row, rows, 1)
        valid = (t8 <= j) if reverse else (t8 >= j)
        sc = jnp.sum(jnp.where(valid, w, 0.0), axis=-1, keepdims=True)
        o = o + sc * _bcast_rows(v_row, rows, 1)
    return o, st_new


def _hgrn_levels(C):
    out = []
    m = C // 2
    while m >= SUBLANE:
        out.append(m)
        m //= 2
    return out


def _hgrn_masks(C, reverse):
    row = lax.broadcasted_iota(jnp.int32, (C, C), 0)
    col = lax.broadcasted_iota(jnp.int32, (C, C), 1)
    masks = []
    for m in _hgrn_levels(C):
        same = (row // (2 * m)) == (col // (2 * m))
        t_second = (row % (2 * m)) >= m
        s_second = (col % (2 * m)) >= m
        if reverse:
            ok = same & jnp.logical_not(t_second) & s_second
        else:
            ok = same & t_second & jnp.logical_not(s_second)
        masks.append(jnp.where(ok, 1.0, 0.0))
    return masks


def _hgrn_kernel(*refs, with_ctx_out):
    (ql, il, gl, ffl, fbl, qc, ic, gc, ffc, fbc, lb_ref, ng_ref) = refs[:12]
    if with_ctx_out:
        ol_ref, oc_ref, accl, accc, b_scr, k_scr, mask_scr = refs[12:]
    else:
        ol_ref, accl, accc, b_scr, k_scr, mask_scr = refs[12:]
        oc_ref = None
    C = HGRN_CHUNK
    nl = ql.shape[1] // C
    nc = qc.shape[1] // C

    accl[...] = jnp.zeros_like(accl)
    accc[...] = jnp.zeros_like(accc)
    for d in range(2):
        for li, mk in enumerate(_hgrn_masks(C, d == 1)):
            mask_scr[d, li] = mk

    def make_step(qr, ir, frs, acc, n):
        def step(s, sts):
            new = []
            for d in range(2):
                reverse = d == 1
                lbd = lb_ref[d:d + 1, :]
                one_m = 1.0 - lbd
                c = (n - 1 - s) if reverse else s
                r0 = pl.multiple_of(c * C, C)
                sg = jax.nn.sigmoid(frs[d][0, pl.ds(r0, C), :])
                kk = one_m * (1.0 - sg)
                lf = jnp.log2(jnp.maximum(lbd + one_m * sg, F_MIN))
                v_row = lambda r, r0=r0: jnp.broadcast_to(ir[0, pl.ds(r0 + r, 1), :], (SUBLANE, LANE))
                o, st = _hgrn_chunk(qr[0, pl.ds(r0, C), :], kk, ir[0, pl.ds(r0, C), :], v_row, lf, sts[d],
                                    b_scr.at[d], k_scr.at[d], mask_scr.at[d], reverse)
                acc[pl.ds(r0, C), :] += o
                new.append(st)
            return tuple(new)
        return step

    zero = jnp.zeros((LANE, LANE), F32)
    unroll = lambda n: 2 if n % 2 == 0 else 1
    sts = lax.fori_loop(0, nc, make_step(qc, ic, (ffc, fbc), accc, nc), (zero, zero), unroll=unroll(nc))
    lax.fori_loop(0, nl, make_step(ql, il, (ffl, fbl), accl, nl), sts, unroll=unroll(nl))

    def readout(acc, g_ref, o_ref):
        o = _rms(acc[...]) * ng_ref[...]
        g = g_ref[0]
        o_ref[0] = (o * (g * jax.nn.sigmoid(g))).astype(BF16)

    readout(accl, gl, ol_ref)
    if with_ctx_out:
        readout(accc, gc, oc_ref)


def _hgrn(hg_l, hg_c, lb, norm_g, with_ctx_out):
    B, T, _ = hg_l.shape
    Tc = hg_c.shape[1]
    nh = A_HEADS

    def col(n, t):
        return [pl.BlockSpec((1, t, LANE), (lambda b, h, j=j: (b, 0, j * nh + h))) for j in range(n)]

    in_specs = col(5, T) + col(5, Tc) + [pl.BlockSpec((2, LANE), lambda b, h: (0, h)),
                                         pl.BlockSpec((1, LANE), lambda b, h: (0, h))]
    out_specs = [pl.BlockSpec((1, T, LANE), lambda b, h: (b, 0, h))]
    out_shape = [jax.ShapeDtypeStruct((B, T, nh * LANE), BF16)]
    if with_ctx_out:
        out_specs.append(pl.BlockSpec((1, Tc, LANE), lambda b, h: (b, 0, h)))
        out_shape.append(jax.ShapeDtypeStruct((B, Tc, nh * LANE), BF16))
    res = pl.pallas_call(
        functools.partial(_hgrn_kernel, with_ctx_out=with_ctx_out),
        grid=(B, nh),
        in_specs=in_specs,
        out_specs=out_specs,
        out_shape=out_shape,
        scratch_shapes=[pltpu.VMEM((T, LANE), F32), pltpu.VMEM((Tc, LANE), F32),
                        pltpu.VMEM((2, HGRN_CHUNK, LANE), F32), pltpu.VMEM((2, HGRN_CHUNK, LANE), F32),
                        pltpu.VMEM((2, len(_hgrn_levels(HGRN_CHUNK)), HGRN_CHUNK, HGRN_CHUNK), F32)],
        compiler_params=_params("parallel", "parallel"),
        name="hgrn2",
    )(*([hg_l] * 5), *([hg_c] * 5), lb, norm_g.reshape(1, -1))
    return (res[0], res[1]) if with_ctx_out else (res[0], None)


ATTN_KEY_CHUNK = 1024


def _key_chunks(kv):
    off = 0
    for k_ref, v_ref in kv:
        tk = k_ref.shape[1]
        kc = min(tk, ATTN_KEY_CHUNK)
        for c in range(tk // kc):
            yield off, kc, k_ref, v_ref, c
            off += kc


def _score_phase(q, kv, kcols, s_scr):
    mx = None
    for off, kc, k_ref, _, c in _key_chunks(kv):
        s = _dot_nt(k_ref[0, c * kc:(c + 1) * kc, kcols], q)
        s_scr[off:off + kc, :] = s
        cm = s.max(axis=0, keepdims=True)
        mx = cm if mx is None else jnp.maximum(mx, cm)
        yield None
    yield mx


def _pv_phase(m, kv, vrows, s_scr):
    lv = acc = None
    for off, kc, _, v_ref, c in _key_chunks(kv):
        p = jnp.exp2(s_scr[off:off + kc, :] - m)
        ps = p.sum(axis=0, keepdims=True)
        lv = ps if lv is None else lv + ps
        pv = _dot(v_ref[0, vrows, c * kc:(c + 1) * kc], p.astype(BF16))
        acc = pv if acc is None else acc + pv
        yield None
    yield acc, lv


def _interleave(*gens):
    last = [None] * len(gens)
    live = list(range(len(gens)))
    while live:
        for i in list(live):
            try:
                v = next(gens[i])
                if v is not None:
                    last[i] = v
            except StopIteration:
                live.remove(i)
    return last


ATTN_SUB_Q = 512


def _softmax_pv_items(items, kv, scrs):
    out = []
    m_prev = None
    for i, (q, kc, vr) in enumerate(items):
        score = _score_phase(q, kv, kc, scrs[i % 2])
        if i == 0:
            m_prev, = _interleave(score)
        else:
            m_prev, res = _interleave(score, _pv_phase(m_prev, kv, items[i - 1][2], scrs[(i - 1) % 2]))
            out.append(res)
    res, = _interleave(_pv_phase(m_prev, kv, items[-1][2], scrs[(len(items) - 1) % 2]))
    out.append(res)
    return out


def _diff_attn_kernel(*refs, n_kv, lam_init):
    q_ref, lam_ref, ng_ref = refs[0], refs[1], refs[2]
    kv = [(refs[3 + 2 * i], refs[4 + 2 * i]) for i in range(n_kv)]
    o_ref, s1_scr, s2_scr = refs[3 + 2 * n_kv:]
    lp = lam_ref[...]
    lam = (jnp.exp(jnp.sum(lp[0:1] * lp[1:2], axis=-1, keepdims=True))
           - jnp.exp(jnp.sum(lp[2:3] * lp[3:4], axis=-1, keepdims=True)) + lam_init)
    sub = s1_scr.shape[1]
    full = slice(None)
    items = []
    for r in range(q_ref.shape[1] // sub):
        q = q_ref[0, r * sub:(r + 1) * sub, :]
        lane = lax.broadcasted_iota(jnp.int32, q.shape, 1)
        zero = jnp.zeros_like(q)
        items += [(jnp.where(lane < B_HD, q, zero), full, full), (jnp.where(lane < B_HD, zero, q), full, full)]
    res = _softmax_pv_items(items, kv, (s1_scr, s2_scr))
    for r in range(q_ref.shape[1] // sub):
        (a1, l1), (a2, l2) = res[2 * r], res[2 * r + 1]
        o = jnp.transpose(a1 * (1.0 / l1) - a2 * (lam / l2))
        o_ref[0, r * sub:(r + 1) * sub, :] = (_rms(o) * ng_ref[...] * (1.0 - lam_init)).astype(BF16)


def _diff_attn(q_src, kvs, lam_params, norm_g, lam_init, tq):
    B, Tq, _ = q_src.shape
    nh = B_HEADS
    in_specs = [pl.BlockSpec((1, tq, LANE), lambda b, h, i: (b, i, h)),
                pl.BlockSpec((4, B_HD), lambda b, h, i: (0, 0)),
                pl.BlockSpec((1, LANE), lambda b, h, i: (0, h))]
    args = [q_src, lam_params, norm_g.reshape(1, -1)]
    n_keys = 0
    for k_arr, v_arr in kvs:
        Tk = k_arr.shape[1]
        n_keys += Tk
        in_specs.append(pl.BlockSpec((1, Tk, LANE), lambda b, h, i: (b, 0, nh + h)))
        in_specs.append(pl.BlockSpec((1, LANE, Tk), lambda b, h, i: (b, h, 0)))
        args += [k_arr, v_arr]
    return pl.pallas_call(
        functools.partial(_diff_attn_kernel, n_kv=len(kvs), lam_init=lam_init),
        grid=(B, nh, Tq // tq),
        in_specs=in_specs,
        out_specs=pl.BlockSpec((1, tq, LANE), lambda b, h, i: (b, i, h)),
        out_shape=jax.ShapeDtypeStruct((B, Tq, nh * LANE), BF16),
        scratch_shapes=[pltpu.VMEM((n_keys, min(tq, ATTN_SUB_Q)), F32)] * 2,
        compiler_params=_params("parallel", "parallel", "parallel"),
        name="diff_attn",
    )(*args)


def _mla_attn_kernel(*refs, n_kv):
    q_ref = refs[0]
    kv = [(refs[1 + 2 * i], refs[2 + 2 * i]) for i in range(n_kv)]
    o_ref, s1_scr, s2_scr = refs[1 + 2 * n_kv:]
    sub = s1_scr.shape[1]
    c0, c1 = slice(0, LANE), slice(LANE, 2 * LANE)
    v0, v1 = slice(0, C_V), slice(C_V, 2 * C_V)
    items = []
    for r in range(q_ref.shape[1] // sub):
        rows = slice(r * sub, (r + 1) * sub)
        items += [(q_ref[0, rows, c0], c0, v0), (q_ref[0, rows, c1], c1, v1)]
    res = _softmax_pv_items(items, kv, (s1_scr, s2_scr))
    for r in range(q_ref.shape[1] // sub):
        (a0, l0), (a1, l1) = res[2 * r], res[2 * r + 1]
        ot = jnp.concatenate([a0 * (1.0 / l0), a1 * (1.0 / l1)], axis=0)
        o_ref[0, r * sub:(r + 1) * sub, :] = jnp.transpose(ot).astype(BF16)


def _mla_attn(q, kvs, tq):
    B, Tq, _ = q.shape
    npair = C_HEADS // 2
    in_specs = [pl.BlockSpec((1, tq, 2 * LANE), lambda b, h, i: (b, i, h))]
    args = [q]
    n_keys = 0
    for k_arr, v_arr in kvs:
        Tk = k_arr.shape[1]
        n_keys += Tk
        in_specs.append(pl.BlockSpec((1, Tk, 2 * LANE), lambda b, h, i: (b, 0, h)))
        in_specs.append(pl.BlockSpec((1, LANE, Tk), lambda b, h, i: (b, h, 0)))
        args += [k_arr, v_arr]
    return pl.pallas_call(
        functools.partial(_mla_attn_kernel, n_kv=len(kvs)),
        grid=(B, npair, Tq // tq),
        in_specs=in_specs,
        out_specs=pl.BlockSpec((1, tq, LANE), lambda b, h, i: (b, i, h)),
        out_shape=jax.ShapeDtypeStruct((B, Tq, C_HEADS * C_V), BF16),
        scratch_shapes=[pltpu.VMEM((n_keys, min(tq, ATTN_SUB_Q)), F32)] * 2,
        compiler_params=_params("parallel", "parallel", "parallel"),
        name="mla_attn",
    )(*args)


def _mla_proj_kernel(x_ref, gq_ref, gkv_ref, wq_ref, wkv_ref, cq_ref, saq_ref, sbq_ref, ck_ref, sak_ref,
                     sbk_ref, q_ref, k_ref, v_ref):
    x = x_ref[0]
    xkv = x[:, :MLA_KV_IN]
    lane = lax.broadcasted_iota(jnp.int32, xkv.shape, 1)
    lat = jnp.where(lane < C_KV_LORA, xkv, 0.0)
    ms = jnp.sum(lat * lat, axis=-1, keepdims=True) * (1.0 / C_KV_LORA)
    hk = jnp.where(lane < C_KV_LORA, lat * lax.rsqrt(ms + EPS) * gkv_ref[...], xkv).astype(BF16)
    hq = (_rms(x[:, MLA_KV_IN:]) * gq_ref[...]).astype(BF16)
    nk = C_HEADS * LANE
    aq = _dot(hq, wq_ref[...])
    akv = _dot(hk, wkv_ref[...])
    for g in range(C_HEADS):
        sl = slice(g * LANE, (g + 1) * LANE)
        q_ref[0, :, sl] = _rope_apply(aq[:, sl], cq_ref[...], saq_ref[...], sbq_ref[...], C_ROPE // 2).astype(BF16)
        k_ref[0, :, sl] = _rope_apply(akv[:, sl], ck_ref[...], sak_ref[...], sbk_ref[...], C_ROPE // 2).astype(BF16)
    v_ref[0] = jnp.transpose(akv[:, nk:]).astype(BF16)


def _mla_proj(mla, gq, gkv, wq, wkv, tabs_q, tabs_k, tm):
    Bx, Tx, _ = mla.shape
    row = lambda b, i: (b, i, 0)
    const = lambda b, i: (0, 0)
    tab = pl.BlockSpec((tm, LANE), lambda b, i: (i, 0))
    nk = C_HEADS * LANE
    outs = [(nk, BF16, False), (nk, BF16, False), (C_HEADS * C_V, BF16, True)]
    return pl.pallas_call(
        _mla_proj_kernel,
        grid=(Bx, Tx // tm),
        in_specs=[pl.BlockSpec((1, tm, MLA_W), row),
                  pl.BlockSpec((1, C_Q_LORA), const),
                  pl.BlockSpec((1, MLA_KV_IN), const),
                  _resident(wq.shape, const),
                  _resident(wkv.shape, const),
                  tab, tab, tab, tab, tab, tab],
        out_specs=[_out_spec(tm, w, t) for w, _, t in outs],
        out_shape=[_out_shape(Bx, Tx, w, dt, t) for w, dt, t in outs],
        compiler_params=_params("parallel", "parallel"),
        name="mla_proj",
    )(mla, gq, gkv, wq, wkv, *tabs_q, *tabs_k)


def _merge_kernel(a_ref, b_ref, c_ref, gate_ref, wb_ref, wo_ref, x_ref, mod_ref, gf_ref, rw_ref,
                  o_ref, h_ref, aff_ref):
    D = x_ref.shape[2]
    mod = mod_ref[0]
    z = None
    for n, br in enumerate((a_ref, b_ref, c_ref)):
        y = _dot(br[0], wb_ref[n])
        t = gate_ref[0, :, n * D:(n + 1) * D].astype(F32) * y
        z = t if z is None else z + t
    out = _dot(z.astype(BF16), wo_ref[...])
    x = x_ref[0] + mod[2:3] * out
    o_ref[0] = x
    h = ((_rms(x) * gf_ref[...]) * (1.0 + mod[4:5]) + mod[3:4]).astype(BF16)
    h_ref[0] = h
    logits = _dot(h, rw_ref[...])
    lane = lax.broadcasted_iota(jnp.int32, logits.shape, 1)
    lg = jnp.where(lane < N_EXPERTS, logits, -jnp.inf)
    e = jnp.exp(lg - lg.max(axis=-1, keepdims=True))
    aff_ref[0] = e / e.sum(axis=-1, keepdims=True)


def _merge(a, b, c, gates, wb, wo, x, mods, mod_row, norm_ffn_g, rw, tm):
    Bx, Tx, D = x.shape
    row = lambda bb, i: (bb, i, 0)
    br = pl.BlockSpec((1, tm, BRANCH_W), row)
    outs = [(D, F32), (D, BF16), (LANE, F32)]
    return pl.pallas_call(
        _merge_kernel,
        grid=(Bx, Tx // tm),
        in_specs=[br, br, br,
                  pl.BlockSpec((1, tm, GATE_W), row),
                  _resident(wb.shape, lambda bb, i: (0, 0, 0)),
                  _resident(wo.shape, lambda bb, i: (0, 0)),
                  pl.BlockSpec((1, tm, D), row),
                  pl.BlockSpec((1, 6, D), lambda bb, i: (mod_row(bb), 0, 0)),
                  pl.BlockSpec((1, D), lambda bb, i: (0, 0)),
                  pl.BlockSpec((D, LANE), lambda bb, i: (0, 0))],
        out_specs=[pl.BlockSpec((1, tm, w), row) for w, _ in outs],
        out_shape=[jax.ShapeDtypeStruct((Bx, Tx, w), dt) for w, dt in outs],
        compiler_params=_params("parallel", "parallel"),
        name="merge_out",
    )(a, b, c, gates, wb, wo, x, mods, norm_ffn_g.reshape(1, D), rw)


def _prefix_count(mask, blk):
    T = mask.shape[1]
    r = lax.broadcasted_iota(jnp.int32, (blk, blk), 0)
    c = lax.broadcasted_iota(jnp.int32, (blk, blk), 1)
    tri = jnp.where(r < c, 1.0, 0.0).astype(BF16)
    parts = []
    carry = jnp.zeros((mask.shape[0], 1), F32)
    for i in range(T // blk):
        mb = mask[:, i * blk:(i + 1) * blk]
        parts.append(_dot(mb.astype(BF16), tri) + carry)
        carry = carry + jnp.sum(mb, axis=1, keepdims=True)
    return jnp.concatenate(parts, axis=1) if len(parts) > 1 else parts[0]


def _select_kernel(aff_ref, pos_ref, post_ref, afft_ref, *, cap):
    T = aff_ref.shape[1]
    aff = jnp.transpose(aff_ref[0])[:N_EXPERTS]
    bits = pltpu.bitcast(aff, jnp.int32)

    def search(i, thr):
        cand = thr | (jnp.int32(1) << (30 - i))
        cnt = jnp.sum(jnp.where(bits >= cand, 1.0, 0.0), axis=1, keepdims=True)
        return jnp.where(cnt >= cap, cand, thr)

    thr = lax.fori_loop(0, 31, search, jnp.zeros((N_EXPERTS, 1), jnp.int32))
    gt = jnp.where(bits > thr, 1.0, 0.0)
    eq = jnp.where(bits == thr, 1.0, 0.0)
    need = cap - jnp.sum(gt, axis=1, keepdims=True)
    blk = min(T, 256)
    sel = gt + eq * jnp.where(_prefix_count(eq, blk) < need, 1.0, 0.0)
    post = jnp.where(sel > 0.0, _prefix_count(sel, blk), -1.0)
    post_ref[0] = post
    afft_ref[0] = aff
    pad = jnp.full((LANE - N_EXPERTS, T), -1.0, F32)
    pos_ref[0] = jnp.transpose(jnp.concatenate([post, pad], axis=0))


def _select(aff, cap):
    Bx, Tx, _ = aff.shape
    blk = pl.BlockSpec((1, Tx, LANE), lambda b: (b, 0, 0))
    blk_t = pl.BlockSpec((1, N_EXPERTS, Tx), lambda b: (b, 0, 0))
    return pl.pallas_call(
        functools.partial(_select_kernel, cap=cap),
        grid=(Bx,),
        in_specs=[blk],
        out_specs=[blk, blk_t, blk_t],
        out_shape=[jax.ShapeDtypeStruct((Bx, Tx, LANE), F32),
                   jax.ShapeDtypeStruct((Bx, N_EXPERTS, Tx), F32),
                   jax.ShapeDtypeStruct((Bx, N_EXPERTS, Tx), F32)],
        compiler_params=_params("parallel"),
        name="moe_select",
    )(aff)


DISPATCH_GROUP = 4


def _dispatch_kernel(h_ref, post_ref, afft_ref, xs_ref, gs_ref, *, cap):
    g = pl.program_id(1)
    T = h_ref.shape[1]
    slot = lax.broadcasted_iota(jnp.int32, (cap, T), 0).astype(F32)
    for j in range(DISPATCH_GROUP):
        e = g * DISPATCH_GROUP + j
        hit = post_ref[0, pl.ds(e, 1), :] == slot
        xs_ref[j] = _dot(jnp.where(hit, 1.0, 0.0).astype(BF16), h_ref[0]).astype(BF16)
        gs_ref[j] = jnp.sum(jnp.where(hit, afft_ref[0, pl.ds(e, 1), :], 0.0), axis=1, keepdims=True)


def _dispatch(h, post, afft, cap):
    Bx, Tx, D = h.shape
    tok = pl.BlockSpec((1, N_EXPERTS, Tx), lambda b, g: (b, 0, 0))
    return pl.pallas_call(
        functools.partial(_dispatch_kernel, cap=cap),
        grid=(Bx, N_EXPERTS // DISPATCH_GROUP),
        in_specs=[pl.BlockSpec((1, Tx, D), lambda b, g: (b, 0, 0)), tok, tok],
        out_specs=[pl.BlockSpec((DISPATCH_GROUP, cap, D), lambda b, g: (g, b, 0)),
                   pl.BlockSpec((DISPATCH_GROUP, cap, 1), lambda b, g: (g, b, 0))],
        out_shape=[jax.ShapeDtypeStruct((N_EXPERTS, Bx * cap, D), BF16),
                   jax.ShapeDtypeStruct((N_EXPERTS, Bx * cap, 1), F32)],
        compiler_params=_params("parallel", "parallel"),
        name="moe_dispatch",
    )(h, post, afft)


def _expert_kernel(x_ref, gs_ref, wg_ref, wu_ref, wd_ref, y_ref):
    x = x_ref[0]
    ff = wg_ref.shape[3]
    fc = 512
    acc = None
    for f in range(ff // fc):
        a = _dot(x, wg_ref[0, 0, :, f * fc:(f + 1) * fc])
        u = _dot(x, wu_ref[0, 0, :, f * fc:(f + 1) * fc])
        hm = (a * jax.nn.sigmoid(a) * u).astype(BF16)
        c = _dot(hm, wd_ref[0, 0, f * fc:(f + 1) * fc, :])
        acc = c if acc is None else acc + c
    y_ref[0] = (acc * gs_ref[0]).astype(BF16)


def _experts(xs, gs, wg, wu, wd, layer, tm):
    E, M, D = xs.shape
    FF = wg.shape[3]
    return pl.pallas_call(
        _expert_kernel,
        grid=(E, M // tm),
        in_specs=[pl.BlockSpec((1, tm, D), lambda e, i: (e, i, 0)),
                  pl.BlockSpec((1, tm, 1), lambda e, i: (e, i, 0)),
                  pl.BlockSpec((1, 1, D, FF), lambda e, i: (layer, e, 0, 0)),
                  pl.BlockSpec((1, 1, D, FF), lambda e, i: (layer, e, 0, 0)),
                  pl.BlockSpec((1, 1, FF, D), lambda e, i: (layer, e, 0, 0))],
        out_specs=pl.BlockSpec((1, tm, D), lambda e, i: (e, i, 0)),
        out_shape=jax.ShapeDtypeStruct((E, M, D), BF16),
        compiler_params=_params("parallel", "parallel"),
        name="moe_experts",
    )(xs, gs, wg, wu, wd)


def _combine_kernel(y_ref, pos_ref, x_ref, mod_ref, *rest, cap, group, final_norm):
    o_ref = rest[-1]
    width = group * cap
    pos = pos_ref[0].astype(BF16)
    src = lax.broadcasted_iota(jnp.int32, (LANE, width), 0)
    dst = lax.broadcasted_iota(jnp.int32, (LANE, width), 1)
    slot = (lax.broadcasted_iota(jnp.int32, (1, width), 1) % cap).astype(F32)
    out = None
    for g in range(N_EXPERTS // group):
        spread = jnp.where(dst // cap + g * group == src, 1.0, 0.0).astype(BF16)
        onehot = jnp.where(_dot(pos, spread) == slot, 1.0, 0.0).astype(BF16)
        c = _dot(onehot, y_ref[g * group:(g + 1) * group].reshape(width, y_ref.shape[2]))
        out = c if out is None else out + c
    x = x_ref[0] + mod_ref[0, 5:6] * out
    if final_norm:
        x = _rms(x) * rest[0][...]
    o_ref[0] = x


def _combine(y, pos, x, mods, mod_row, cap, tm, final_g=None):
    Bx, Tx, D = x.shape
    row = lambda b, i: (b, i, 0)
    group = max(4, LANE // cap)
    in_specs = [pl.BlockSpec((N_EXPERTS, cap, D), lambda b, i: (0, b, 0)),
                pl.BlockSpec((1, tm, LANE), row),
                pl.BlockSpec((1, tm, D), row),
                pl.BlockSpec((1, 6, D), lambda b, i: (mod_row(b), 0, 0))]
    args = [y, pos, x, mods]
    if final_g is not None:
        in_specs.append(pl.BlockSpec((1, D), lambda b, i: (0, 0)))
        args.append(final_g.reshape(1, D))
    return pl.pallas_call(
        functools.partial(_combine_kernel, cap=cap, group=group, final_norm=final_g is not None),
        grid=(Bx, Tx // tm),
        in_specs=in_specs,
        out_specs=pl.BlockSpec((1, tm, D), row),
        out_shape=jax.ShapeDtypeStruct((Bx, Tx, D), F32),
        compiler_params=_params("parallel", "parallel"),
        name="moe_combine",
    )(*args)


def _moe(x, h, aff, mods, mod_row, wg, wu, wd, layer, tm, final_g=None):
    Bx, Tx, D = x.shape
    cap = CAPACITY_FACTOR * Tx // N_EXPERTS
    pos, post, afft = _select(aff, cap)
    xs, gs = _dispatch(h, post, afft, cap)
    y = _experts(xs, gs, wg, wu, wd, layer, min(Bx * cap, 512))
    return _combine(y, pos, x, mods, mod_row, cap, tm, final_g)


def _rope_tables(T, rot_dim, lane_lo, period, scale):
    rows = T // GRID_W
    row = jnp.repeat(jnp.arange(rows, dtype=F32), GRID_W)
    colp = jnp.tile(jnp.arange(GRID_W, dtype=F32), rows)
    n_freq = rot_dim // 4
    inv_freq = ROPE_BASE ** (-jnp.arange(n_freq, dtype=F32) / n_freq)
    ang = jnp.concatenate([row[:, None] * inv_freq, colp[:, None] * inv_freq], axis=-1)
    cos_h, sin_h = jnp.cos(ang), jnp.sin(ang)
    half = rot_dim // 2
    cos_g = jnp.ones((T, period), F32)
    sa_g = jnp.zeros((T, period), F32)
    sb_g = jnp.zeros((T, period), F32)
    cos_g = cos_g.at[:, lane_lo:lane_lo + rot_dim].set(jnp.concatenate([cos_h, cos_h], axis=-1))
    sa_g = sa_g.at[:, lane_lo:lane_lo + half].set(-sin_h)
    sb_g = sb_g.at[:, lane_lo + half:lane_lo + rot_dim].set(sin_h)
    rep = LANE // period
    return tuple(jnp.tile(t, (1, rep)) * scale for t in (cos_g, sa_g, sb_g))


def _identity_tables(T, scale):
    return (jnp.full((T, LANE), scale, F32), jnp.zeros((T, LANE), F32), jnp.zeros((T, LANE), F32))


def _prep_w_in(w):
    D = w.shape[0]
    o_diff = HGRN_W
    o_cq = o_diff + 1536
    o_ckv = o_cq + C_Q_LORA
    o_kr = o_ckv + C_KV_LORA
    o_gate = o_kr + C_ROPE
    pad = jnp.zeros((D, MLA_KV_IN - C_KV_LORA - C_ROPE), w.dtype)
    return jnp.concatenate([w[:, :o_cq], w[:, o_ckv:o_kr], w[:, o_kr:o_gate], pad, w[:, o_cq:o_ckv],
                            w[:, o_gate:]], axis=1).astype(BF16)


def _prep_mla_w(w_uq, w_ukv):
    wq = w_uq.reshape(C_Q_LORA, C_HEADS, C_NOPE + C_ROPE)
    wq = jnp.pad(wq, ((0, 0), (0, 0), (0, LANE - C_NOPE - C_ROPE))).reshape(C_Q_LORA, C_HEADS * LANE)
    wkv = w_ukv.reshape(C_KV_LORA, C_HEADS, C_NOPE + C_V)
    wk = jnp.pad(wkv[:, :, :C_NOPE], ((0, 0), (0, 0), (0, LANE - C_NOPE)))
    place = jnp.zeros((C_ROPE, C_HEADS, LANE), F32).at[:, :, C_NOPE:C_NOPE + C_ROPE].set(
        jnp.broadcast_to(jnp.eye(C_ROPE, dtype=F32)[:, None, :], (C_ROPE, C_HEADS, C_ROPE)))
    wk = jnp.concatenate([wk, place, jnp.zeros((MLA_KV_IN - C_KV_LORA - C_ROPE, C_HEADS, LANE), F32)], axis=0)
    wv = jnp.pad(wkv[:, :, C_NOPE:].reshape(C_KV_LORA, C_HEADS * C_V), ((0, MLA_KV_IN - C_KV_LORA), (0, 0)))
    wkv_comb = jnp.concatenate([wk.reshape(MLA_KV_IN, C_HEADS * LANE), wv], axis=1)
    return wq.astype(BF16), wkv_comb.astype(BF16)


def kernel(x, c, ctx, c_ctx, ada_w, ada_b, norm_mix_g, norm_ffn_g, w_in, hgrn_lb, hgrn_norm_g, diff_lambda,
           diff_norm_g, mla_q_norm_g, mla_w_uq, mla_kv_norm_g, mla_w_ukv, w_branch, w_out, router_w,
           exp_w_gate, exp_w_up, exp_w_down, final_norm_g):
    B, T, D = x.shape
    Tc = ctx.shape[1]
    depth = ada_w.shape[0]
    tm_l = min(T, 512)
    tm_c = min(Tc, 256)
    tq = min(Tc, 256)
    tq_l = min(T, 2048)

    rows = ((B + 1 + SUBLANE - 1) // SUBLANE) * SUBLANE
    vecs = jnp.concatenate([c, c_ctx[None], jnp.zeros((rows - B - 1, D), F32)], axis=0)
    mods_all = _modulation(vecs, ada_w, ada_b).reshape(depth, rows, 6, D)
    lat_row = lambda b: b
    ctx_row = lambda b: B

    p = jax.nn.softmax(hgrn_lb.astype(F32), axis=0)
    lower_bounds = jnp.clip(jnp.cumsum(p, axis=0) - p[0], 0.0, LB_MAX)

    d_scale = B_HD ** -0.5 * math.log2(math.e)
    c_scale = (C_NOPE + C_ROPE) ** -0.5 * math.log2(math.e)
    dq_l = _rope_tables(T, B_HD, 0, B_HD, d_scale)
    dk_l = _rope_tables(T, B_HD, 0, B_HD, 1.0)
    dq_c, dk_c = _identity_tables(Tc, d_scale), _identity_tables(Tc, 1.0)
    mq_l = _rope_tables(T, C_ROPE, C_NOPE, LANE, c_scale)
    mk_l = _rope_tables(T, C_ROPE, C_NOPE, LANE, 1.0)
    mq_c, mk_c = _identity_tables(Tc, c_scale), _identity_tables(Tc, 1.0)

    wg, wu, wd = exp_w_gate.astype(BF16), exp_w_up.astype(BF16), exp_w_down.astype(BF16)

    xl, xc = x, ctx
    for layer in range(depth):
        with_ctx = layer < depth - 1
        mods = mods_all[layer]
        w_perm = _prep_w_in(w_in[layer])
        wq, wkv = _prep_mla_w(mla_w_uq[layer], mla_w_ukv[layer])
        wb = w_branch[layer].astype(BF16)
        wo = w_out[layer].astype(BF16)
        rw = jnp.pad(router_w[layer], ((0, 0), (0, LANE - N_EXPERTS))).astype(BF16)
        lam_init = 0.8 - 0.6 * math.exp(-0.3 * layer)
        gq = mla_q_norm_g[layer].reshape(1, -1)
        gkv = jnp.pad(mla_kv_norm_g[layer], (0, MLA_KV_IN - C_KV_LORA)).reshape(1, -1)

        hg_l, dqk_l, dv_l, ml_l, gt_l = _layer_in(xl, norm_mix_g[layer], mods, lat_row, w_perm, dq_l, dk_l, tm_l)
        hg_c, dqk_c, dv_c, ml_c, gt_c = _layer_in(xc, norm_mix_g[layer], mods, ctx_row, w_perm, dq_c, dk_c, tm_c)

        a_l, a_c = _hgrn(hg_l, hg_c, lower_bounds[layer], hgrn_norm_g[layer], with_ctx)

        b_l = _diff_attn(dqk_l, [(dqk_c, dv_c), (dqk_l, dv_l)], diff_lambda[layer], diff_norm_g[layer], lam_init, tq_l)
        mq_lat, mk_lat, mv_lat = _mla_proj(ml_l, gq, gkv, wq, wkv, mq_l, mk_l, tm_l)
        mq_ctx, mk_ctx, mv_ctx = _mla_proj(ml_c, gq, gkv, wq, wkv, mq_c, mk_c, tm_c)
        m_l = _mla_attn(mq_lat, [(mk_ctx, mv_ctx), (mk_lat, mv_lat)], tq_l)

        xl, h_l, aff_l = _merge(a_l, b_l, m_l, gt_l, wb, wo, xl, mods, lat_row, norm_ffn_g[layer], rw, tm_l)
        xl = _moe(xl, h_l, aff_l, mods, lat_row, wg, wu, wd, layer, tm_l, None if with_ctx else final_norm_g)
        if with_ctx:
            b_c = _diff_attn(dqk_c, [(dqk_c, dv_c)], diff_lambda[layer], diff_norm_g[layer], lam_init, tq)
            m_c = _mla_attn(mq_ctx, [(mk_ctx, mv_ctx)], tq)
            xc, h_c, aff_c = _merge(a_c, b_c, m_c, gt_c, wb, wo, xc, mods, ctx_row, norm_ffn_g[layer], rw, tm_c)
            xc = _moe(xc, h_c, aff_c, mods, ctx_row, wg, wu, wd, layer, tm_c)
    return xl
```

```python
import functools
import math

import jax
import jax.numpy as jnp
from jax import lax
from jax.experimental import pallas as pl
from jax.experimental.pallas import tpu as pltpu

F32 = jnp.float32
BF16 = jnp.bfloat16

EPS = 1e-6
ROPE_BASE = 10000.0
GRID_W = 64
F_MIN = 1e-30
LB_MAX = 1.0 - 1e-6

A_HEADS, A_DK = 4, 128
B_HEADS, B_HD = 4, 64
C_HEADS, C_NOPE, C_ROPE, C_V = 8, 64, 32, 64
C_Q_LORA, C_KV_LORA = 384, 256
N_BRANCH, BRANCH_W = 3, 512
N_EXPERTS = 16
CAPACITY_FACTOR = 2

LANE = 128
SUBLANE = 8
VMEM_LIMIT = 56 * 1024 * 1024

HGRN_W = 5 * 512
MLA_W = 768
MLA_KV_IN = 384
GATE_W = N_BRANCH * 1024
HGRN_CHUNK = 128


def _params(*sem):
    return pltpu.CompilerParams(dimension_semantics=sem, vmem_limit_bytes=VMEM_LIMIT)


def _resident(shape, index_map):
    return pl.BlockSpec(shape, index_map, pipeline_mode=pl.Buffered(1))


def _rms(x, eps=EPS):
    return x * lax.rsqrt(jnp.mean(x * x, axis=-1, keepdims=True) + eps)


def _dot(a, b):
    return jnp.dot(a, b, preferred_element_type=F32)


def _dot_nt(a, b):
    return lax.dot_general(a, b, (((1,), (1,)), ((), ())), preferred_element_type=F32)


def _dot_tn(a, b):
    return lax.dot_general(a, b, (((0,), (0,)), ((), ())), preferred_element_type=F32)


def _mods_kernel(v_ref, w_ref, b_ref, o_ref):
    v = v_ref[...]
    s = (v * jax.nn.sigmoid(v)).astype(BF16)
    o_ref[0] = _dot(s, w_ref[0].astype(BF16)) + b_ref[0]


def _modulation(vecs, ada_w, ada_b):
    L, D, N = ada_w.shape
    R = vecs.shape[0]
    tn = 1536
    return pl.pallas_call(
        _mods_kernel,
        grid=(L, N // tn),
        in_specs=[pl.BlockSpec((R, D), lambda l, j: (0, 0)),
                  pl.BlockSpec((1, D, tn), lambda l, j: (l, 0, j)),
                  pl.BlockSpec((1, 1, tn), lambda l, j: (l, 0, j))],
        out_specs=pl.BlockSpec((1, R, tn), lambda l, j: (l, 0, j)),
        out_shape=jax.ShapeDtypeStruct((L, R, N), F32),
        compiler_params=_params("parallel", "parallel"),
        name="adaln_mods",
    )(vecs, ada_w, ada_b.reshape(L, 1, N))


def _rope_apply(x, cos, sin_a, sin_b, half):
    return (x * cos + pltpu.roll(x, LANE - half, axis=1) * sin_a + pltpu.roll(x, half, axis=1) * sin_b)


def _layer_in_kernel(x_ref, g_ref, mod_ref, w_ref, cq_ref, saq_ref, sbq_ref, ck_ref, sak_ref, sbk_ref,
                     hg_ref, dqk_ref, dv_ref, mla_ref, gate_ref, h_scr):
    mod = mod_ref[0]
    half = x_ref.shape[1] // 2
    c0 = 0
    for r in range(2):
        rows = slice(r * half, (r + 1) * half)
        h_scr[rows] = ((_rms(x_ref[0, rows]) * g_ref[...]) * (1.0 + mod[1:2]) + mod[0:1]).astype(BF16)
        hg_ref[0, rows] = _dot(h_scr[rows], w_ref[:, c0:c0 + HGRN_W])
    h = h_scr[...]
    c0 += HGRN_W
    for j in range(2):
        acc = _dot(h, w_ref[:, c0:c0 + 512])
        cos, sa, sb = (cq_ref, saq_ref, sbq_ref) if j == 0 else (ck_ref, sak_ref, sbk_ref)
        for g in range(4):
            xg = acc[:, g * LANE:(g + 1) * LANE]
            dqk_ref[0, :, j * 512 + g * LANE:j * 512 + (g + 1) * LANE] = _rope_apply(
                xg, cos[...], sa[...], sb[...], B_HD // 2).astype(BF16)
        c0 += 512
    dv_ref[0] = jnp.transpose(_dot(h, w_ref[:, c0:c0 + 512])).astype(BF16)
    c0 += 512
    mla_ref[0] = _dot(h, w_ref[:, c0:c0 + MLA_W])
    c0 += MLA_W
    for j in range(GATE_W // 1024):
        gate_ref[0, :, j * 1024:(j + 1) * 1024] = jax.nn.sigmoid(_dot(h, w_ref[:, c0:c0 + 1024])).astype(BF16)
        c0 += 1024


def _out_spec(tm, width, transposed):
    if transposed:
        return pl.BlockSpec((1, width, tm), lambda b, i: (b, 0, i))
    return pl.BlockSpec((1, tm, width), lambda b, i: (b, i, 0))


def _out_shape(bx, tx, width, dtype, transposed):
    return jax.ShapeDtypeStruct((bx, width, tx) if transposed else (bx, tx, width), dtype)


def _layer_in(x, norm_g, mods, mod_row, w_perm, tabs_q, tabs_k, tm):
    Bx, Tx, D = x.shape
    NW = w_perm.shape[1]
    row = lambda b, i: (b, i, 0)
    tab = pl.BlockSpec((tm, LANE), lambda b, i: (i, 0))
    outs = [(HGRN_W, F32, False), (1024, BF16, False), (512, BF16, True), (MLA_W, F32, False),
            (GATE_W, BF16, False)]
    return pl.pallas_call(
        _layer_in_kernel,
        grid=(Bx, Tx // tm),
        in_specs=[pl.BlockSpec((1, tm, D), row),
                  pl.BlockSpec((1, D), lambda b, i: (0, 0)),
                  pl.BlockSpec((1, 6, D), lambda b, i: (mod_row(b), 0, 0)),
                  _resident((D, NW), lambda b, i: (0, 0)),
                  tab, tab, tab, tab, tab, tab],
        out_specs=[_out_spec(tm, w, t) for w, _, t in outs],
        out_shape=[_out_shape(Bx, Tx, w, dt, t) for w, dt, t in outs],
        scratch_shapes=[pltpu.VMEM((tm, D), BF16)],
        compiler_params=_params("parallel", "parallel"),
        name="layer_in",
    )(x, norm_g.reshape(1, D), mods, w_perm, *tabs_q, *tabs_k)


def _cumsum_rows(x, reverse):
    n = x.shape[0]
    s = 1
    while s < n:
        if s < SUBLANE:
            row = lax.broadcasted_iota(jnp.int32, x.shape, 0)
            if reverse:
                sh = jnp.where(row < n - s, pltpu.roll(x, n - s, axis=0), 0.0)
            else:
                sh = jnp.where(row >= s, pltpu.roll(x, s, axis=0), 0.0)
        else:
            z = jnp.zeros((s, x.shape[1]), x.dtype)
            sh = jnp.concatenate([x[s:], z], axis=0) if reverse else jnp.concatenate([z, x[:n - s]], axis=0)
        x = x + sh
        s *= 2
    return x


def _bcast_rows(load_row, rows, reps):
    parts = []
    for r in rows:
        parts += [load_row(r)] * reps
    return jnp.concatenate(parts, axis=0)


def _hgrn_chunk(q, k_in, v, v_row, lf, st, b_scr, k_scr, mask_ref, reverse):
    C = q.shape[0]
    b = _cumsum_rows(lf, reverse)
    b_scr[...] = b
    k_scr[...] = k_in
    b_row = lambda r: jnp.broadcast_to(b_scr[pl.ds(r, 1), :], (SUBLANE, LANE))
    k_row = lambda r: jnp.broadcast_to(k_scr[pl.ds(r, 1), :], (SUBLANE, LANE))
    b_last = b[0:1] if reverse else b[C - 1:C]
    qd = (q * jnp.exp2(b)).astype(BF16)
    o = _dot_nt(qd, st.astype(BF16))
    kd = (k_in * jnp.exp2(b_last - b)).astype(BF16)
    vb = v.astype(BF16)
    st_new = st * jnp.exp2(b_last) + _dot_tn(vb, kd)

    scores = None
    for li, m in enumerate(_hgrn_levels(C)):
        mid = m if reverse else m - 1
        anchor = _bcast_rows(b_row, [i * 2 * m + mid for i in range(C // (2 * m))], 2 * m // SUBLANE)
        e = jnp.exp2(-jnp.abs(b - anchor))
        sc = _dot_nt((q * e).astype(BF16), (k_in * e).astype(BF16)) * mask_ref[li]
        scores = sc if scores is None else scores + sc
    o = o + _dot(scores.astype(BF16), vb)

    t8 = lax.broadcasted_iota(jnp.int32, (C, 1), 0) % SUBLANE
    for j in range(SUBLANE):
        rows = [SUBLANE * i + j for i in range(C // SUBLANE)]
        w = q * jnp.exp2(b - _bcast_rows(b_row, rows, 1)) * _bcast_rows(k_row, rows, 1)
        valid = (t8 <= j) if reverse else (t8 >= j)
        sc = jnp.sum(jnp.where(valid, w, 0.0), axis=-1, keepdims=True)
        o = o + sc * _bcast_rows(v_row, rows, 1)
    return o, st_new


def _hgrn_levels(C):
    out = []
    m = C // 2
    while m >= SUBLANE:
        out.append(m)
        m //= 2
    return out


def _hgrn_masks(C, reverse):
    row = lax.broadcasted_iota(jnp.int32, (C, C), 0)
    col = lax.broadcasted_iota(jnp.int32, (C, C), 1)
    masks = []
    for m in _hgrn_levels(C):
        same = (row // (2 * m)) == (col // (2 * m))
        t_second = (row % (2 * m)) >= m
        s_second = (col % (2 * m)) >= m
        if reverse:
            ok = same & jnp.logical_not(t_second) & s_second
        else:
            ok = same & t_second & jnp.logical_not(s_second)
        masks.append(jnp.where(ok, 1.0, 0.0))
    return masks


def _hgrn_kernel(*refs, with_ctx_out):
    (ql, il, gl, ffl, fbl, qc, ic, gc, ffc, fbc, lb_ref, ng_ref) = refs[:12]
    if with_ctx_out:
        ol_ref, oc_ref, accl, accc, b_scr, k_scr, mask_scr = refs[12:]
    else:
        ol_ref, accl, accc, b_scr, k_scr, mask_scr = refs[12:]
        oc_ref = None
    C = HGRN_CHUNK
    nl = ql.shape[1] // C
    nc = qc.shape[1] // C

    accl[...] = jnp.zeros_like(accl)
    accc[...] = jnp.zeros_like(accc)
    for d in range(2):
        for li, mk in enumerate(_hgrn_masks(C, d == 1)):
            mask_scr[d, li] = mk

    def make_step(qr, ir, frs, acc, n):
        def step(s, sts):
            new = []
            for d in range(2):
                reverse = d == 1
                lbd = lb_ref[d:d + 1, :]
                one_m = 1.0 - lbd
                c = (n - 1 - s) if reverse else s
                r0 = pl.multiple_of(c * C, C)
                sg = jax.nn.sigmoid(frs[d][0, pl.ds(r0, C), :])
                kk = one_m * (1.0 - sg)
                lf = jnp.log2(jnp.maximum(lbd + one_m * sg, F_MIN))
                v_row = lambda r, r0=r0: jnp.broadcast_to(ir[0, pl.ds(r0 + r, 1), :], (SUBLANE, LANE))
                o, st = _hgrn_chunk(qr[0, pl.ds(r0, C), :], kk, ir[0, pl.ds(r0, C), :], v_row, lf, sts[d],
                                    b_scr.at[d], k_scr.at[d], mask_scr.at[d], reverse)
                acc[pl.ds(r0, C), :] += o
                new.append(st)
            return tuple(new)
        return step

    zero = jnp.zeros((LANE, LANE), F32)
    unroll = lambda n: 4 if n % 4 == 0 else (2 if n % 2 == 0 else 1)
    sts = lax.fori_loop(0, nc, make_step(qc, ic, (ffc, fbc), accc, nc), (zero, zero), unroll=unroll(nc))
    lax.fori_loop(0, nl, make_step(ql, il, (ffl, fbl), accl, nl), sts, unroll=unroll(nl))

    def readout(acc, g_ref, o_ref):
        o = _rms(acc[...]) * ng_ref[...]
        g = g_ref[0]
        o_ref[0] = (o * (g * jax.nn.sigmoid(g))).astype(BF16)

    readout(accl, gl, ol_ref)
    if with_ctx_out:
        readout(accc, gc, oc_ref)


def _hgrn(hg_l, hg_c, lb, norm_g, with_ctx_out):
    B, T, _ = hg_l.shape
    Tc = hg_c.shape[1]
    nh = A_HEADS

    def col(n, t):
        return [pl.BlockSpec((1, t, LANE), (lambda b, h, j=j: (b, 0, j * nh + h))) for j in range(n)]

    in_specs = col(5, T) + col(5, Tc) + [pl.BlockSpec((2, LANE), lambda b, h: (0, h)),
                                         pl.BlockSpec((1, LANE), lambda b, h: (0, h))]
    out_specs = [pl.BlockSpec((1, T, LANE), lambda b, h: (b, 0, h))]
    out_shape = [jax.ShapeDtypeStruct((B, T, nh * LANE), BF16)]
    if with_ctx_out:
        out_specs.append(pl.BlockSpec((1, Tc, LANE), lambda b, h: (b, 0, h)))
        out_shape.append(jax.ShapeDtypeStruct((B, Tc, nh * LANE), BF16))
    res = pl.pallas_call(
        functools.partial(_hgrn_kernel, with_ctx_out=with_ctx_out),
        grid=(B, nh),
        in_specs=in_specs,
        out_specs=out_specs,
        out_shape=out_shape,
        scratch_shapes=[pltpu.VMEM((T, LANE), F32), pltpu.VMEM((Tc, LANE), F32),
                        pltpu.VMEM((2, HGRN_CHUNK, LANE), F32), pltpu.VMEM((2, HGRN_CHUNK, LANE), F32),
                        pltpu.VMEM((2, len(_hgrn_levels(HGRN_CHUNK)), HGRN_CHUNK, HGRN_CHUNK), F32)],
        compiler_params=_params("parallel", "parallel"),
        name="hgrn2",
    )(*([hg_l] * 5), *([hg_c] * 5), lb, norm_g.reshape(1, -1))
    return (res[0], res[1]) if with_ctx_out else (res[0], None)


ATTN_KEY_CHUNK = 1024


def _key_chunks(kv):
    off = 0
    for k_ref, v_ref in kv:
        tk = k_ref.shape[1]
        kc = min(tk, ATTN_KEY_CHUNK)
        for c in range(tk // kc):
            yield off, kc, k_ref, v_ref, c
            off += kc


def _score_phase(q, kv, kcols, s_scr):
    mx = None
    for off, kc, k_ref, _, c in _key_chunks(kv):
        s = _dot_nt(k_ref[0, c * kc:(c + 1) * kc, kcols], q)
        s_scr[off:off + kc, :] = s
        cm = s.max(axis=0, keepdims=True)
        mx = cm if mx is None else jnp.maximum(mx, cm)
        yield None
    yield mx


def _pv_phase(m, kv, vrows, s_scr):
    lv = acc = None
    for off, kc, _, v_ref, c in _key_chunks(kv):
        p = jnp.exp2(s_scr[off:off + kc, :] - m)
        ps = p.sum(axis=0, keepdims=True)
        lv = ps if lv is None else lv + ps
        pv = _dot(v_ref[0, vrows, c * kc:(c + 1) * kc], p.astype(BF16))
        acc = pv if acc is None else acc + pv
        yield None
    yield acc, lv


def _interleave(*gens):
    last = [None] * len(gens)
    live = list(range(len(gens)))
    while live:
        for i in list(live):
            try:
                v = next(gens[i])
                if v is not None:
                    last[i] = v
            except StopIteration:
                live.remove(i)
    return last


ATTN_SUB_Q = 512


def _softmax_pv_items(items, kv, scrs):
    out = []
    m_prev = None
    for i, (q, kc, vr) in enumerate(items):
        score = _score_phase(q, kv, kc, scrs[i % 2])
        if i == 0:
            m_prev, = _interleave(score)
        else:
            m_prev, res = _interleave(score, _pv_phase(m_prev, kv, items[i - 1][2], scrs[(i - 1) % 2]))
            out.append(res)
    res, = _interleave(_pv_phase(m_prev, kv, items[-1][2], scrs[(len(items) - 1) % 2]))
    out.append(res)
    return out


def _diff_attn_kernel(*refs, n_kv, lam_init):
    q_ref, lam_ref, ng_ref = refs[0], refs[1], refs[2]
    kv = [(refs[3 + 2 * i], refs[4 + 2 * i]) for i in range(n_kv)]
    o_ref, s1_scr, s2_scr = refs[3 + 2 * n_kv:]
    lp = lam_ref[...]
    lam = (jnp.exp(jnp.sum(lp[0:1] * lp[1:2], axis=-1, keepdims=True))
           - jnp.exp(jnp.sum(lp[2:3] * lp[3:4], axis=-1, keepdims=True)) + lam_init)
    sub = s1_scr.shape[1]
    full = slice(None)
    items = []
    for r in range(q_ref.shape[1] // sub):
        q = q_ref[0, r * sub:(r + 1) * sub, :]
        lane = lax.broadcasted_iota(jnp.int32, q.shape, 1)
        zero = jnp.zeros_like(q)
        items += [(jnp.where(lane < B_HD, q, zero), full, full), (jnp.where(lane < B_HD, zero, q), full, full)]
    res = _softmax_pv_items(items, kv, (s1_scr, s2_scr))
    for r in range(q_ref.shape[1] // sub):
        (a1, l1), (a2, l2) = res[2 * r], res[2 * r + 1]
        o = jnp.transpose(a1 * (1.0 / l1) - a2 * (lam / l2))
        o_ref[0, r * sub:(r + 1) * sub, :] = (_rms(o) * ng_ref[...] * (1.0 - lam_init)).astype(BF16)


def _diff_attn(q_src, kvs, lam_params, norm_g, lam_init, tq):
    B, Tq, _ = q_src.shape
    nh = B_HEADS
    in_specs = [pl.BlockSpec((1, tq, LANE), lambda b, h, i: (b, i, h)),
                pl.BlockSpec((4, B_HD), lambda b, h, i: (0, 0)),
                pl.BlockSpec((1, LANE), lambda b, h, i: (0, h))]
    args = [q_src, lam_params, norm_g.reshape(1, -1)]
    n_keys = 0
    for k_arr, v_arr in kvs:
        Tk = k_arr.shape[1]
        n_keys += Tk
        in_specs.append(pl.BlockSpec((1, Tk, LANE), lambda b, h, i: (b, 0, nh + h)))
        in_specs.append(pl.BlockSpec((1, LANE, Tk), lambda b, h, i: (b, h, 0)))
        args += [k_arr, v_arr]
    return pl.pallas_call(
        functools.partial(_diff_attn_kernel, n_kv=len(kvs), lam_init=lam_init),
        grid=(B, nh, Tq // tq),
        in_specs=in_specs,
        out_specs=pl.BlockSpec((1, tq, LANE), lambda b, h, i: (b, i, h)),
        out_shape=jax.ShapeDtypeStruct((B, Tq, nh * LANE), BF16),
        scratch_shapes=[pltpu.VMEM((n_keys, min(tq, ATTN_SUB_Q)), F32)] * 2,
        compiler_params=_params("parallel", "parallel", "parallel"),
        name="diff_attn",
    )(*args)


def _mla_attn_kernel(*refs, n_kv):
    q_ref = refs[0]
    kv = [(refs[1 + 2 * i], refs[2 + 2 * i]) for i in range(n_kv)]
    o_ref, s1_scr, s2_scr = refs[1 + 2 * n_kv:]
    sub = s1_scr.shape[1]
    c0, c1 = slice(0, LANE), slice(LANE, 2 * LANE)
    v0, v1 = slice(0, C_V), slice(C_V, 2 * C_V)
    items = []
    for r in range(q_ref.shape[1] // sub):
        rows = slice(r * sub, (r + 1) * sub)
        items += [(q_ref[0, rows, c0], c0, v0), (q_ref[0, rows, c1], c1, v1)]
    res = _softmax_pv_items(items, kv, (s1_scr, s2_scr))
    for r in range(q_ref.shape[1] // sub):
        (a0, l0), (a1, l1) = res[2 * r], res[2 * r + 1]
        ot = jnp.concatenate([a0 * (1.0 / l0), a1 * (1.0 / l1)], axis=0)
        o_ref[0, r * sub:(r + 1) * sub, :] = jnp.transpose(ot).astype(BF16)


def _mla_attn(q, kvs, tq):
    B, Tq, _ = q.shape
    npair = C_HEADS // 2
    in_specs = [pl.BlockSpec((1, tq, 2 * LANE), lambda b, h, i: (b, i, h))]
    args = [q]
    n_keys = 0
    for k_arr, v_arr in kvs:
        Tk = k_arr.shape[1]
        n_keys += Tk
        in_specs.append(pl.BlockSpec((1, Tk, 2 * LANE), lambda b, h, i: (b, 0, h)))
        in_specs.append(pl.BlockSpec((1, LANE, Tk), lambda b, h, i: (b, h, 0)))
        args += [k_arr, v_arr]
    return pl.pallas_call(
        functools.partial(_mla_attn_kernel, n_kv=len(kvs)),
        grid=(B, npair, Tq // tq),
        in_specs=in_specs,
        out_specs=pl.BlockSpec((1, tq, LANE), lambda b, h, i: (b, i, h)),
        out_shape=jax.ShapeDtypeStruct((B, Tq, C_HEADS * C_V), BF16),
        scratch_shapes=[pltpu.VMEM((n_keys, min(tq, ATTN_SUB_Q)), F32)] * 2,
        compiler_params=_params("parallel", "parallel", "parallel"),
        name="mla_attn",
    )(*args)


def _mla_proj_kernel(x_ref, gq_ref, gkv_ref, wq_ref, wkv_ref, cq_ref, saq_ref, sbq_ref, ck_ref, sak_ref,
                     sbk_ref, q_ref, k_ref, v_ref):
    x = x_ref[0]
    xkv = x[:, :MLA_KV_IN]
    lane = lax.broadcasted_iota(jnp.int32, xkv.shape, 1)
    lat = jnp.where(lane < C_KV_LORA, xkv, 0.0)
    ms = jnp.sum(lat * lat, axis=-1, keepdims=True) * (1.0 / C_KV_LORA)
    hk = jnp.where(lane < C_KV_LORA, lat * lax.rsqrt(ms + EPS) * gkv_ref[...], xkv).astype(BF16)
    hq = (_rms(x[:, MLA_KV_IN:]) * gq_ref[...]).astype(BF16)
    nk = C_HEADS * LANE
    aq = _dot(hq, wq_ref[...])
    akv = _dot(hk, wkv_ref[...])
    for g in range(C_HEADS):
        sl = slice(g * LANE, (g + 1) * LANE)
        q_ref[0, :, sl] = _rope_apply(aq[:, sl], cq_ref[...], saq_ref[...], sbq_ref[...], C_ROPE // 2).astype(BF16)
        k_ref[0, :, sl] = _rope_apply(akv[:, sl], ck_ref[...], sak_ref[...], sbk_ref[...], C_ROPE // 2).astype(BF16)
    v_ref[0] = jnp.transpose(akv[:, nk:]).astype(BF16)


def _mla_proj(mla, gq, gkv, wq, wkv, tabs_q, tabs_k, tm):
    Bx, Tx, _ = mla.shape
    row = lambda b, i: (b, i, 0)
    const = lambda b, i: (0, 0)
    tab = pl.BlockSpec((tm, LANE), lambda b, i: (i, 0))
    nk = C_HEADS * LANE
    outs = [(nk, BF16, False), (nk, BF16, False), (C_HEADS * C_V, BF16, True)]
    return pl.pallas_call(
        _mla_proj_kernel,
        grid=(Bx, Tx // tm),
        in_specs=[pl.BlockSpec((1, tm, MLA_W), row),
                  pl.BlockSpec((1, C_Q_LORA), const),
                  pl.BlockSpec((1, MLA_KV_IN), const),
                  _resident(wq.shape, const),
                  _resident(wkv.shape, const),
                  tab, tab, tab, tab, tab, tab],
        out_specs=[_out_spec(tm, w, t) for w, _, t in outs],
        out_shape=[_out_shape(Bx, Tx, w, dt, t) for w, dt, t in outs],
        compiler_params=_params("parallel", "parallel"),
        name="mla_proj",
    )(mla, gq, gkv, wq, wkv, *tabs_q, *tabs_k)


def _merge_kernel(a_ref, b_ref, c_ref, gate_ref, wb_ref, wo_ref, x_ref, mod_ref, gf_ref, rw_ref,
                  o_ref, h_ref, aff_ref):
    D = x_ref.shape[2]
    mod = mod_ref[0]
    z = None
    for n, br in enumerate((a_ref, b_ref, c_ref)):
        y = _dot(br[0], wb_ref[n])
        t = gate_ref[0, :, n * D:(n + 1) * D].astype(F32) * y
        z = t if z is None else z + t
    out = _dot(z.astype(BF16), wo_ref[...])
    x = x_ref[0] + mod[2:3] * out
    o_ref[0] = x
    h = ((_rms(x) * gf_ref[...]) * (1.0 + mod[4:5]) + mod[3:4]).astype(BF16)
    h_ref[0] = h
    logits = _dot(h, rw_ref[...])
    lane = lax.broadcasted_iota(jnp.int32, logits.shape, 1)
    lg = jnp.where(lane < N_EXPERTS, logits, -jnp.inf)
    e = jnp.exp(lg - lg.max(axis=-1, keepdims=True))
    aff_ref[0] = e / e.sum(axis=-1, keepdims=True)


def _merge(a, b, c, gates, wb, wo, x, mods, mod_row, norm_ffn_g, rw, tm):
    Bx, Tx, D = x.shape
    row = lambda bb, i: (bb, i, 0)
    br = pl.BlockSpec((1, tm, BRANCH_W), row)
    outs = [(D, F32), (D, BF16), (LANE, F32)]
    return pl.pallas_call(
        _merge_kernel,
        grid=(Bx, Tx // tm),
        in_specs=[br, br, br,
                  pl.BlockSpec((1, tm, GATE_W), row),
                  _resident(wb.shape, lambda bb, i: (0, 0, 0)),
                  _resident(wo.shape, lambda bb, i: (0, 0)),
                  pl.BlockSpec((1, tm, D), row),
                  pl.BlockSpec((1, 6, D), lambda bb, i: (mod_row(bb), 0, 0)),
                  pl.BlockSpec((1, D), lambda bb, i: (0, 0)),
                  pl.BlockSpec((D, LANE), lambda bb, i: (0, 0))],
        out_specs=[pl.BlockSpec((1, tm, w), row) for w, _ in outs],
        out_shape=[jax.ShapeDtypeStruct((Bx, Tx, w), dt) for w, dt in outs],
        compiler_params=_params("parallel", "parallel"),
        name="merge_out",
    )(a, b, c, gates, wb, wo, x, mods, norm_ffn_g.reshape(1, D), rw)


def _prefix_count(mask, blk):
    T = mask.shape[1]
    r = lax.broadcasted_iota(jnp.int32, (blk, blk), 0)
    c = lax.broadcasted_iota(jnp.int32, (blk, blk), 1)
    tri = jnp.where(r < c, 1.0, 0.0).astype(BF16)
    parts = []
    carry = jnp.zeros((mask.shape[0], 1), F32)
    for i in range(T // blk):
        mb = mask[:, i * blk:(i + 1) * blk]
        parts.append(_dot(mb.astype(BF16), tri) + carry)
        carry = carry + jnp.sum(mb, axis=1, keepdims=True)
    return jnp.concatenate(parts, axis=1) if len(parts) > 1 else parts[0]


def _select_kernel(aff_ref, pos_ref, post_ref, afft_ref, *, cap):
    T = aff_ref.shape[1]
    aff = jnp.transpose(aff_ref[0])[:N_EXPERTS]
    bits = pltpu.bitcast(aff, jnp.int32)

    def search(i, thr):
        cand = thr | (jnp.int32(1) << (30 - i))
        cnt = jnp.sum(jnp.where(bits >= cand, 1.0, 0.0), axis=1, keepdims=True)
        return jnp.where(cnt >= cap, cand, thr)

    thr = lax.fori_loop(0, 31, search, jnp.zeros((N_EXPERTS, 1), jnp.int32))
    gt = jnp.where(bits > thr, 1.0, 0.0)
    eq = jnp.where(bits == thr, 1.0, 0.0)
    need = cap - jnp.sum(gt, axis=1, keepdims=True)
    blk = min(T, 256)
    sel = gt + eq * jnp.where(_prefix_count(eq, blk) < need, 1.0, 0.0)
    post = jnp.where(sel > 0.0, _prefix_count(sel, blk), -1.0)
    post_ref[0] = post
    afft_ref[0] = aff
    pad = jnp.full((LANE - N_EXPERTS, T), -1.0, F32)
    pos_ref[0] = jnp.transpose(jnp.concatenate([post, pad], axis=0))


def _select(aff, cap):
    Bx, Tx, _ = aff.shape
    blk = pl.BlockSpec((1, Tx, LANE), lambda b: (b, 0, 0))
    blk_t = pl.BlockSpec((1, N_EXPERTS, Tx), lambda b: (b, 0, 0))
    return pl.pallas_call(
        functools.partial(_select_kernel, cap=cap),
        grid=(Bx,),
        in_specs=[blk],
        out_specs=[blk, blk_t, blk_t],
        out_shape=[jax.ShapeDtypeStruct((Bx, Tx, LANE), F32),
                   jax.ShapeDtypeStruct((Bx, N_EXPERTS, Tx), F32),
                   jax.ShapeDtypeStruct((Bx, N_EXPERTS, Tx), F32)],
        compiler_params=_params("parallel"),
        name="moe_select",
    )(aff)


DISPATCH_GROUP = 8


def _dispatch_kernel(h_ref, post_ref, afft_ref, xs_ref, gs_ref, *, cap):
    g = pl.program_id(1)
    T = h_ref.shape[1]
    slot = lax.broadcasted_iota(jnp.int32, (cap, T), 0).astype(F32)
    for j in range(DISPATCH_GROUP):
        e = g * DISPATCH_GROUP + j
        hit = post_ref[0, pl.ds(e, 1), :] == slot
        xs_ref[j] = _dot(jnp.where(hit, 1.0, 0.0).astype(BF16), h_ref[0]).astype(BF16)
        gs_ref[j] = jnp.sum(jnp.where(hit, afft_ref[0, pl.ds(e, 1), :], 0.0), axis=1, keepdims=True)


def _dispatch(h, post, afft, cap):
    Bx, Tx, D = h.shape
    tok = pl.BlockSpec((1, N_EXPERTS, Tx), lambda b, g: (b, 0, 0))
    return pl.pallas_call(
        functools.partial(_dispatch_kernel, cap=cap),
        grid=(Bx, N_EXPERTS // DISPATCH_GROUP),
        in_specs=[pl.BlockSpec((1, Tx, D), lambda b, g: (b, 0, 0)), tok, tok],
        out_specs=[pl.BlockSpec((DISPATCH_GROUP, cap, D), lambda b, g: (g, b, 0)),
                   pl.BlockSpec((DISPATCH_GROUP, cap, 1), lambda b, g: (g, b, 0))],
        out_shape=[jax.ShapeDtypeStruct((N_EXPERTS, Bx * cap, D), BF16),
                   jax.ShapeDtypeStruct((N_EXPERTS, Bx * cap, 1), F32)],
        compiler_params=_params("parallel", "parallel"),
        name="moe_dispatch",
    )(h, post, afft)


def _expert_kernel(x_ref, gs_ref, wg_ref, wu_ref, wd_ref, y_ref):
    x = x_ref[0]
    ff = wg_ref.shape[3]
    fc = 512
    acc = None
    for f in range(ff // fc):
        a = _dot(x, wg_ref[0, 0, :, f * fc:(f + 1) * fc])
        u = _dot(x, wu_ref[0, 0, :, f * fc:(f + 1) * fc])
        hm = (a * jax.nn.sigmoid(a) * u).astype(BF16)
        c = _dot(hm, wd_ref[0, 0, f * fc:(f + 1) * fc, :])
        acc = c if acc is None else acc + c
    y_ref[0] = (acc * gs_ref[0]).astype(BF16)


def _experts(xs, gs, wg, wu, wd, layer, tm):
    E, M, D = xs.shape
    FF = wg.shape[3]
    return pl.pallas_call(
        _expert_kernel,
        grid=(E, M // tm),
        in_specs=[pl.BlockSpec((1, tm, D), lambda e, i: (e, i, 0)),
                  pl.BlockSpec((1, tm, 1), lambda e, i: (e, i, 0)),
                  pl.BlockSpec((1, 1, D, FF), lambda e, i: (layer, e, 0, 0)),
                  pl.BlockSpec((1, 1, D, FF), lambda e, i: (layer, e, 0, 0)),
                  pl.BlockSpec((1, 1, FF, D), lambda e, i: (layer, e, 0, 0))],
        out_specs=pl.BlockSpec((1, tm, D), lambda e, i: (e, i, 0)),
        out_shape=jax.ShapeDtypeStruct((E, M, D), BF16),
        compiler_params=_params("parallel", "parallel"),
        name="moe_experts",
    )(xs, gs, wg, wu, wd)


def _combine_kernel(y_ref, pos_ref, x_ref, mod_ref, *rest, cap, group, final_norm):
    o_ref = rest[-1]
    width = group * cap
    pos = pos_ref[0].astype(BF16)
    src = lax.broadcasted_iota(jnp.int32, (LANE, width), 0)
    dst = lax.broadcasted_iota(jnp.int32, (LANE, width), 1)
    slot = (lax.broadcasted_iota(jnp.int32, (1, width), 1) % cap).astype(F32)
    out = None
    for g in range(N_EXPERTS // group):
        spread = jnp.where(dst // cap + g * group == src, 1.0, 0.0).astype(BF16)
        onehot = jnp.where(_dot(pos, spread) == slot, 1.0, 0.0).astype(BF16)
        c = _dot(onehot, y_ref[g * group:(g + 1) * group].reshape(width, y_ref.shape[2]))
        out = c if out is None else out + c
    x = x_ref[0] + mod_ref[0, 5:6] * out
    if final_norm:
        x = _rms(x) * rest[0][...]
    o_ref[0] = x


def _combine(y, pos, x, mods, mod_row, cap, tm, final_g=None):
    Bx, Tx, D = x.shape
    row = lambda b, i: (b, i, 0)
    group = max(4, LANE // cap)
    in_specs = [pl.BlockSpec((N_EXPERTS, cap, D), lambda b, i: (0, b, 0)),
                pl.BlockSpec((1, tm, LANE), row),
                pl.BlockSpec((1, tm, D), row),
                pl.BlockSpec((1, 6, D), lambda b, i: (mod_row(b), 0, 0))]
    args = [y, pos, x, mods]
    if final_g is not None:
        in_specs.append(pl.BlockSpec((1, D), lambda b, i: (0, 0)))
        args.append(final_g.reshape(1, D))
    return pl.pallas_call(
        functools.partial(_combine_kernel, cap=cap, group=group, final_norm=final_g is not None),
        grid=(Bx, Tx // tm),
        in_specs=in_specs,
        out_specs=pl.BlockSpec((1, tm, D), row),
        out_shape=jax.ShapeDtypeStruct((Bx, Tx, D), F32),
        compiler_params=_params("parallel", "parallel"),
        name="moe_combine",
    )(*args)


def _moe(x, h, aff, mods, mod_row, wg, wu, wd, layer, tm, final_g=None):
    Bx, Tx, D = x.shape
    cap = CAPACITY_FACTOR * Tx // N_EXPERTS
    pos, post, afft = _select(aff, cap)
    xs, gs = _dispatch(h, post, afft, cap)
    y = _experts(xs, gs, wg, wu, wd, layer, min(Bx * cap, 512))
    return _combine(y, pos, x, mods, mod_row, cap, tm, final_g)


def _rope_tables(T, rot_dim, lane_lo, period, scale):
    rows = T // GRID_W
    row = jnp.repeat(jnp.arange(rows, dtype=F32), GRID_W)
    colp = jnp.tile(jnp.arange(GRID_W, dtype=F32), rows)
    n_freq = rot_dim // 4
    inv_freq = ROPE_BASE ** (-jnp.arange(n_freq, dtype=F32) / n_freq)
    ang = jnp.concatenate([row[:, None] * inv_freq, colp[:, None] * inv_freq], axis=-1)
    cos_h, sin_h = jnp.cos(ang), jnp.sin(ang)
    half = rot_dim // 2
    cos_g = jnp.ones((T, period), F32)
    sa_g = jnp.zeros((T, period), F32)
    sb_g = jnp.zeros((T, period), F32)
    cos_g = cos_g.at[:, lane_lo:lane_lo + rot_dim].set(jnp.concatenate([cos_h, cos_h], axis=-1))
    sa_g = sa_g.at[:, lane_lo:lane_lo + half].set(-sin_h)
    sb_g = sb_g.at[:, lane_lo + half:lane_lo + rot_dim].set(sin_h)
    rep = LANE // period
    return tuple(jnp.tile(t, (1, rep)) * scale for t in (cos_g, sa_g, sb_g))


def _identity_tables(T, scale):
    return (jnp.full((T, LANE), scale, F32), jnp.zeros((T, LANE), F32), jnp.zeros((T, LANE), F32))


def _prep_w_in(w):
    D = w.shape[0]
    o_diff = HGRN_W
    o_cq = o_diff + 1536
    o_ckv = o_cq + C_Q_LORA
    o_kr = o_ckv + C_KV_LORA
    o_gate = o_kr + C_ROPE
    pad = jnp.zeros((D, MLA_KV_IN - C_KV_LORA - C_ROPE), w.dtype)
    return jnp.concatenate([w[:, :o_cq], w[:, o_ckv:o_kr], w[:, o_kr:o_gate], pad, w[:, o_cq:o_ckv],
                            w[:, o_gate:]], axis=1).astype(BF16)


def _prep_mla_w(w_uq, w_ukv):
    wq = w_uq.reshape(C_Q_LORA, C_HEADS, C_NOPE + C_ROPE)
    wq = jnp.pad(wq, ((0, 0), (0, 0), (0, LANE - C_NOPE - C_ROPE))).reshape(C_Q_LORA, C_HEADS * LANE)
    wkv = w_ukv.reshape(C_KV_LORA, C_HEADS, C_NOPE + C_V)
    wk = jnp.pad(wkv[:, :, :C_NOPE], ((0, 0), (0, 0), (0, LANE - C_NOPE)))
    place = jnp.zeros((C_ROPE, C_HEADS, LANE), F32).at[:, :, C_NOPE:C_NOPE + C_ROPE].set(
        jnp.broadcast_to(jnp.eye(C_ROPE, dtype=F32)[:, None, :], (C_ROPE, C_HEADS, C_ROPE)))
    wk = jnp.concatenate([wk, place, jnp.zeros((MLA_KV_IN - C_KV_LORA - C_ROPE, C_HEADS, LANE), F32)], axis=0)
    wv = jnp.pad(wkv[:, :, C_NOPE:].reshape(C_KV_LORA, C_HEADS * C_V), ((0, MLA_KV_IN - C_KV_LORA), (0, 0)))
    wkv_comb = jnp.concatenate([wk.reshape(MLA_KV_IN, C_HEADS * LANE), wv], axis=1)
    return wq.astype(BF16), wkv_comb.astype(BF16)


def kernel(x, c, ctx, c_ctx, ada_w, ada_b, norm_mix_g, norm_ffn_g, w_in, hgrn_lb, hgrn_norm_g, diff_lambda,
           diff_norm_g, mla_q_norm_g, mla_w_uq, mla_kv_norm_g, mla_w_ukv, w_branch, w_out, router_w,
           exp_w_gate, exp_w_up, exp_w_down, final_norm_g):
    B, T, D = x.shape
    Tc = ctx.shape[1]
    depth = ada_w.shape[0]
    tm_l = min(T, 512)
    tm_c = min(Tc, 256)
    tq = min(Tc, 256)
    tq_l = min(T, 2048)

    rows = ((B + 1 + SUBLANE - 1) // SUBLANE) * SUBLANE
    vecs = jnp.concatenate([c, c_ctx[None], jnp.zeros((rows - B - 1, D), F32)], axis=0)
    mods_all = _modulation(vecs, ada_w, ada_b).reshape(depth, rows, 6, D)
    lat_row = lambda b: b
    ctx_row = lambda b: B

    p = jax.nn.softmax(hgrn_lb.astype(F32), axis=0)
    lower_bounds = jnp.clip(jnp.cumsum(p, axis=0) - p[0], 0.0, LB_MAX)

    d_scale = B_HD ** -0.5 * math.log2(math.e)
    c_scale = (C_NOPE + C_ROPE) ** -0.5 * math.log2(math.e)
    dq_l = _rope_tables(T, B_HD, 0, B_HD, d_scale)
    dk_l = _rope_tables(T, B_HD, 0, B_HD, 1.0)
    dq_c, dk_c = _identity_tables(Tc, d_scale), _identity_tables(Tc, 1.0)
    mq_l = _rope_tables(T, C_ROPE, C_NOPE, LANE, c_scale)
    mk_l = _rope_tables(T, C_ROPE, C_NOPE, LANE, 1.0)
    mq_c, mk_c = _identity_tables(Tc, c_scale), _identity_tables(Tc, 1.0)

    wg, wu, wd = exp_w_gate.astype(BF16), exp_w_up.astype(BF16), exp_w_down.astype(BF16)

    xl, xc = x, ctx
    for layer in range(depth):
        with_ctx = layer < depth - 1
        mods = mods_all[layer]
        w_perm = _prep_w_in(w_in[layer])
        wq, wkv = _prep_mla_w(mla_w_uq[layer], mla_w_ukv[layer])
        wb = w_branch[layer].astype(BF16)
        wo = w_out[layer].astype(BF16)
        rw = jnp.pad(router_w[layer], ((0, 0), (0, LANE - N_EXPERTS))).astype(BF16)
        lam_init = 0.8 - 0.6 * math.exp(-0.3 * layer)
        gq = mla_q_norm_g[layer].reshape(1, -1)
        gkv = jnp.pad(mla_kv_norm_g[layer], (0, MLA_KV_IN - C_KV_LORA)).reshape(1, -1)

        hg_l, dqk_l, dv_l, ml_l, gt_l = _layer_in(xl, norm_mix_g[layer], mods, lat_row, w_perm, dq_l, dk_l, tm_l)
        hg_c, dqk_c, dv_c, ml_c, gt_c = _layer_in(xc, norm_mix_g[layer], mods, ctx_row, w_perm, dq_c, dk_c, tm_c)

        a_l, a_c = _hgrn(hg_l, hg_c, lower_bounds[layer], hgrn_norm_g[layer], with_ctx)

        b_l = _diff_attn(dqk_l, [(dqk_c, dv_c), (dqk_l, dv_l)], diff_lambda[layer], diff_norm_g[layer], lam_init, tq_l)
        mq_lat, mk_lat, mv_lat = _mla_proj(ml_l, gq, gkv, wq, wkv, mq_l, mk_l, tm_l)
        mq_ctx, mk_ctx, mv_ctx = _mla_proj(ml_c, gq, gkv, wq, wkv, mq_c, mk_c, tm_c)
        m_l = _mla_attn(mq_lat, [(mk_ctx, mv_ctx), (mk_lat, mv_lat)], tq_l)

        xl, h_l, aff_l = _merge(a_l, b_l, m_l, gt_l, wb, wo, xl, mods, lat_row, norm_ffn_g[layer], rw, tm_l)
        xl = _moe(xl, h_l, aff_l, mods, lat_row, wg, wu, wd, layer, tm_l, None if with_ctx else final_norm_g)
        if with_ctx:
            b_c = _diff_attn(dqk_c, [(dqk_c, dv_c)], diff_lambda[layer], diff_norm_g[layer], lam_init, tq)
            m_c = _mla_attn(mq_ctx, [(mk_ctx, mv_ctx)], tq)
            xc, h_c, aff_c = _merge(a_c, b_c, m_c, gt_c, wb, wo, xc, mods, ctx_row, norm_ffn_g[layer], rw, tm_c)
            xc = _moe(xc, h_c, aff_c, mods, ctx_row, wg, wu, wd, layer, tm_c)
    return xl
```
